```python
import jax
import jax.numpy as jnp
from jax import lax
import numpy as np


D_MODEL = 1024
BATCH = 4
SEQ = 4096
DEPTH = 2

GRID_W = 64
CTX_LEN = 256
MLA_HEADS = 8
MLA_NOPE = 64
MLA_ROPE = 32
MLA_QK = MLA_NOPE + MLA_ROPE
MLA_V = 64
Q_LORA = 256
KV_LORA = 128
GQA_HEADS = 8
GQA_KV_HEADS = 2
GQA_GROUP = GQA_HEADS // GQA_KV_HEADS
GQA_HD = 64
N_EXPERTS = 16
N_GROUPS = 4
EXPERTS_PER_GROUP = N_EXPERTS // N_GROUPS
TOP_K = 2
FF_EXPERT = 512
FF_SHARED = 512
ROPE_THETA = 10000.0
EPS = 1e-6
Q_BLOCK = 128
IN_SIZES = (Q_LORA, KV_LORA, MLA_ROPE, GQA_HEADS * GQA_HD, GQA_KV_HEADS * GQA_HD, GQA_KV_HEADS * GQA_HD, 2 * D_MODEL)
IN_COLS = Q_LORA + KV_LORA + MLA_ROPE + GQA_HEADS * GQA_HD + 2 * GQA_KV_HEADS * GQA_HD + 2 * D_MODEL

kernel_name = 'hybrid_mla_gqa_grouped_moe_dit_block'


def rmsnorm(x, g):
    xf = x.astype(jnp.float32)
    y = xf * lax.rsqrt(jnp.mean(xf * xf, axis=-1, keepdims=True) + EPS)
    return (y * g.astype(jnp.float32)).astype(x.dtype)


def modulate(h, shift, scale):
    return h * (1 + scale[:, None, :]) + shift[:, None, :]


def axial_angles(pos, half_dim):
    inv = ROPE_THETA ** (-jnp.arange(0, half_dim, 2, dtype=jnp.float32) / half_dim)
    return pos.astype(jnp.float32)[:, None] * inv[None, :]


def rotate(x, ang):
    cos = jnp.cos(ang)[None, :, None, :].astype(x.dtype)
    sin = jnp.sin(ang)[None, :, None, :].astype(x.dtype)
    x1, x2 = jnp.split(x, 2, axis=-1)
    return jnp.concatenate([x1 * cos - x2 * sin, x1 * sin + x2 * cos], axis=-1)


def rope_2d(x, ang_row, ang_col):
    xr, xc = jnp.split(x, 2, axis=-1)
    return jnp.concatenate([rotate(xr, ang_row), rotate(xc, ang_col)], axis=-1)


def attend(q, k, v, scale):
    b, n, hk, g, d = q.shape
    nb = n // Q_BLOCK
    qb = q.reshape(b, nb, Q_BLOCK, hk, g, d).transpose(1, 0, 2, 3, 4, 5)

    def one_block(qi):
        s = jnp.einsum('bqhgd,bkhd->bhgqk', qi, k).astype(jnp.float32) * scale
        p = jax.nn.softmax(s, axis=-1).astype(v.dtype)
        return jnp.einsum('bhgqk,bkhe->bqhge', p, v)

    o = lax.map(one_block, qb)
    return o.transpose(1, 0, 2, 3, 4, 5).reshape(b, n, hk * g * v.shape[-1])


def mixer_inputs(h, w_in_l, g_cq_l, w_uq_l, g_ckv_l, w_ukv_l, g_qa_l, g_ka_l, g_qb_l, g_kb_l, angles):
    b, n, _ = h.shape
    proj = h @ w_in_l
    splits = [int(s) for s in np.cumsum(IN_SIZES)[:-1]]
    cq, ckv, kr, qb, kb, vb, gts = jnp.split(proj, splits, axis=-1)
    qa = (rmsnorm(cq, g_cq_l) @ w_uq_l).reshape(b, n, MLA_HEADS, MLA_QK)
    kva = (rmsnorm(ckv, g_ckv_l) @ w_ukv_l).reshape(b, n, MLA_HEADS, MLA_NOPE + MLA_V)
    k_nope, va = jnp.split(kva, [MLA_NOPE], axis=-1)
    k_rope = jnp.broadcast_to(kr[:, :, None, :], (b, n, MLA_HEADS, MLA_ROPE))
    ka = jnp.concatenate([k_nope, k_rope], axis=-1)
    qa = rmsnorm(qa, g_qa_l)
    ka = rmsnorm(ka, g_ka_l)
    qb = rmsnorm(qb.reshape(b, n, GQA_HEADS, GQA_HD), g_qb_l)
    kb = rmsnorm(kb.reshape(b, n, GQA_KV_HEADS, GQA_HD), g_kb_l)
    vb = vb.reshape(b, n, GQA_KV_HEADS, GQA_HD)
    if angles is not None:
        ar_a, ac_a, ar_b, ac_b = angles
        qa = jnp.concatenate([qa[..., :MLA_NOPE], rope_2d(qa[..., MLA_NOPE:], ar_a, ac_a)], axis=-1)
        ka = jnp.concatenate([ka[..., :MLA_NOPE], rope_2d(ka[..., MLA_NOPE:], ar_a, ac_a)], axis=-1)
        qb = rope_2d(qb, ar_b, ac_b)
        kb = rope_2d(kb, ar_b, ac_b)
    qa = qa[:, :, :, None, :]
    qb = qb.reshape(b, n, GQA_KV_HEADS, GQA_GROUP, GQA_HD)
    ga, gb = jnp.split(gts, 2, axis=-1)
    return (qa, ka, va, qb, kb, vb, ga, gb)


def merge_branches(oa, ob, ga, gb, w_oa_l, w_ob_l, w_out_l):
    y = jax.nn.sigmoid(ga) * (oa @ w_oa_l) + jax.nn.sigmoid(gb) * (ob @ w_ob_l)
    return y @ w_out_l


def swiglu(h, wg, wu, wd):
    return (jax.nn.silu(h @ wg) * (h @ wu)) @ wd


def moe(h, w_router, b_router, wg, wu, wd, sg, su, sd):
    scores = jax.nn.sigmoid(jnp.einsum('bnd,de->bne', h, w_router).astype(jnp.float32))
    sel = scores + b_router.astype(jnp.float32)
    grp = sel.reshape(sel.shape[:-1] + (N_GROUPS, EXPERTS_PER_GROUP))
    grp_score = lax.top_k(grp, TOP_K)[0].sum(axis=-1)
    best = jnp.argmax(grp_score, axis=-1)
    gmask = jnp.arange(N_GROUPS)[None, None, :] == best[..., None]
    masked = jnp.where(gmask[..., None], grp, -jnp.inf).reshape(sel.shape)
    _, idx = lax.top_k(masked, TOP_K)
    w = jnp.take_along_axis(scores, idx, axis=-1)
    w = w / jnp.sum(w, axis=-1, keepdims=True)
    gate = jnp.sum(jax.nn.one_hot(idx, N_EXPERTS, dtype=jnp.float32) * w[..., None], axis=-2).astype(h.dtype)
    y = swiglu(h, sg, su, sd)
    for e in range(N_EXPERTS):
        y = y + gate[..., e:e + 1] * swiglu(h, wg[e], wu[e], wd[e])
    return y


def setup_inputs(seed: int = 0) -> dict:
    key = jax.random.key(seed)
    ks = jax.random.split(key, 32)
    f32 = jnp.float32

    def nrm(k, shape, fan_in, mult=1.0):
        return jax.random.normal(k, shape, f32) * (mult * fan_in ** -0.5)

    def gain(k, shape):
        return 1.0 + 0.02 * jax.random.normal(k, shape, f32)

    L = DEPTH
    return {
        'x': jax.random.normal(ks[0], (BATCH, SEQ, D_MODEL), f32),
        'c': jax.random.normal(ks[1], (BATCH, D_MODEL), f32),
        'ctx': jax.random.normal(ks[2], (BATCH, CTX_LEN, D_MODEL), f32),
        'c_ctx': jax.random.normal(ks[3], (D_MODEL,), f32),
        'w_mod': nrm(ks[4], (L, D_MODEL, 6 * D_MODEL), D_MODEL, 0.5),
        'b_mod': 0.02 * jax.random.normal(ks[5], (L, 6 * D_MODEL), f32),
        'g_attn': gain(ks[6], (L, D_MODEL)),
        'g_ffn': gain(ks[7], (L, D_MODEL)),
        'w_in': nrm(ks[8], (L, D_MODEL, IN_COLS), D_MODEL),
        'g_cq': gain(ks[9], (L, Q_LORA)),
        'w_uq': nrm(ks[10], (L, Q_LORA, MLA_HEADS * MLA_QK), Q_LORA),
        'g_ckv': gain(ks[11], (L, KV_LORA)),
        'w_ukv': nrm(ks[12], (L, KV_LORA, MLA_HEADS * (MLA_NOPE + MLA_V)), KV_LORA),
        'g_qa': gain(ks[13], (L, MLA_QK)),
        'g_ka': gain(ks[14], (L, MLA_QK)),
        'g_qb': gain(ks[15], (L, GQA_HD)),
        'g_kb': gain(ks[16], (L, GQA_HD)),
        'w_oa': nrm(ks[17], (L, MLA_HEADS * MLA_V, D_MODEL), MLA_HEADS * MLA_V),
        'w_ob': nrm(ks[18], (L, GQA_HEADS * GQA_HD, D_MODEL), GQA_HEADS * GQA_HD),
        'w_out': nrm(ks[19], (L, D_MODEL, D_MODEL), D_MODEL),
        'w_router': nrm(ks[20], (D_MODEL, N_EXPERTS), D_MODEL),
        'b_router': 0.01 * jax.random.normal(ks[21], (N_EXPERTS,), f32),
        'w_e_gate': nrm(ks[22], (L, N_EXPERTS, D_MODEL, FF_EXPERT), D_MODEL),
        'w_e_up': nrm(ks[23], (L, N_EXPERTS, D_MODEL, FF_EXPERT), D_MODEL),
        'w_e_down': nrm(ks[24], (L, N_EXPERTS, FF_EXPERT, D_MODEL), FF_EXPERT),
        'w_s_gate': nrm(ks[25], (L, D_MODEL, FF_SHARED), D_MODEL),
        'w_s_up': nrm(ks[26], (L, D_MODEL, FF_SHARED), D_MODEL),
        'w_s_down': nrm(ks[27], (L, FF_SHARED, D_MODEL), FF_SHARED),
    }


def reference(x, c, ctx, c_ctx, w_mod, b_mod, g_attn, g_ffn, w_in, g_cq, w_uq, g_ckv, w_ukv,
              g_qa, g_ka, g_qb, g_kb, w_oa, w_ob, w_out, w_router, b_router,
              w_e_gate, w_e_up, w_e_down, w_s_gate, w_s_up, w_s_down):
    n_lat = x.shape[1]
    n_ctx = ctx.shape[1]
    rows_n = n_lat // GRID_W
    rows = jnp.broadcast_to(jnp.arange(rows_n)[:, None], (rows_n, GRID_W)).reshape(-1)
    cols = jnp.broadcast_to(jnp.arange(GRID_W)[None, :], (rows_n, GRID_W)).reshape(-1)
    angles = (axial_angles(rows, MLA_ROPE // 2), axial_angles(cols, MLA_ROPE // 2),
              axial_angles(rows, GQA_HD // 2), axial_angles(cols, GQA_HD // 2))
    scale_a = MLA_QK ** -0.5
    scale_b = GQA_HD ** -0.5

    for l in range(DEPTH):
        last = l == DEPTH - 1
        mod_l = jax.nn.silu(c) @ w_mod[l] + b_mod[l]
        mod_c = jax.nn.silu(c_ctx)[None, :] @ w_mod[l] + b_mod[l]
        sh1_l, sc1_l, gt1_l, sh2_l, sc2_l, gt2_l = jnp.split(mod_l, 6, axis=-1)
        sh1_c, sc1_c, gt1_c, sh2_c, sc2_c, gt2_c = jnp.split(mod_c, 6, axis=-1)
        wl = (w_in[l], g_cq[l], w_uq[l], g_ckv[l], w_ukv[l], g_qa[l], g_ka[l], g_qb[l], g_kb[l])

        h_l = modulate(rmsnorm(x, g_attn[l]), sh1_l, sc1_l)
        h_c = modulate(rmsnorm(ctx, g_attn[l]), sh1_c, sc1_c)
        qa_l, ka_l, va_l, qb_l, kb_l, vb_l, ga_l, gb_l = mixer_inputs(h_l, *wl, angles)
        qa_c, ka_c, va_c, qb_c, kb_c, vb_c, ga_c, gb_c = mixer_inputs(h_c, *wl, None)
        oa_l = attend(qa_l, jnp.concatenate([ka_c, ka_l], axis=1), jnp.concatenate([va_c, va_l], axis=1), scale_a)
        ob_l = attend(qb_l, jnp.concatenate([kb_c, kb_l], axis=1), jnp.concatenate([vb_c, vb_l], axis=1), scale_b)
        x_new = x + gt1_l[:, None, :] * merge_branches(oa_l, ob_l, ga_l, gb_l, w_oa[l], w_ob[l], w_out[l])
        if not last:
            oa_c = attend(qa_c, ka_c, va_c, scale_a)
            ob_c = attend(qb_c, kb_c, vb_c, scale_b)
            ctx = ctx + gt1_c[:, None, :] * merge_branches(oa_c, ob_c, ga_c, gb_c, w_oa[l], w_ob[l], w_out[l])
        x = x_new

        e_args = (w_router, b_router, w_e_gate[l], w_e_up[l], w_e_down[l], w_s_gate[l], w_s_up[l], w_s_down[l])
        f_l = modulate(rmsnorm(x, g_ffn[l]), sh2_l, sc2_l)
        if not last:
            f_c = modulate(rmsnorm(ctx, g_ffn[l]), sh2_c, sc2_c)
            y_all = moe(jnp.concatenate([f_c, f_l], axis=1), *e_args)
            ctx = ctx + gt2_c[:, None, :] * y_all[:, :n_ctx]
            x = x + gt2_l[:, None, :] * y_all[:, n_ctx:]
        else:
            x = x + gt2_l[:, None, :] * moe(f_l, *e_args)
    return x
```

```python
import functools

import numpy as np
import jax
import jax.numpy as jnp
from jax import lax
from jax.experimental import pallas as pl
from jax.experimental.pallas import tpu as pltpu

D_MODEL = 1024
GRID_W = 64
MLA_HEADS = 8
MLA_NOPE = 64
MLA_ROPE = 32
MLA_QK = MLA_NOPE + MLA_ROPE
MLA_V = 64
Q_LORA = 256
KV_LORA = 128
GQA_HEADS = 8
GQA_KV_HEADS = 2
GQA_GROUP = GQA_HEADS // GQA_KV_HEADS
GQA_HD = 64
N_EXPERTS = 16
N_GROUPS = 4
EXPERTS_PER_GROUP = N_EXPERTS // N_GROUPS
FF_EXPERT = 512
ROPE_THETA = 10000.0
EPS = 1e-6
IN_SIZES = (Q_LORA, KV_LORA, MLA_ROPE, GQA_HEADS * GQA_HD, GQA_KV_HEADS * GQA_HD,
            GQA_KV_HEADS * GQA_HD, 2 * D_MODEL)

LANES = 128
ROW_TILE = 256
N_HEADS_ALL = MLA_HEADS + GQA_HEADS
N_KHEADS_ALL = MLA_HEADS + GQA_KV_HEADS
V_COLS = MLA_HEADS * MLA_V + GQA_KV_HEADS * LANES
VMEM_LIMIT = 56 * 1024 * 1024

C_CQ = 0
C_CKV = C_CQ + Q_LORA
C_KR = C_CKV + KV_LORA
C_QB = C_KR + LANES
C_KB = C_QB + GQA_HEADS * LANES
C_VB = C_KB + GQA_KV_HEADS * LANES
C_G = C_VB + GQA_KV_HEADS * GQA_HD
C_END = C_G + 2 * D_MODEL

BF16 = jnp.bfloat16
F32 = jnp.float32


def _dot(a, b):
    return jnp.dot(a, b, preferred_element_type=F32)


def _split_bf16(x):
    hi = x.astype(BF16)
    lo = (x - hi.astype(F32)).astype(BF16)
    return hi, lo


def _rms_rows(x, g):
    ms = jnp.mean(x * x, axis=-1, keepdims=True)
    return x * lax.rsqrt(ms + EPS) * g


def _head_norm(blk, g, dim):
    ms = jnp.sum(blk * blk, axis=-1, keepdims=True) * (1.0 / dim)
    return blk * lax.rsqrt(ms + EPS) * g


def _rope(x, cos, sin_a, sin_b, shift):
    up = pltpu.roll(x, LANES - shift, 1)
    down = pltpu.roll(x, shift, 1)
    return x * cos + up * sin_a + down * sin_b


def _mod_kernel(c_ref, w_ref, b_ref, o_ref):
    c = c_ref[...]
    s_hi, s_lo = _split_bf16(c * jax.nn.sigmoid(c))
    w_hi, w_lo = _split_bf16(w_ref[0])
    o_ref[...] = _dot(s_hi, w_hi) + _dot(s_lo, w_hi) + _dot(s_hi, w_lo) + b_ref[0]


def _mod_call(cs, w_mod, b_mod, layer):
    n_cols = w_mod.shape[2]
    tn = 1536
    return pl.pallas_call(
        _mod_kernel,
        grid=(n_cols // tn,),
        in_specs=[
            pl.BlockSpec(cs.shape, lambda j: (0, 0)),
            pl.BlockSpec((1, D_MODEL, tn), lambda j: (layer, 0, j)),
            pl.BlockSpec((1, 1, tn), lambda j: (layer, 0, j)),
        ],
        out_specs=pl.BlockSpec((cs.shape[0], tn), lambda j: (0, j)),
        out_shape=jax.ShapeDtypeStruct((cs.shape[0], n_cols), F32),
        compiler_params=pltpu.CompilerParams(
            dimension_semantics=("arbitrary",), vmem_limit_bytes=VMEM_LIMIT),
        name="mod",
    )(cs, w_mod, b_mod)


def _in_kernel(x_ref, mod_ref, gattn_ref, win_ref, gcq_ref, wuq_ref, gckv_ref, wukv_ref,
               gqa_ref, gka_ref, gqb_ref, gkb_ref,
               cosa_ref, sinaa_ref, sinba_ref, cosb_ref, sinab_ref, sinbb_ref,
               q_ref, k_ref, v_ref, g_ref):
    x = x_ref[0]
    shift = mod_ref[0, 0, 0:1, :]
    scale = mod_ref[0, 0, 1:2, :]
    h = (_rms_rows(x, gattn_ref[0]) * (1 + scale) + shift).astype(BF16)

    g_ref[0] = _dot(h, win_ref[0, :, C_G:C_END])

    cos_a, sin_aa, sin_ba = cosa_ref[...], sinaa_ref[...], sinba_ref[...]
    cos_b, sin_ab, sin_bb = cosb_ref[...], sinab_ref[...], sinbb_ref[...]
    scale_a = MLA_QK ** -0.5
    scale_b = GQA_HD ** -0.5
    half_a = MLA_ROPE // 4
    half_b = GQA_HD // 4

    p0 = _dot(h, win_ref[0, :, C_CQ:C_QB])
    cq = p0[:, C_CQ:C_CKV]
    ckv = p0[:, C_CKV:C_KR]
    kr_blk = p0[:, C_KR:C_QB]
    qa = _dot(_rms_rows(cq, gcq_ref[0]).astype(BF16), wuq_ref[0])
    kva = _dot(_rms_rows(ckv, gckv_ref[0]).astype(BF16), wukv_ref[0])
    g_qa, g_ka = gqa_ref[0], gka_ref[0]
    for hd in range(MLA_HEADS):
        qh = _head_norm(qa[:, hd * LANES:(hd + 1) * LANES], g_qa, MLA_QK)
        q_ref[0, hd] = (_rope(qh, cos_a, sin_aa, sin_ba, half_a) * scale_a).astype(BF16)
        kh = _head_norm(kva[:, hd * LANES:(hd + 1) * LANES] + kr_blk, g_ka, MLA_QK)
        k_ref[0, hd] = _rope(kh, cos_a, sin_aa, sin_ba, half_a).astype(BF16)
    n_va = MLA_HEADS * MLA_V
    v_ref[0, :, 0:n_va] = kva[:, MLA_HEADS * LANES:MLA_HEADS * LANES + n_va].astype(BF16)

    pq = _dot(h, win_ref[0, :, C_QB:C_KB])
    g_qb, g_kb = gqb_ref[0], gkb_ref[0]
    for hd in range(GQA_HEADS):
        qh = _head_norm(pq[:, hd * LANES:(hd + 1) * LANES], g_qb, GQA_HD)
        q_ref[0, MLA_HEADS + hd] = (
            _rope(qh, cos_b, sin_ab, sin_bb, half_b) * scale_b).astype(BF16)
    pk = _dot(h, win_ref[0, :, C_KB:C_G])
    for hd in range(GQA_KV_HEADS):
        kh = _head_norm(pk[:, hd * LANES:(hd + 1) * LANES], g_kb, GQA_HD)
        k_ref[0, MLA_HEADS + hd] = _rope(kh, cos_b, sin_ab, sin_bb, half_b).astype(BF16)
    vb = pk[:, GQA_KV_HEADS * LANES:GQA_KV_HEADS * LANES + LANES]
    vb_sw = pltpu.roll(vb, GQA_HD, 1)
    low = lax.broadcasted_iota(jnp.int32, vb.shape, 1) < GQA_HD
    v_ref[0, :, n_va:n_va + LANES] = jnp.where(low, vb, vb_sw).astype(BF16)
    v_ref[0, :, n_va + LANES:n_va + 2 * LANES] = jnp.where(low, vb_sw, vb).astype(BF16)


def _in_call(xs, mod1, layer, g_attn, w_in_p, g_cq, w_uq_p, g_ckv, w_ukv_p,
             g_qa, g_ka, g_qb, g_kb, tables):
    b, n, _ = xs.shape
    nt = n // ROW_TILE
    t = ROW_TILE

    def lspec(arr):
        shp = arr.shape
        return pl.BlockSpec((1,) + shp[1:], lambda i, j: (layer,) + (0,) * (len(shp) - 1))

    tab_spec = pl.BlockSpec((t, LANES), lambda i, j: (j, 0))
    return pl.pallas_call(
        _in_kernel,
        grid=(b, nt),
        in_specs=[
            pl.BlockSpec((1, t, D_MODEL), lambda i, j: (i, j, 0)),
            pl.BlockSpec((1, 1, 2, D_MODEL), lambda i, j: (i, jnp.minimum(j, 1), 0, 0)),
            lspec(g_attn), lspec(w_in_p), lspec(g_cq), lspec(w_uq_p), lspec(g_ckv),
            lspec(w_ukv_p), lspec(g_qa), lspec(g_ka), lspec(g_qb), lspec(g_kb),
        ] + [tab_spec] * 6,
        out_specs=[
            pl.BlockSpec((1, N_HEADS_ALL, t, LANES), lambda i, j: (i, 0, j, 0)),
            pl.BlockSpec((1, N_KHEADS_ALL, t, LANES), lambda i, j: (i, 0, j, 0)),
            pl.BlockSpec((1, t, V_COLS), lambda i, j: (i, j, 0)),
            pl.BlockSpec((1, t, 2 * D_MODEL), lambda i, j: (i, j, 0)),
        ],
        out_shape=[
            jax.ShapeDtypeStruct((b, N_HEADS_ALL, n, LANES), BF16),
            jax.ShapeDtypeStruct((b, N_KHEADS_ALL, n, LANES), BF16),
            jax.ShapeDtypeStruct((b, n, V_COLS), BF16),
            jax.ShapeDtypeStruct((b, n, 2 * D_MODEL), F32),
        ],
        compiler_params=pltpu.CompilerParams(
            dimension_semantics=("arbitrary", "arbitrary"), vmem_limit_bytes=VMEM_LIMIT),
        name="mixer_in",
    )(xs, mod1, g_attn, w_in_p, g_cq, w_uq_p, g_ckv, w_ukv_p, g_qa, g_ka, g_qb, g_kb, *tables)


def _attn_kernel(q_ref, ka_ref, kb_ref, v_ref, o_ref, *, n_ctx_chunks, n_chunks):
    trips = jnp.where(pl.program_id(2) == 0, n_ctx_chunks, n_chunks)
    q0 = q_ref[0, 0]
    q1 = q_ref[0, 1]
    rows = q0.shape[0]

    def one_head(q, k, v, m, l, acc):
        s = lax.dot_general(q, k, (((1,), (1,)), ((), ())), preferred_element_type=F32)
        m_new = jnp.maximum(m, jnp.max(s, axis=-1, keepdims=True))
        p = jnp.exp(s - m_new)
        alpha = jnp.exp(m - m_new)
        l = alpha * l + jnp.sum(p, axis=-1, keepdims=True)
        acc = alpha * acc + _dot(p.astype(BF16), v)
        return m_new, l, acc

    def body(c, carry):
        m0, l0, a0, m1, l1, a1 = carry
        off = pl.multiple_of(c * ROW_TILE, ROW_TILE)
        v = v_ref[0, pl.ds(off, ROW_TILE), :]
        m0, l0, a0 = one_head(q0, ka_ref[0, 0, pl.ds(off, ROW_TILE), :], v, m0, l0, a0)
        m1, l1, a1 = one_head(q1, kb_ref[0, 0, pl.ds(off, ROW_TILE), :], v, m1, l1, a1)
        return m0, l0, a0, m1, l1, a1

    m_init = jnp.full((rows, 1), -1e30, F32)
    l_init = jnp.zeros((rows, 1), F32)
    a_init = jnp.zeros((rows, LANES), F32)
    _, l0, a0, _, l1, a1 = lax.fori_loop(
        0, trips, body, (m_init, l_init, a_init, m_init, l_init, a_init))
    low = lax.broadcasted_iota(jnp.int32, (rows, LANES), 1) < MLA_V
    o_ref[0] = jnp.where(low, a0 / l0, a1 / l1).astype(o_ref.dtype)


def _attn_call(q_all, k_all, v_all, n_ctx):
    b, _, n, _ = q_all.shape
    t = ROW_TILE
    n_pairs = N_HEADS_ALL // 2
    mla_pairs = MLA_HEADS // 2
    pairs_per_kv = GQA_GROUP // 2

    def ka_map(i, p, j):
        return (i, jnp.where(p < mla_pairs, 2 * p, MLA_HEADS + (p - mla_pairs) // pairs_per_kv), 0, 0)

    def kb_map(i, p, j):
        return (i, jnp.where(p < mla_pairs, 2 * p + 1, MLA_HEADS + (p - mla_pairs) // pairs_per_kv), 0, 0)

    def v_map(i, p, j):
        return (i, 0, jnp.where(p < mla_pairs, p, mla_pairs + (p - mla_pairs) // pairs_per_kv))

    kern = functools.partial(_attn_kernel, n_ctx_chunks=n_ctx // t, n_chunks=n // t)
    return pl.pallas_call(
        kern,
        grid=(b, n_pairs, n // t),
        in_specs=[
            pl.BlockSpec((1, 2, t, LANES), lambda i, p, j: (i, p, j, 0)),
            pl.BlockSpec((1, 1, n, LANES), ka_map),
            pl.BlockSpec((1, 1, n, LANES), kb_map),
            pl.BlockSpec((1, n, LANES), v_map),
        ],
        out_specs=pl.BlockSpec((1, t, LANES), lambda i, p, j: (i, j, p)),
        out_shape=jax.ShapeDtypeStruct((b, n, n_pairs * LANES), BF16),
        compiler_params=pltpu.CompilerParams(
            dimension_semantics=("arbitrary", "arbitrary", "arbitrary"),
            vmem_limit_bytes=VMEM_LIMIT),
        name="attention",
    )(q_all, k_all, k_all, v_all)


def _route_rows(sel, scores):
    rows = [sel[e:e + 1, :] for e in range(N_EXPERTS)]
    srow = [scores[e:e + 1, :] for e in range(N_EXPERTS)]
    grp = []
    for g in range(N_GROUPS):
        a, b, c, d = rows[4 * g:4 * g + 4]
        hi1, lo1 = jnp.maximum(a, b), jnp.minimum(a, b)
        hi2, lo2 = jnp.maximum(c, d), jnp.minimum(c, d)
        top1 = jnp.maximum(hi1, hi2)
        top2 = jnp.maximum(jnp.minimum(hi1, hi2), jnp.maximum(lo1, lo2))
        grp.append(top1 + top2)
    best = jnp.zeros_like(grp[0], dtype=jnp.int32)
    best_v = grp[0]
    for g in range(1, N_GROUPS):
        upd = grp[g] > best_v
        best = jnp.where(upd, g, best)
        best_v = jnp.where(upd, grp[g], best_v)
    picked = []
    for e in range(N_EXPERTS):
        g = e // EXPERTS_PER_GROUP
        rank = jnp.zeros_like(best)
        for o in range(4 * g, 4 * g + 4):
            if o == e:
                continue
            ahead = rows[o] > rows[e]
            if o < e:
                ahead = ahead | (rows[o] == rows[e])
            rank = rank + ahead.astype(jnp.int32)
        keep = (rank < 2) & (best == g)
        picked.append(jnp.where(keep, srow[e], 0.0))
    total = picked[0]
    for e in range(1, N_EXPERTS):
        total = total + picked[e]
    return [p / total for p in picked]


def _mid_kernel(o_ref, g_ref, x_ref, mod_ref, woa_ref, wob_ref, wout_ref, gffn_ref,
                wrh_ref, wrl_ref, brc_ref, xn_ref, f_ref, gate_ref):
    o = o_ref[0]
    n_a = MLA_HEADS * MLA_V
    ya = _dot(o[:, 0:n_a], woa_ref[0])
    yb = _dot(o[:, n_a:], wob_ref[0])
    g = g_ref[0]
    y = jax.nn.sigmoid(g[:, 0:D_MODEL]) * ya + jax.nn.sigmoid(g[:, D_MODEL:]) * yb
    z = _dot(y.astype(BF16), wout_ref[0])
    gt1 = mod_ref[0, 0, 0:1, :]
    sh2 = mod_ref[0, 0, 1:2, :]
    sc2 = mod_ref[0, 0, 2:3, :]
    xn = x_ref[0] + gt1 * z
    xn_ref[0] = xn
    f = _rms_rows(xn, gffn_ref[0]) * (1 + sc2) + sh2
    f_hi, f_lo = _split_bf16(f)
    f_ref[0] = f_hi
    w_hi = wrh_ref[...]
    logits = _dot(f_hi, w_hi) + _dot(f_lo, w_hi) + _dot(f_hi, wrl_ref[...])
    scores_t = jax.nn.sigmoid(logits).T
    scores = scores_t[0:N_EXPERTS, :]
    sel = scores + brc_ref[0:N_EXPERTS, :]
    gates = _route_rows(sel, scores)
    t = scores.shape[1]
    gates.append(jnp.ones((1, t), F32))
    gates.append(jnp.zeros((LANES - N_EXPERTS - 1, t), F32))
    gate_ref[0] = jnp.concatenate(gates, axis=0).T


def _mid_call(o_all, gts, xs, modm, layer, w_oa, w_ob, w_out, g_ffn, wr_hi, wr_lo, br_col):
    b, n, _ = xs.shape
    t = ROW_TILE

    def lspec(arr):
        shp = arr.shape
        return pl.BlockSpec((1,) + shp[1:], lambda i, j: (layer,) + (0,) * (len(shp) - 1))

    def full(arr):
        return pl.BlockSpec(arr.shape, lambda i, j: (0,) * arr.ndim)

    return pl.pallas_call(
        _mid_kernel,
        grid=(b, n // t),
        in_specs=[
            pl.BlockSpec((1, t, o_all.shape[2]), lambda i, j: (i, j, 0)),
            pl.BlockSpec((1, t, 2 * D_MODEL), lambda i, j: (i, j, 0)),
            pl.BlockSpec((1, t, D_MODEL), lambda i, j: (i, j, 0)),
            pl.BlockSpec((1, 1, 3, D_MODEL), lambda i, j: (i, jnp.minimum(j, 1), 0, 0)),
            lspec(w_oa), lspec(w_ob), lspec(w_out), lspec(g_ffn),
            full(wr_hi), full(wr_lo), full(br_col),
        ],
        out_specs=[
            pl.BlockSpec((1, t, D_MODEL), lambda i, j: (i, j, 0)),
            pl.BlockSpec((1, t, D_MODEL), lambda i, j: (i, j, 0)),
            pl.BlockSpec((1, t, LANES), lambda i, j: (i, j, 0)),
        ],
        out_shape=[
            jax.ShapeDtypeStruct((b, n, D_MODEL), F32),
            jax.ShapeDtypeStruct((b, n, D_MODEL), BF16),
            jax.ShapeDtypeStruct((b, n, LANES), F32),
        ],
        compiler_params=pltpu.CompilerParams(
            dimension_semantics=("arbitrary", "arbitrary"), vmem_limit_bytes=VMEM_LIMIT),
        name="mixer_out_router",
    )(o_all, gts, xs, modm, w_oa, w_ob, w_out, g_ffn, wr_hi, wr_lo, br_col)


def _moe_kernel(f_ref, gate_ref, x_ref, gt_ref, wg_ref, wu_ref, wd_ref, o_ref, acc_ref, *, n_ctx):
    e = pl.program_id(2)
    last = pl.num_programs(2) - 1

    @pl.when(e == 0)
    def _():
        acc_ref[...] = jnp.zeros_like(acc_ref)

    h = f_ref[0]
    a = jax.nn.silu(_dot(h, wg_ref[0, 0])) * _dot(h, wu_ref[0, 0])
    y = _dot(a.astype(BF16), wd_ref[0, 0])
    gates = gate_ref[0]
    lane = lax.broadcasted_iota(jnp.int32, gates.shape, 1)
    gcol = jnp.sum(jnp.where(lane == e, gates, 0.0), axis=-1, keepdims=True)
    acc_ref[...] += gcol * y

    @pl.when(e == last)
    def _():
        rows = acc_ref.shape[0]
        row = lax.broadcasted_iota(jnp.int32, (rows, 1), 0) + pl.program_id(1) * rows
        gt = jnp.where(row < n_ctx, gt_ref[0, 0:1, :], gt_ref[0, 1:2, :])
        o_ref[0] = x_ref[0] + gt * acc_ref[...]


def _moe_call(f, gate, xn, gt2, layer, wg_all, wu_all, wd_all, n_ctx):
    b, n, _ = xn.shape
    tiles = 4
    tm = n // tiles
    n_e = wg_all.shape[1]
    kern = functools.partial(_moe_kernel, n_ctx=n_ctx)
    return pl.pallas_call(
        kern,
        grid=(b, tiles, n_e),
        in_specs=[
            pl.BlockSpec((1, tm, D_MODEL), lambda i, j, e: (i, j, 0)),
            pl.BlockSpec((1, tm, LANES), lambda i, j, e: (i, j, 0)),
            pl.BlockSpec((1, tm, D_MODEL), lambda i, j, e: (i, j, 0)),
            pl.BlockSpec((1, 2, D_MODEL), lambda i, j, e: (i, 0, 0)),
            pl.BlockSpec((1, 1, D_MODEL, FF_EXPERT), lambda i, j, e: (layer, e, 0, 0)),
            pl.BlockSpec((1, 1, D_MODEL, FF_EXPERT), lambda i, j, e: (layer, e, 0, 0)),
            pl.BlockSpec((1, 1, FF_EXPERT, D_MODEL), lambda i, j, e: (layer, e, 0, 0)),
        ],
        out_specs=pl.BlockSpec((1, tm, D_MODEL), lambda i, j, e: (i, j, 0)),
        out_shape=jax.ShapeDtypeStruct((b, n, D_MODEL), F32),
        scratch_shapes=[pltpu.VMEM((tm, D_MODEL), F32)],
        compiler_params=pltpu.CompilerParams(
            dimension_semantics=("arbitrary", "arbitrary", "arbitrary"),
            vmem_limit_bytes=VMEM_LIMIT),
        name="moe",
    )(f, gate, xn, gt2, wg_all, wu_all, wd_all)


def _pad_heads(w, n_heads, dim):
    lead = w.shape[:-1]
    w = w.reshape(lead + (n_heads, dim))
    w = jnp.pad(w, [(0, 0)] * len(lead) + [(0, 0), (0, LANES - dim)])
    return w.reshape(lead + (n_heads * LANES,))


def _prep_w_in(w_in):
    offs = np.concatenate([[0], np.cumsum(IN_SIZES)])
    cq, ckv, kr, qb, kb, vb, gts = [w_in[..., int(offs[i]):int(offs[i + 1])] for i in range(7)]
    kr_blk = jnp.pad(kr, [(0, 0), (0, 0), (MLA_NOPE, LANES - MLA_QK)])
    return jnp.concatenate(
        [cq, ckv, kr_blk, _pad_heads(qb, GQA_HEADS, GQA_HD), _pad_heads(kb, GQA_KV_HEADS, GQA_HD),
         vb, gts], axis=-1).astype(BF16)


def _prep_w_ukv(w_ukv):
    l, r, _ = w_ukv.shape
    w = w_ukv.reshape(l, r, MLA_HEADS, MLA_NOPE + MLA_V)
    uk = _pad_heads(w[..., :MLA_NOPE].reshape(l, r, MLA_HEADS * MLA_NOPE), MLA_HEADS, MLA_NOPE)
    uv = w[..., MLA_NOPE:].reshape(l, r, MLA_HEADS * MLA_V)
    return jnp.concatenate([uk, uv], axis=-1).astype(BF16)


def _pad_gain(g, dim):
    return jnp.pad(g, [(0, 0), (0, LANES - dim)])[:, None, :]


def _rope_tables(n_ctx, n_lat, lane0, half_dim):
    pos = jnp.arange(n_lat)
    quarter = half_dim // 2
    inv = ROPE_THETA ** (-jnp.arange(0, half_dim, 2, dtype=F32) / half_dim)
    ang_r = (pos // GRID_W).astype(F32)[:, None] * inv[None, :]
    ang_c = (pos % GRID_W).astype(F32)[:, None] * inv[None, :]
    cos = jnp.concatenate([jnp.cos(ang_r)] * 2 + [jnp.cos(ang_c)] * 2, axis=-1)
    zero = jnp.zeros((n_lat, quarter), F32)
    sin_a = jnp.concatenate([-jnp.sin(ang_r), zero, -jnp.sin(ang_c), zero], axis=-1)
    sin_b = jnp.concatenate([zero, jnp.sin(ang_r), zero, jnp.sin(ang_c)], axis=-1)
    pad = [(n_ctx, 0), (lane0, LANES - lane0 - 2 * half_dim)]
    return (jnp.pad(cos, pad, constant_values=1.0), jnp.pad(sin_a, pad), jnp.pad(sin_b, pad))


def kernel(x, c, ctx, c_ctx, w_mod, b_mod, g_attn, g_ffn, w_in, g_cq, w_uq, g_ckv, w_ukv,
           g_qa, g_ka, g_qb, g_kb, w_oa, w_ob, w_out, w_router, b_router,
           w_e_gate, w_e_up, w_e_down, w_s_gate, w_s_up, w_s_down):
    b, n_lat, d = x.shape
    n_ctx = ctx.shape[1]
    depth = w_mod.shape[0]
    assert d == D_MODEL and n_ctx == ROW_TILE and n_lat % ROW_TILE == 0

    xs = jnp.concatenate([ctx, x], axis=1)
    cs = jnp.concatenate([c, c_ctx[None, :], jnp.zeros((8 - b - 1, d), F32)], axis=0)

    w_in_p = _prep_w_in(w_in)
    w_uq_p = _pad_heads(w_uq, MLA_HEADS, MLA_QK).astype(BF16)
    w_ukv_p = _prep_w_ukv(w_ukv)
    g3 = lambda g: g[:, None, :]
    g_qa_p, g_ka_p = _pad_gain(g_qa, MLA_QK), _pad_gain(g_ka, MLA_QK)
    g_qb_p, g_kb_p = _pad_gain(g_qb, GQA_HD), _pad_gain(g_kb, GQA_HD)
    tables = (_rope_tables(n_ctx, n_lat, MLA_NOPE, MLA_ROPE // 2)
              + _rope_tables(n_ctx, n_lat, 0, GQA_HD // 2))
    w_oa_b, w_ob_b, w_out_b = w_oa.astype(BF16), w_ob.astype(BF16), w_out.astype(BF16)
    wr_p = jnp.pad(w_router, [(0, 0), (0, LANES - N_EXPERTS)])
    wr_hi = wr_p.astype(BF16)
    wr_lo = (wr_p - wr_hi.astype(F32)).astype(BF16)
    br_col = jnp.pad(b_router, (0, LANES - N_EXPERTS))[:, None]
    wg_all = jnp.concatenate([w_e_gate, w_s_gate[:, None]], axis=1).astype(BF16)
    wu_all = jnp.concatenate([w_e_up, w_s_up[:, None]], axis=1).astype(BF16)
    wd_all = jnp.concatenate([w_e_down, w_s_down[:, None]], axis=1).astype(BF16)
    b_mod3 = b_mod[:, None, :]

    for layer in range(depth):
        mods = _mod_call(cs, w_mod, b_mod3, layer).reshape(8, 6, d)
        lat, cx = mods[:b], jnp.broadcast_to(mods[b:b + 1], (b, 6, d))
        both = jnp.stack([cx, lat], axis=1)
        mod1, modm, gt2 = both[:, :, 0:2], both[:, :, 2:5], both[:, :, 5]
        q_all, k_all, v_all, gts = _in_call(
            xs, mod1, layer, g3(g_attn), w_in_p, g3(g_cq), w_uq_p, g3(g_ckv), w_ukv_p,
            g_qa_p, g_ka_p, g_qb_p, g_kb_p, tables)
        o_all = _attn_call(q_all, k_all, v_all, n_ctx)
        xn, f, gate = _mid_call(o_all, gts, xs, modm, layer, w_oa_b, w_ob_b, w_out_b,
                                g3(g_ffn), wr_hi, wr_lo, br_col)
        xs = _moe_call(f, gate, xn, gt2, layer, wg_all, wu_all, wd_all, n_ctx)
    return xs[:, n_ctx:]
```

```python
import functools

import numpy as np
import jax
import jax.numpy as jnp
from jax import lax
from jax.experimental import pallas as pl
from jax.experimental.pallas import tpu as pltpu

D_MODEL = 1024
GRID_W = 64
MLA_HEADS = 8
MLA_NOPE = 64
MLA_ROPE = 32
MLA_QK = MLA_NOPE + MLA_ROPE
MLA_V = 64
Q_LORA = 256
KV_LORA = 128
GQA_HEADS = 8
GQA_KV_HEADS = 2
GQA_GROUP = GQA_HEADS // GQA_KV_HEADS
GQA_HD = 64
N_EXPERTS = 16
N_GROUPS = 4
EXPERTS_PER_GROUP = N_EXPERTS // N_GROUPS
FF_EXPERT = 512
ROPE_THETA = 10000.0
EPS = 1e-6
IN_SIZES = (Q_LORA, KV_LORA, MLA_ROPE, GQA_HEADS * GQA_HD, GQA_KV_HEADS * GQA_HD,
            GQA_KV_HEADS * GQA_HD, 2 * D_MODEL)

LANES = 128
ROW_TILE = 256
Q_TILE = 512
KV_CHUNK = 512
N_HEADS_ALL = MLA_HEADS + GQA_HEADS
N_KHEADS_ALL = MLA_HEADS + GQA_KV_HEADS
LOG2_E = 1.4426950408889634
ONES_ROW = MLA_V
VMEM_LIMIT = 56 * 1024 * 1024

C_CQ = 0
C_CKV = C_CQ + Q_LORA
C_KR = C_CKV + KV_LORA
C_QB = C_KR + LANES
C_KB = C_QB + GQA_HEADS * LANES
C_G = C_KB + GQA_KV_HEADS * LANES
C_END = C_G + 2 * D_MODEL

BF16 = jnp.bfloat16
F32 = jnp.float32


def _dot(a, b):
    return jnp.dot(a, b, preferred_element_type=F32)


def _dot_nt(a, b):
    return lax.dot_general(a, b, (((1,), (1,)), ((), ())), preferred_element_type=F32)


def _split_bf16(x):
    hi = x.astype(BF16)
    lo = (x - hi.astype(F32)).astype(BF16)
    return hi, lo


def _rms_rows(x, g):
    ms = jnp.mean(x * x, axis=-1, keepdims=True)
    return x * lax.rsqrt(ms + EPS) * g


def _head_norm(blk, g, dim):
    ms = jnp.sum(blk * blk, axis=-1, keepdims=True) * (1.0 / dim)
    return blk * lax.rsqrt(ms + EPS) * g


def _rope(x, cos, sin_a, sin_b, shift):
    up = pltpu.roll(x, LANES - shift, 1)
    down = pltpu.roll(x, shift, 1)
    return x * cos + up * sin_a + down * sin_b


def _mod_kernel(c_ref, w_ref, b_ref, o_ref):
    c = c_ref[...]
    s_hi, s_lo = _split_bf16(c * jax.nn.sigmoid(c))
    w_hi, w_lo = _split_bf16(w_ref[0])
    o_ref[...] = _dot(s_hi, w_hi) + _dot(s_lo, w_hi) + _dot(s_hi, w_lo) + b_ref[0]


def _mod_call(cs, w_mod, b_mod, layer):
    n_cols = w_mod.shape[2]
    tn = 1536
    return pl.pallas_call(
        _mod_kernel,
        grid=(n_cols // tn,),
        in_specs=[
            pl.BlockSpec(cs.shape, lambda j: (0, 0)),
            pl.BlockSpec((1, D_MODEL, tn), lambda j: (layer, 0, j)),
            pl.BlockSpec((1, 1, tn), lambda j: (layer, 0, j)),
        ],
        out_specs=pl.BlockSpec((cs.shape[0], tn), lambda j: (0, j)),
        out_shape=jax.ShapeDtypeStruct((cs.shape[0], n_cols), F32),
        compiler_params=pltpu.CompilerParams(
            dimension_semantics=("arbitrary",), vmem_limit_bytes=VMEM_LIMIT),
        name="mod",
    )(cs, w_mod, b_mod)


def _in_kernel(x_ref, mod_ref, gattn_ref, win_ref, gcq_ref, wuq_ref, gckv_ref, wuk_ref, wuvt_ref,
               wvbt_ref, gqa_ref, gka_ref, gqb_ref, gkb_ref,
               cosa_ref, sinaa_ref, sinba_ref, cosb_ref, sinab_ref, sinbb_ref,
               q_ref, k_ref, v_ref, g_ref):
    x = x_ref[0]
    shift = mod_ref[0, 0, 0:1, :]
    scale = mod_ref[0, 0, 1:2, :]
    h = (_rms_rows(x, gattn_ref[0]) * (1 + scale) + shift).astype(BF16)

    g_ref[0] = _dot(h, win_ref[0, :, C_G:C_END])

    cos_a, sin_aa, sin_ba = cosa_ref[...], sinaa_ref[...], sinba_ref[...]
    cos_b, sin_ab, sin_bb = cosb_ref[...], sinab_ref[...], sinbb_ref[...]
    scale_a = MLA_QK ** -0.5 * LOG2_E
    scale_b = GQA_HD ** -0.5 * LOG2_E
    half_a = MLA_ROPE // 4
    half_b = GQA_HD // 4
    t = x.shape[0]
    ones_row = lax.broadcasted_iota(jnp.int32, (LANES, t), 0) == ONES_ROW

    p0 = _dot(h, win_ref[0, :, C_CQ:C_QB])
    cq = p0[:, C_CQ:C_CKV]
    ckv = p0[:, C_CKV:C_KR]
    kr_blk = p0[:, C_KR:C_QB]
    qa = _dot(_rms_rows(cq, gcq_ref[0]).astype(BF16), wuq_ref[0])
    ckv_n = _rms_rows(ckv, gckv_ref[0]).astype(BF16)
    ka = _dot(ckv_n, wuk_ref[0])
    va_t = _dot_nt(wuvt_ref[0], ckv_n)
    g_qa, g_ka = gqa_ref[0], gka_ref[0]
    for hd in range(MLA_HEADS):
        qh = _head_norm(qa[:, hd * LANES:(hd + 1) * LANES], g_qa, MLA_QK)
        q_ref[0, hd] = (_rope(qh, cos_a, sin_aa, sin_ba, half_a) * scale_a).astype(BF16)
        kh = _head_norm(ka[:, hd * LANES:(hd + 1) * LANES] + kr_blk, g_ka, MLA_QK)
        k_ref[0, hd] = _rope(kh, cos_a, sin_aa, sin_ba, half_a).astype(BF16)
        v_ref[0, hd, 0] = jnp.where(ones_row, 1.0, va_t[hd * LANES:(hd + 1) * LANES, :]).astype(BF16)

    pq = _dot(h, win_ref[0, :, C_QB:C_KB])
    g_qb, g_kb = gqb_ref[0], gkb_ref[0]
    for hd in range(GQA_HEADS):
        qh = _head_norm(pq[:, hd * LANES:(hd + 1) * LANES], g_qb, GQA_HD)
        q_ref[0, MLA_HEADS + hd] = (
            _rope(qh, cos_b, sin_ab, sin_bb, half_b) * scale_b).astype(BF16)
    pk = _dot(h, win_ref[0, :, C_KB:C_G])
    vb_t = _dot_nt(wvbt_ref[0], h)
    for hd in range(GQA_KV_HEADS):
        kh = _head_norm(pk[:, hd * LANES:(hd + 1) * LANES], g_kb, GQA_HD)
        k_ref[0, MLA_HEADS + hd] = _rope(kh, cos_b, sin_ab, sin_bb, half_b).astype(BF16)
        v_ref[0, MLA_HEADS + hd, 0] = jnp.where(
            ones_row, 1.0, vb_t[hd * LANES:(hd + 1) * LANES, :]).astype(BF16)


def _in_call(xs, mod1, layer, n_lat, g_attn, w_in_p, g_cq, w_uq_p, g_ckv, w_uk_p, w_uvt, w_vbt,
             g_qa, g_ka, g_qb, g_kb, tables):
    b, n, _ = xs.shape
    nt = n // ROW_TILE
    n_lat_tiles = n_lat // ROW_TILE
    t = ROW_TILE

    def lspec(arr):
        shp = arr.shape
        return pl.BlockSpec((1,) + shp[1:], lambda i, j: (layer,) + (0,) * (len(shp) - 1))

    tab_spec = pl.BlockSpec((t, LANES), lambda i, j: (j, 0))
    return pl.pallas_call(
        _in_kernel,
        grid=(b, nt),
        in_specs=[
            pl.BlockSpec((1, t, D_MODEL), lambda i, j: (i, j, 0)),
            pl.BlockSpec((1, 1, 2, D_MODEL), lambda i, j: (i, jnp.where(j >= n_lat_tiles, 1, 0), 0, 0)),
            lspec(g_attn), lspec(w_in_p), lspec(g_cq), lspec(w_uq_p), lspec(g_ckv),
            lspec(w_uk_p), lspec(w_uvt), lspec(w_vbt), lspec(g_qa), lspec(g_ka), lspec(g_qb),
            lspec(g_kb),
        ] + [tab_spec] * 6,
        out_specs=[
            pl.BlockSpec((1, N_HEADS_ALL, t, LANES), lambda i, j: (i, 0, j, 0)),
            pl.BlockSpec((1, N_KHEADS_ALL, t, LANES), lambda i, j: (i, 0, j, 0)),
            pl.BlockSpec((1, N_KHEADS_ALL, 1, LANES, t), lambda i, j: (i, 0, j, 0, 0)),
            pl.BlockSpec((1, t, 2 * D_MODEL), lambda i, j: (i, j, 0)),
        ],
        out_shape=[
            jax.ShapeDtypeStruct((b, N_HEADS_ALL, n, LANES), BF16),
            jax.ShapeDtypeStruct((b, N_KHEADS_ALL, n, LANES), BF16),
            jax.ShapeDtypeStruct((b, N_KHEADS_ALL, nt, LANES, t), BF16),
            jax.ShapeDtypeStruct((b, n, 2 * D_MODEL), F32),
        ],
        compiler_params=pltpu.CompilerParams(
            dimension_semantics=("arbitrary", "arbitrary"), vmem_limit_bytes=VMEM_LIMIT),
        name="mixer_in",
    )(xs, mod1, g_attn, w_in_p, g_cq, w_uq_p, g_ckv, w_uk_p, w_uvt, w_vbt,
      g_qa, g_ka, g_qb, g_kb, *tables)


def _attn_kernel(q_ref, k0_ref, k1_ref, v0_ref, v1_ref, *rest, n_main, tail_chunk, pipelined):
    if pipelined:
        o_ref, s_scr, acc_scr = rest[-3:]
    else:
        o_ref, acc_scr = rest[-2:]
    qs = (q_ref[0, 0], q_ref[0, 1])
    k_refs = (k0_ref, k1_ref)
    v_refs = (v0_ref, v1_ref)
    tq = qs[0].shape[0]
    per = KV_CHUNK // ROW_TILE

    def fold(hh, m, cmax, st, chunk0, n_sub, first=False):
        m_new = cmax if first else jnp.maximum(m, cmax)
        pt = jnp.exp2(st - m_new).astype(BF16)
        pv = _dot(v_refs[hh][0, 0, chunk0], pt[0:ROW_TILE])
        for u in range(1, n_sub):
            pv = pv + _dot(v_refs[hh][0, 0, chunk0 + u], pt[u * ROW_TILE:(u + 1) * ROW_TILE])
        if first:
            acc_scr[hh] = pv
        else:
            acc_scr[hh] = jnp.exp2(m - m_new) * acc_scr[hh] + pv
        return m_new

    def produce(c, slot):
        cmax = []
        for hh in range(2):
            off = c * KV_CHUNK
            if not isinstance(off, int):
                off = pl.multiple_of(off, KV_CHUNK)
            st = _dot_nt(k_refs[hh][0, 0, pl.ds(off, KV_CHUNK), :], qs[hh])
            s_scr[slot, hh] = st
            cmax.append(jnp.max(st, axis=0, keepdims=True))
        return tuple(cmax)

    def consume(c, slot, ms, cmax):
        return tuple(fold(hh, ms[hh], cmax[hh], s_scr[slot, hh], c * per, per) for hh in range(2))

    ms = []
    for hh in range(2):
        st = _dot_nt(k_refs[hh][0, 0, pl.ds(tail_chunk * ROW_TILE, ROW_TILE), :], qs[hh])
        ms.append(fold(hh, None, jnp.max(st, axis=0, keepdims=True), st, tail_chunk, 1, first=True))
    ms = tuple(ms)

    if pipelined:
        def body(i, carry):
            ms, cmax = carry
            cmax1 = produce(2 * i + 1, 1)
            ms = consume(2 * i, 0, ms, cmax)
            cmax2 = produce(2 * i + 2, 0)
            ms = consume(2 * i + 1, 1, ms, cmax1)
            return ms, cmax2

        ms, cmax = lax.fori_loop(0, (n_main - 2) // 2, body, (ms, produce(0, 0)))
        cmax1 = produce(n_main - 1, 1)
        ms = consume(n_main - 2, 0, ms, cmax)
        ms = consume(n_main - 1, 1, ms, cmax1)

    outs = []
    for hh in range(2):
        acc = acc_scr[hh]
        outs.append((acc / acc[ONES_ROW:ONES_ROW + 1, :]).T)
    low = lax.broadcasted_iota(jnp.int32, (tq, LANES), 1) < MLA_V
    o_ref[0] = jnp.where(low, outs[0], pltpu.roll(outs[1], MLA_V, 1)).astype(o_ref.dtype)


def _kv_head(p):
    mla_pairs = MLA_HEADS // 2
    pairs_per_kv = GQA_GROUP // 2
    gqa = MLA_HEADS + (p - mla_pairs) // pairs_per_kv
    return jnp.where(p < mla_pairs, 2 * p, gqa), jnp.where(p < mla_pairs, 2 * p + 1, gqa)


def _attn_lat_call(q_all, k_all, v_all, n_lat):
    b, _, n, _ = q_all.shape
    n_pairs = N_HEADS_ALL // 2
    n_chunks = n // ROW_TILE
    n_main = n_lat // KV_CHUNK
    assert n_main >= 2 and n_main % 2 == 0
    kern = functools.partial(_attn_kernel, n_main=n_main, tail_chunk=n_lat // ROW_TILE,
                             pipelined=True)
    k_spec = lambda which: pl.BlockSpec(
        (1, 1, n, LANES), lambda i, p, j: (i, _kv_head(p)[which], 0, 0))
    v_spec = lambda which: pl.BlockSpec(
        (1, 1, n_chunks, LANES, ROW_TILE), lambda i, p, j: (i, _kv_head(p)[which], 0, 0, 0))
    return pl.pallas_call(
        kern,
        grid=(b, n_pairs, n_lat // Q_TILE),
        in_specs=[pl.BlockSpec((1, 2, Q_TILE, LANES), lambda i, p, j: (i, p, j, 0)),
                  k_spec(0), k_spec(1), v_spec(0), v_spec(1)],
        out_specs=pl.BlockSpec((1, Q_TILE, LANES), lambda i, p, j: (i, j, p)),
        out_shape=jax.ShapeDtypeStruct((b, n, n_pairs * LANES), BF16),
        scratch_shapes=[pltpu.VMEM((2, 2, KV_CHUNK, Q_TILE), F32),
                        pltpu.VMEM((2, LANES, Q_TILE), F32)],
        compiler_params=pltpu.CompilerParams(
            dimension_semantics=("arbitrary", "arbitrary", "arbitrary"),
            vmem_limit_bytes=VMEM_LIMIT),
        name="attention",
    )(q_all, k_all, k_all, v_all, v_all)


def _attn_ctx_call(q_all, k_all, v_all, o_all, n_lat):
    b, _, n, _ = q_all.shape
    n_pairs = N_HEADS_ALL // 2
    c = n_lat // ROW_TILE
    assert n - n_lat == ROW_TILE
    kern = functools.partial(_attn_kernel, n_main=0, tail_chunk=0, pipelined=False)
    k_spec = lambda which: pl.BlockSpec(
        (1, 1, ROW_TILE, LANES), lambda i, p: (i, _kv_head(p)[which], c, 0))
    v_spec = lambda which: pl.BlockSpec(
        (1, 1, 1, LANES, ROW_TILE), lambda i, p: (i, _kv_head(p)[which], c, 0, 0))
    return pl.pallas_call(
        kern,
        grid=(b, n_pairs),
        in_specs=[pl.BlockSpec((1, 2, ROW_TILE, LANES), lambda i, p: (i, p, c, 0)),
                  k_spec(0), k_spec(1), v_spec(0), v_spec(1),
                  pl.BlockSpec(memory_space=pl.ANY)],
        out_specs=pl.BlockSpec((1, ROW_TILE, LANES), lambda i, p: (i, c, p)),
        out_shape=jax.ShapeDtypeStruct(o_all.shape, o_all.dtype),
        scratch_shapes=[pltpu.VMEM((2, LANES, ROW_TILE), F32)],
        input_output_aliases={5: 0},
        compiler_params=pltpu.CompilerParams(
            dimension_semantics=("arbitrary", "arbitrary"), vmem_limit_bytes=VMEM_LIMIT),
        name="attention_ctx",
    )(q_all, k_all, k_all, v_all, v_all, o_all)


def _route_rows(sel, scores):
    rows = [sel[e:e + 1, :] for e in range(N_EXPERTS)]
    srow = [scores[e:e + 1, :] for e in range(N_EXPERTS)]
    grp = []
    for g in range(N_GROUPS):
        a, b, c, d = rows[4 * g:4 * g + 4]
        hi1, lo1 = jnp.maximum(a, b), jnp.minimum(a, b)
        hi2, lo2 = jnp.maximum(c, d), jnp.minimum(c, d)
        top1 = jnp.maximum(hi1, hi2)
        top2 = jnp.maximum(jnp.minimum(hi1, hi2), jnp.maximum(lo1, lo2))
        grp.append(top1 + top2)
    best = jnp.zeros_like(grp[0], dtype=jnp.int32)
    best_v = grp[0]
    for g in range(1, N_GROUPS):
        upd = grp[g] > best_v
        best = jnp.where(upd, g, best)
        best_v = jnp.where(upd, grp[g], best_v)
    picked = []
    for e in range(N_EXPERTS):
        g = e // EXPERTS_PER_GROUP
        rank = jnp.zeros_like(best)
        for o in range(4 * g, 4 * g + 4):
            if o == e:
                continue
            ahead = rows[o] > rows[e]
            if o < e:
                ahead = ahead | (rows[o] == rows[e])
            rank = rank + ahead.astype(jnp.int32)
        keep = (rank < 2) & (best == g)
        picked.append(jnp.where(keep, srow[e], 0.0))
    total = picked[0]
    for e in range(1, N_EXPERTS):
        total = total + picked[e]
    return [p / total for p in picked]


def _mid_kernel(o_ref, g_ref, x_ref, mod_ref, woa_ref, wob_ref, wout_ref, gffn_ref,
                wrh_ref, wrl_ref, brc_ref, xn_ref, f_ref, gate_ref):
    o = o_ref[0]
    n_a = MLA_HEADS * MLA_V
    ya = _dot(o[:, 0:n_a], woa_ref[0])
    yb = _dot(o[:, n_a:], wob_ref[0])
    g = g_ref[0]
    y = jax.nn.sigmoid(g[:, 0:D_MODEL]) * ya + jax.nn.sigmoid(g[:, D_MODEL:]) * yb
    z = _dot(y.astype(BF16), wout_ref[0])
    gt1 = mod_ref[0, 0, 0:1, :]
    sh2 = mod_ref[0, 0, 1:2, :]
    sc2 = mod_ref[0, 0, 2:3, :]
    xn = x_ref[0] + gt1 * z
    xn_ref[0] = xn
    f = _rms_rows(xn, gffn_ref[0]) * (1 + sc2) + sh2
    f_hi, f_lo = _split_bf16(f)
    f_ref[0] = f_hi
    w_hi = wrh_ref[...]
    logits = _dot(f_hi, w_hi) + _dot(f_lo, w_hi) + _dot(f_hi, wrl_ref[...])
    scores_t = jax.nn.sigmoid(logits).T
    scores = scores_t[0:N_EXPERTS, :]
    sel = scores + brc_ref[0:N_EXPERTS, :]
    gates = _route_rows(sel, scores)
    t = scores.shape[1]
    gates.append(jnp.ones((1, t), F32))
    gates.append(jnp.zeros((LANES - N_EXPERTS - 1, t), F32))
    gate_ref[0] = jnp.concatenate(gates, axis=0).T


def _mid_call(o_all, gts, xs, modm, layer, n_rows, n_lat, w_oa, w_ob, w_out, g_ffn, wr_hi, wr_lo,
              br_col):
    b = xs.shape[0]
    n = n_rows
    n_lat_tiles = n_lat // ROW_TILE
    t = ROW_TILE

    def lspec(arr):
        shp = arr.shape
        return pl.BlockSpec((1,) + shp[1:], lambda i, j: (layer,) + (0,) * (len(shp) - 1))

    def full(arr):
        return pl.BlockSpec(arr.shape, lambda i, j: (0,) * arr.ndim)

    return pl.pallas_call(
        _mid_kernel,
        grid=(b, n // t),
        in_specs=[
            pl.BlockSpec((1, t, o_all.shape[2]), lambda i, j: (i, j, 0)),
            pl.BlockSpec((1, t, 2 * D_MODEL), lambda i, j: (i, j, 0)),
            pl.BlockSpec((1, t, D_MODEL), lambda i, j: (i, j, 0)),
            pl.BlockSpec((1, 1, 3, D_MODEL), lambda i, j: (i, jnp.where(j >= n_lat_tiles, 1, 0), 0, 0)),
            lspec(w_oa), lspec(w_ob), lspec(w_out), lspec(g_ffn),
            full(wr_hi), full(wr_lo), full(br_col),
        ],
        out_specs=[
            pl.BlockSpec((1, t, D_MODEL), lambda i, j: (i, j, 0)),
            pl.BlockSpec((1, t, D_MODEL), lambda i, j: (i, j, 0)),
            pl.BlockSpec((1, t, LANES), lambda i, j: (i, j, 0)),
        ],
        out_shape=[
            jax.ShapeDtypeStruct((b, n, D_MODEL), F32),
            jax.ShapeDtypeStruct((b, n, D_MODEL), BF16),
            jax.ShapeDtypeStruct((b, n, LANES), F32),
        ],
        compiler_params=pltpu.CompilerParams(
            dimension_semantics=("arbitrary", "arbitrary"), vmem_limit_bytes=VMEM_LIMIT),
        name="mixer_out_router",
    )(o_all, gts, xs, modm, w_oa, w_ob, w_out, g_ffn, wr_hi, wr_lo, br_col)


def _moe_kernel(f_ref, gate_ref, x_ref, gt_ref, wg_ref, wu_ref, wd_ref, o_ref, acc_ref, *, n_lat):
    e = pl.program_id(2)
    last = pl.num_programs(2) - 1

    @pl.when(e == 0)
    def _():
        acc_ref[...] = jnp.zeros_like(acc_ref)

    h = f_ref[0]
    a = jax.nn.silu(_dot(h, wg_ref[0, 0])) * _dot(h, wu_ref[0, 0])
    y = _dot(a.astype(BF16), wd_ref[0, 0])
    gates = gate_ref[0]
    lane = lax.broadcasted_iota(jnp.int32, gates.shape, 1)
    gcol = jnp.sum(jnp.where(lane == e, gates, 0.0), axis=-1, keepdims=True)
    acc_ref[...] += gcol * y

    @pl.when(e == last)
    def _():
        rows = acc_ref.shape[0]
        row = lax.broadcasted_iota(jnp.int32, (rows, 1), 0) + pl.program_id(1) * rows
        gt = jnp.where(row < n_lat, gt_ref[0, 0:1, :], gt_ref[0, 1:2, :])
        o_ref[0] = x_ref[0] + gt * acc_ref[...]


def _moe_call(f, gate, xn, gt2, layer, n_lat, wg_all, wu_all, wd_all):
    b, n, _ = xn.shape
    tiles = 4
    tm = n // tiles
    n_e = wg_all.shape[1]
    kern = functools.partial(_moe_kernel, n_lat=n_lat)
    return pl.pallas_call(
        kern,
        grid=(b, tiles, n_e),
        in_specs=[
            pl.BlockSpec((1, tm, D_MODEL), lambda i, j, e: (i, j, 0)),
            pl.BlockSpec((1, tm, LANES), lambda i, j, e: (i, j, 0)),
            pl.BlockSpec((1, tm, D_MODEL), lambda i, j, e: (i, j, 0)),
            pl.BlockSpec((1, 2, D_MODEL), lambda i, j, e: (i, 0, 0)),
            pl.BlockSpec((1, 1, D_MODEL, FF_EXPERT), lambda i, j, e: (layer, e, 0, 0)),
            pl.BlockSpec((1, 1, D_MODEL, FF_EXPERT), lambda i, j, e: (layer, e, 0, 0)),
            pl.BlockSpec((1, 1, FF_EXPERT, D_MODEL), lambda i, j, e: (layer, e, 0, 0)),
        ],
        out_specs=pl.BlockSpec((1, tm, D_MODEL), lambda i, j, e: (i, j, 0)),
        out_shape=jax.ShapeDtypeStruct((b, n, D_MODEL), F32),
        scratch_shapes=[pltpu.VMEM((tm, D_MODEL), F32)],
        compiler_params=pltpu.CompilerParams(
            dimension_semantics=("arbitrary", "arbitrary", "arbitrary"),
            vmem_limit_bytes=VMEM_LIMIT),
        name="moe",
    )(f, gate, xn, gt2, wg_all, wu_all, wd_all)


def _pad_heads(w, n_heads, dim):
    lead = w.shape[:-1]
    w = w.reshape(lead + (n_heads, dim))
    w = jnp.pad(w, [(0, 0)] * len(lead) + [(0, 0), (0, LANES - dim)])
    return w.reshape(lead + (n_heads * LANES,))


def _prep_w_in(w_in):
    offs = np.concatenate([[0], np.cumsum(IN_SIZES)])
    cq, ckv, kr, qb, kb, vb, gts = [w_in[..., int(offs[i]):int(offs[i + 1])] for i in range(7)]
    kr_blk = jnp.pad(kr, [(0, 0), (0, 0), (MLA_NOPE, LANES - MLA_QK)])
    w_main = jnp.concatenate(
        [cq, ckv, kr_blk, _pad_heads(qb, GQA_HEADS, GQA_HD), _pad_heads(kb, GQA_KV_HEADS, GQA_HD),
         gts], axis=-1).astype(BF16)
    w_vbt = jnp.swapaxes(_pad_heads(vb, GQA_KV_HEADS, GQA_HD), 1, 2).astype(BF16)
    return w_main, w_vbt


def _prep_w_ukv(w_ukv):
    l, r, _ = w_ukv.shape
    w = w_ukv.reshape(l, r, MLA_HEADS, MLA_NOPE + MLA_V)
    uk = _pad_heads(w[..., :MLA_NOPE].reshape(l, r, MLA_HEADS * MLA_NOPE), MLA_HEADS, MLA_NOPE)
    uv = _pad_heads(w[..., MLA_NOPE:].reshape(l, r, MLA_HEADS * MLA_V), MLA_HEADS, MLA_V)
    return uk.astype(BF16), jnp.swapaxes(uv, 1, 2).astype(BF16)


def _pad_gain(g, dim):
    return jnp.pad(g, [(0, 0), (0, LANES - dim)])[:, None, :]


def _rope_tables(n_ctx, n_lat, lane0, half_dim):
    pos = jnp.arange(n_lat)
    quarter = half_dim // 2
    inv = ROPE_THETA ** (-jnp.arange(0, half_dim, 2, dtype=F32) / half_dim)
    ang_r = (pos // GRID_W).astype(F32)[:, None] * inv[None, :]
    ang_c = (pos % GRID_W).astype(F32)[:, None] * inv[None, :]
    cos = jnp.concatenate([jnp.cos(ang_r)] * 2 + [jnp.cos(ang_c)] * 2, axis=-1)
    zero = jnp.zeros((n_lat, quarter), F32)
    sin_a = jnp.concatenate([-jnp.sin(ang_r), zero, -jnp.sin(ang_c), zero], axis=-1)
    sin_b = jnp.concatenate([zero, jnp.sin(ang_r), zero, jnp.sin(ang_c)], axis=-1)
    pad = [(0, n_ctx), (lane0, LANES - lane0 - 2 * half_dim)]
    return (jnp.pad(cos, pad, constant_values=1.0), jnp.pad(sin_a, pad), jnp.pad(sin_b, pad))


def kernel(x, c, ctx, c_ctx, w_mod, b_mod, g_attn, g_ffn, w_in, g_cq, w_uq, g_ckv, w_ukv,
           g_qa, g_ka, g_qb, g_kb, w_oa, w_ob, w_out, w_router, b_router,
           w_e_gate, w_e_up, w_e_down, w_s_gate, w_s_up, w_s_down):
    b, n_lat, d = x.shape
    n_ctx = ctx.shape[1]
    n = n_lat + n_ctx
    depth = w_mod.shape[0]
    assert d == D_MODEL and n_ctx == ROW_TILE and n_lat % KV_CHUNK == 0 and n_lat % Q_TILE == 0

    xs = jnp.concatenate([x, ctx], axis=1)
    cs = jnp.concatenate([c, c_ctx[None, :], jnp.zeros((8 - b - 1, d), F32)], axis=0)

    w_in_p, w_vbt = _prep_w_in(w_in)
    w_uq_p = _pad_heads(w_uq, MLA_HEADS, MLA_QK).astype(BF16)
    w_uk_p, w_uvt = _prep_w_ukv(w_ukv)
    g3 = lambda g: g[:, None, :]
    g_qa_p, g_ka_p = _pad_gain(g_qa, MLA_QK), _pad_gain(g_ka, MLA_QK)
    g_qb_p, g_kb_p = _pad_gain(g_qb, GQA_HD), _pad_gain(g_kb, GQA_HD)
    tables = (_rope_tables(n_ctx, n_lat, MLA_NOPE, MLA_ROPE // 2)
              + _rope_tables(n_ctx, n_lat, 0, GQA_HD // 2))
    w_oa_b, w_ob_b, w_out_b = w_oa.astype(BF16), w_ob.astype(BF16), w_out.astype(BF16)
    wr_p = jnp.pad(w_router, [(0, 0), (0, LANES - N_EXPERTS)])
    wr_hi = wr_p.astype(BF16)
    wr_lo = (wr_p - wr_hi.astype(F32)).astype(BF16)
    br_col = jnp.pad(b_router, (0, LANES - N_EXPERTS))[:, None]
    wg_all = jnp.concatenate([w_e_gate, w_s_gate[:, None]], axis=1).astype(BF16)
    wu_all = jnp.concatenate([w_e_up, w_s_up[:, None]], axis=1).astype(BF16)
    wd_all = jnp.concatenate([w_e_down, w_s_down[:, None]], axis=1).astype(BF16)
    b_mod3 = b_mod[:, None, :]

    for layer in range(depth):
        last = layer == depth - 1
        n_rows = n_lat if last else n
        mods = _mod_call(cs, w_mod, b_mod3, layer).reshape(8, 6, d)
        lat, cx = mods[:b], jnp.broadcast_to(mods[b:b + 1], (b, 6, d))
        both = jnp.stack([lat, cx], axis=1)
        mod1, modm, gt2 = both[:, :, 0:2], both[:, :, 2:5], both[:, :, 5]
        q_all, k_all, v_all, gts = _in_call(
            xs, mod1, layer, n_lat, g3(g_attn), w_in_p, g3(g_cq), w_uq_p, g3(g_ckv), w_uk_p, w_uvt,
            w_vbt, g_qa_p, g_ka_p, g_qb_p, g_kb_p, tables)
        o_all = _attn_lat_call(q_all, k_all, v_all, n_lat)
        if not last:
            o_all = _attn_ctx_call(q_all, k_all, v_all, o_all, n_lat)
        xn, f, gate = _mid_call(o_all, gts, xs, modm, layer, n_rows, n_lat, w_oa_b, w_ob_b, w_out_b,
                                g3(g_ffn), wr_hi, wr_lo, br_col)
        xs = _moe_call(f, gate, xn, gt2, layer, n_lat, wg_all, wu_all, wd_all)
    return xs
```

```python
import functools

import numpy as np
import jax
import jax.numpy as jnp
from jax import lax
from jax.experimental import pallas as pl
from jax.experimental.pallas import tpu as pltpu

D_MODEL = 1024
GRID_W = 64
MLA_HEADS = 8
MLA_NOPE = 64
MLA_ROPE = 32
MLA_QK = MLA_NOPE + MLA_ROPE
MLA_V = 64
Q_LORA = 256
KV_LORA = 128
GQA_HEADS = 8
GQA_KV_HEADS = 2
GQA_GROUP = GQA_HEADS // GQA_KV_HEADS
GQA_HD = 64
N_EXPERTS = 16
N_GROUPS = 4
EXPERTS_PER_GROUP = N_EXPERTS // N_GROUPS
FF_EXPERT = 512
ROPE_THETA = 10000.0
EPS = 1e-6
IN_SIZES = (Q_LORA, KV_LORA, MLA_ROPE, GQA_HEADS * GQA_HD, GQA_KV_HEADS * GQA_HD,
            GQA_KV_HEADS * GQA_HD, 2 * D_MODEL)

LANES = 128
ROW_TILE = 256
Q_TILE = 1024
KV_CHUNK = 512
N_HEADS_ALL = MLA_HEADS + GQA_HEADS
N_KHEADS_ALL = MLA_HEADS + GQA_KV_HEADS
LOG2_E = 1.4426950408889634
ONES_ROW = MLA_V
VMEM_LIMIT = 56 * 1024 * 1024
MOE_TILE = 256
N_CLASSES = N_GROUPS << EXPERTS_PER_GROUP
N_PAIR_CLASSES = N_GROUPS * 6
F_EXT = D_MODEL + LANES
META_CLS, META_RANK, META_W_LO, META_W_HI = 0, 1, 2, 3

C_CQ = 0
C_CKV = C_CQ + Q_LORA
C_KR = C_CKV + KV_LORA
C_QB = C_KR + LANES
C_KB = C_QB + GQA_HEADS * LANES
C_G = C_KB + GQA_KV_HEADS * LANES
C_END = C_G + 2 * D_MODEL

BF16 = jnp.bfloat16
F32 = jnp.float32


def _dot(a, b):
    return jnp.dot(a, b, preferred_element_type=F32)


def _dot_nt(a, b):
    return lax.dot_general(a, b, (((1,), (1,)), ((), ())), preferred_element_type=F32)


def _split_bf16(x):
    hi = x.astype(BF16)
    lo = (x - hi.astype(F32)).astype(BF16)
    return hi, lo


def _rms_rows(x, g):
    ms = jnp.mean(x * x, axis=-1, keepdims=True)
    return x * lax.rsqrt(ms + EPS) * g


def _head_norm(blk, g, dim):
    ms = jnp.sum(blk * blk, axis=-1, keepdims=True) * (1.0 / dim)
    return blk * lax.rsqrt(ms + EPS) * g


def _rope(x, cos, sin_a, sin_b, shift):
    up = pltpu.roll(x, LANES - shift, 1)
    down = pltpu.roll(x, shift, 1)
    return x * cos + up * sin_a + down * sin_b


def _mod_kernel(c_ref, w_ref, b_ref, o_ref):
    c = c_ref[...]
    s_hi, s_lo = _split_bf16(c * jax.nn.sigmoid(c))
    w_hi, w_lo = _split_bf16(w_ref[0])
    o_ref[...] = _dot(s_hi, w_hi) + _dot(s_lo, w_hi) + _dot(s_hi, w_lo) + b_ref[0]


def _mod_call(cs, w_mod, b_mod, layer):
    n_cols = w_mod.shape[2]
    tn = 1536
    return pl.pallas_call(
        _mod_kernel,
        grid=(n_cols // tn,),
        in_specs=[
            pl.BlockSpec(cs.shape, lambda j: (0, 0)),
            pl.BlockSpec((1, D_MODEL, tn), lambda j: (layer, 0, j)),
            pl.BlockSpec((1, 1, tn), lambda j: (layer, 0, j)),
        ],
        out_specs=pl.BlockSpec((cs.shape[0], tn), lambda j: (0, j)),
        out_shape=jax.ShapeDtypeStruct((cs.shape[0], n_cols), F32),
        compiler_params=pltpu.CompilerParams(
            dimension_semantics=("arbitrary",), vmem_limit_bytes=VMEM_LIMIT),
        name="mod",
    )(cs, w_mod, b_mod)


def _in_kernel(x_ref, mod_ref, gattn_ref, win_ref, gcq_ref, wuq_ref, gckv_ref, wuk_ref, wuvt_ref,
               wvbt_ref, gqa_ref, gka_ref, gqb_ref, gkb_ref,
               cosa_ref, sinaa_ref, sinba_ref, cosb_ref, sinab_ref, sinbb_ref,
               q_ref, k_ref, v_ref, g_ref):
    x = x_ref[0]
    shift = mod_ref[0, 0, 0:1, :]
    scale = mod_ref[0, 0, 1:2, :]
    h = (_rms_rows(x, gattn_ref[0]) * (1 + scale) + shift).astype(BF16)

    g_ref[0] = _dot(h, win_ref[0, :, C_G:C_END])

    cos_a, sin_aa, sin_ba = cosa_ref[...], sinaa_ref[...], sinba_ref[...]
    cos_b, sin_ab, sin_bb = cosb_ref[...], sinab_ref[...], sinbb_ref[...]
    scale_a = MLA_QK ** -0.5 * LOG2_E
    scale_b = GQA_HD ** -0.5 * LOG2_E
    half_a = MLA_ROPE // 4
    half_b = GQA_HD // 4
    t = x.shape[0]
    ones_row = lax.broadcasted_iota(jnp.int32, (LANES, t), 0) == ONES_ROW

    p0 = _dot(h, win_ref[0, :, C_CQ:C_QB])
    cq = p0[:, C_CQ:C_CKV]
    ckv = p0[:, C_CKV:C_KR]
    kr_blk = p0[:, C_KR:C_QB]
    qa = _dot(_rms_rows(cq, gcq_ref[0]).astype(BF16), wuq_ref[0])
    ckv_n = _rms_rows(ckv, gckv_ref[0]).astype(BF16)
    ka = _dot(ckv_n, wuk_ref[0])
    va_t = _dot_nt(wuvt_ref[0], ckv_n)
    g_qa, g_ka = gqa_ref[0], gka_ref[0]
    for hd in range(MLA_HEADS):
        qh = _head_norm(qa[:, hd * LANES:(hd + 1) * LANES], g_qa, MLA_QK)
        q_ref[0, hd] = (_rope(qh, cos_a, sin_aa, sin_ba, half_a) * scale_a).astype(BF16)
        kh = _head_norm(ka[:, hd * LANES:(hd + 1) * LANES] + kr_blk, g_ka, MLA_QK)
        k_ref[0, hd] = _rope(kh, cos_a, sin_aa, sin_ba, half_a).astype(BF16)
        v_ref[0, hd, 0] = jnp.where(ones_row, 1.0, va_t[hd * LANES:(hd + 1) * LANES, :]).astype(BF16)

    pq = _dot(h, win_ref[0, :, C_QB:C_KB])
    g_qb, g_kb = gqb_ref[0], gkb_ref[0]
    for hd in range(GQA_HEADS):
        qh = _head_norm(pq[:, hd * LANES:(hd + 1) * LANES], g_qb, GQA_HD)
        q_ref[0, MLA_HEADS + hd] = (
            _rope(qh, cos_b, sin_ab, sin_bb, half_b) * scale_b).astype(BF16)
    pk = _dot(h, win_ref[0, :, C_KB:C_G])
    vb_t = _dot_nt(wvbt_ref[0], h)
    for hd in range(GQA_KV_HEADS):
        kh = _head_norm(pk[:, hd * LANES:(hd + 1) * LANES], g_kb, GQA_HD)
        k_ref[0, MLA_HEADS + hd] = _rope(kh, cos_b, sin_ab, sin_bb, half_b).astype(BF16)
        v_ref[0, MLA_HEADS + hd, 0] = jnp.where(
            ones_row, 1.0, vb_t[hd * LANES:(hd + 1) * LANES, :]).astype(BF16)


def _in_call(xs, mod1, layer, n_lat, g_attn, w_in_p, g_cq, w_uq_p, g_ckv, w_uk_p, w_uvt, w_vbt,
             g_qa, g_ka, g_qb, g_kb, tables):
    b, n, _ = xs.shape
    nt = n // ROW_TILE
    n_lat_tiles = n_lat // ROW_TILE
    t = ROW_TILE

    def lspec(arr):
        shp = arr.shape
        return pl.BlockSpec((1,) + shp[1:], lambda i, j: (layer,) + (0,) * (len(shp) - 1))

    tab_spec = pl.BlockSpec((t, LANES), lambda i, j: (j, 0))
    return pl.pallas_call(
        _in_kernel,
        grid=(b, nt),
        in_specs=[
            pl.BlockSpec((1, t, D_MODEL), lambda i, j: (i, j, 0)),
            pl.BlockSpec((1, 1, 2, D_MODEL), lambda i, j: (i, jnp.where(j >= n_lat_tiles, 1, 0), 0, 0)),
            lspec(g_attn), lspec(w_in_p), lspec(g_cq), lspec(w_uq_p), lspec(g_ckv),
            lspec(w_uk_p), lspec(w_uvt), lspec(w_vbt), lspec(g_qa), lspec(g_ka), lspec(g_qb),
            lspec(g_kb),
        ] + [tab_spec] * 6,
        out_specs=[
            pl.BlockSpec((1, N_HEADS_ALL, t, LANES), lambda i, j: (i, 0, j, 0)),
            pl.BlockSpec((1, N_KHEADS_ALL, t, LANES), lambda i, j: (i, 0, j, 0)),
            pl.BlockSpec((1, N_KHEADS_ALL, 1, LANES, t), lambda i, j: (i, 0, j, 0, 0)),
            pl.BlockSpec((1, t, 2 * D_MODEL), lambda i, j: (i, j, 0)),
        ],
        out_shape=[
            jax.ShapeDtypeStruct((b, N_HEADS_ALL, n, LANES), BF16),
            jax.ShapeDtypeStruct((b, N_KHEADS_ALL, n, LANES), BF16),
            jax.ShapeDtypeStruct((b, N_KHEADS_ALL, nt, LANES, t), BF16),
            jax.ShapeDtypeStruct((b, n, 2 * D_MODEL), F32),
        ],
        compiler_params=pltpu.CompilerParams(
            dimension_semantics=("arbitrary", "arbitrary"), vmem_limit_bytes=VMEM_LIMIT),
        name="mixer_in",
    )(xs, mod1, g_attn, w_in_p, g_cq, w_uq_p, g_ckv, w_uk_p, w_uvt, w_vbt,
      g_qa, g_ka, g_qb, g_kb, *tables)


def _attn_kernel(q_ref, k0_ref, k1_ref, v0_ref, v1_ref, *rest, n_main, tail_chunk, pipelined):
    if pipelined:
        o_ref, s_scr, acc_scr = rest[-3:]
    else:
        o_ref, acc_scr = rest[-2:]
    qs = (q_ref[0, 0], q_ref[0, 1])
    k_refs = (k0_ref, k1_ref)
    v_refs = (v0_ref, v1_ref)
    tq = qs[0].shape[0]
    per = KV_CHUNK // ROW_TILE

    def fold(hh, m, cmax, st, chunk0, n_sub, first=False):
        m_new = cmax if first else jnp.maximum(m, cmax)
        pt = jnp.exp2(st - m_new).astype(BF16)
        pv = _dot(v_refs[hh][0, 0, chunk0], pt[0:ROW_TILE])
        for u in range(1, n_sub):
            pv = pv + _dot(v_refs[hh][0, 0, chunk0 + u], pt[u * ROW_TILE:(u + 1) * ROW_TILE])
        if first:
            acc_scr[hh] = pv
        else:
            acc_scr[hh] = jnp.exp2(m - m_new) * acc_scr[hh] + pv
        return m_new

    def produce(c, slot):
        cmax = []
        for hh in range(2):
            off = c * KV_CHUNK
            if not isinstance(off, int):
                off = pl.multiple_of(off, KV_CHUNK)
            st = _dot_nt(k_refs[hh][0, 0, pl.ds(off, KV_CHUNK), :], qs[hh])
            s_scr[slot, hh] = st
            cmax.append(jnp.max(st, axis=0, keepdims=True))
        return tuple(cmax)

    def consume(c, slot, ms, cmax):
        return tuple(fold(hh, ms[hh], cmax[hh], s_scr[slot, hh], c * per, per) for hh in range(2))

    ms = []
    for hh in range(2):
        st = _dot_nt(k_refs[hh][0, 0, pl.ds(tail_chunk * ROW_TILE, ROW_TILE), :], qs[hh])
        ms.append(fold(hh, None, jnp.max(st, axis=0, keepdims=True), st, tail_chunk, 1, first=True))
    ms = tuple(ms)

    if pipelined:
        def body(i, carry):
            ms, cmax = carry
            cmax1 = produce(2 * i + 1, 1)
            ms = consume(2 * i, 0, ms, cmax)
            cmax2 = produce(2 * i + 2, 0)
            ms = consume(2 * i + 1, 1, ms, cmax1)
            return ms, cmax2

        ms, cmax = lax.fori_loop(0, (n_main - 2) // 2, body, (ms, produce(0, 0)))
        cmax1 = produce(n_main - 1, 1)
        ms = consume(n_main - 2, 0, ms, cmax)
        ms = consume(n_main - 1, 1, ms, cmax1)

    outs = []
    for hh in range(2):
        acc = acc_scr[hh]
        outs.append((acc / acc[ONES_ROW:ONES_ROW + 1, :]).T)
    low = lax.broadcasted_iota(jnp.int32, (tq, LANES), 1) < MLA_V
    o_ref[0] = jnp.where(low, outs[0], pltpu.roll(outs[1], MLA_V, 1)).astype(o_ref.dtype)


def _kv_head(p):
    mla_pairs = MLA_HEADS // 2
    pairs_per_kv = GQA_GROUP // 2
    gqa = MLA_HEADS + (p - mla_pairs) // pairs_per_kv
    return jnp.where(p < mla_pairs, 2 * p, gqa), jnp.where(p < mla_pairs, 2 * p + 1, gqa)


def _attn_lat_call(q_all, k_all, v_all, n_lat):
    b, _, n, _ = q_all.shape
    n_pairs = N_HEADS_ALL // 2
    n_chunks = n // ROW_TILE
    n_main = n_lat // KV_CHUNK
    assert n_main >= 2 and n_main % 2 == 0
    kern = functools.partial(_attn_kernel, n_main=n_main, tail_chunk=n_lat // ROW_TILE,
                             pipelined=True)
    k_spec = lambda which: pl.BlockSpec(
        (1, 1, n, LANES), lambda i, p, j: (i, _kv_head(p)[which], 0, 0))
    v_spec = lambda which: pl.BlockSpec(
        (1, 1, n_chunks, LANES, ROW_TILE), lambda i, p, j: (i, _kv_head(p)[which], 0, 0, 0))
    return pl.pallas_call(
        kern,
        grid=(b, n_pairs, n_lat // Q_TILE),
        in_specs=[pl.BlockSpec((1, 2, Q_TILE, LANES), lambda i, p, j: (i, p, j, 0)),
                  k_spec(0), k_spec(1), v_spec(0), v_spec(1)],
        out_specs=pl.BlockSpec((1, Q_TILE, LANES), lambda i, p, j: (i, j, p)),
        out_shape=jax.ShapeDtypeStruct((b, n_lat, n_pairs * LANES), BF16),
        scratch_shapes=[pltpu.VMEM((2, 2, KV_CHUNK, Q_TILE), F32),
                        pltpu.VMEM((2, LANES, Q_TILE), F32)],
        compiler_params=pltpu.CompilerParams(
            dimension_semantics=("arbitrary", "arbitrary", "arbitrary"),
            vmem_limit_bytes=VMEM_LIMIT),
        name="attention",
    )(q_all, k_all, k_all, v_all, v_all)


def _attn_ctx_call(q_all, k_all, v_all, n_lat):
    b, _, n, _ = q_all.shape
    n_pairs = N_HEADS_ALL // 2
    c = n_lat // ROW_TILE
    assert n - n_lat == ROW_TILE
    kern = functools.partial(_attn_kernel, n_main=0, tail_chunk=0, pipelined=False)
    k_spec = lambda which: pl.BlockSpec(
        (1, 1, ROW_TILE, LANES), lambda i, p: (i, _kv_head(p)[which], c, 0))
    v_spec = lambda which: pl.BlockSpec(
        (1, 1, 1, LANES, ROW_TILE), lambda i, p: (i, _kv_head(p)[which], c, 0, 0))
    return pl.pallas_call(
        kern,
        grid=(b, n_pairs),
        in_specs=[pl.BlockSpec((1, 2, ROW_TILE, LANES), lambda i, p: (i, p, c, 0)),
                  k_spec(0), k_spec(1), v_spec(0), v_spec(1)],
        out_specs=pl.BlockSpec((1, ROW_TILE, LANES), lambda i, p: (i, 0, p)),
        out_shape=jax.ShapeDtypeStruct((b, ROW_TILE, n_pairs * LANES), BF16),
        scratch_shapes=[pltpu.VMEM((2, LANES, ROW_TILE), F32)],
        compiler_params=pltpu.CompilerParams(
            dimension_semantics=("arbitrary", "arbitrary"), vmem_limit_bytes=VMEM_LIMIT),
        name="attention_ctx",
    )(q_all, k_all, k_all, v_all, v_all)


def _route_rows(sel, scores):
    rows = [sel[e:e + 1, :] for e in range(N_EXPERTS)]
    srow = [scores[e:e + 1, :] for e in range(N_EXPERTS)]
    grp = []
    for g in range(N_GROUPS):
        a, b, c, d = rows[4 * g:4 * g + 4]
        hi1, lo1 = jnp.maximum(a, b), jnp.minimum(a, b)
        hi2, lo2 = jnp.maximum(c, d), jnp.minimum(c, d)
        top1 = jnp.maximum(hi1, hi2)
        top2 = jnp.maximum(jnp.minimum(hi1, hi2), jnp.maximum(lo1, lo2))
        grp.append(top1 + top2)
    best = jnp.zeros_like(grp[0], dtype=jnp.int32)
    best_v = grp[0]
    for g in range(1, N_GROUPS):
        upd = grp[g] > best_v
        best = jnp.where(upd, g, best)
        best_v = jnp.where(upd, grp[g], best_v)
    picked = []
    keeps = []
    for e in range(N_EXPERTS):
        g = e // EXPERTS_PER_GROUP
        rank = jnp.zeros_like(best)
        for o in range(4 * g, 4 * g + 4):
            if o == e:
                continue
            ahead = rows[o] > rows[e]
            if o < e:
                ahead = ahead | (rows[o] == rows[e])
            rank = rank + ahead.astype(jnp.int32)
        keep = (rank < 2) & (best == g)
        picked.append(jnp.where(keep, srow[e], 0.0))
        keeps.append(keep)
    total = picked[0]
    for e in range(1, N_EXPERTS):
        total = total + picked[e]
    return [p / total for p in picked], best, keeps


def _route_meta(gates, best, keeps):
    mask = jnp.zeros_like(best)
    w_lo = jnp.zeros_like(gates[0])
    w_hi = jnp.zeros_like(gates[0])
    for e in range(N_EXPERTS):
        g, i = divmod(e, EXPERTS_PER_GROUP)
        mask = mask + jnp.where(keeps[e], 1 << i, 0)
        if i == 0:
            w_lo = w_lo + jnp.where(keeps[e], gates[e], 0.0)
            continue
        below = keeps[4 * g]
        for o in range(4 * g + 1, e):
            below = below | keeps[o]
        w_lo = w_lo + jnp.where(keeps[e] & ~below, gates[e], 0.0)
        w_hi = w_hi + jnp.where(keeps[e] & below, gates[e], 0.0)
    cls = (best * (1 << EXPERTS_PER_GROUP) + mask).astype(F32)
    return cls, w_lo, w_hi


def _mid_kernel(o_ref, octx_ref, g_ref, x_ref, mod_ref, woa_ref, wob_ref, wout_ref, gffn_ref,
                wrh_ref, wrl_ref, brc_ref, xn_ref, fext_ref, cnt_ref, run_ref, *, n_lat_tiles):
    o = jnp.where(pl.program_id(1) >= n_lat_tiles, octx_ref[0], o_ref[0])
    n_a = MLA_HEADS * MLA_V
    ya = _dot(o[:, 0:n_a], woa_ref[0])
    yb = _dot(o[:, n_a:], wob_ref[0])
    g = g_ref[0]
    y = jax.nn.sigmoid(g[:, 0:D_MODEL]) * ya + jax.nn.sigmoid(g[:, D_MODEL:]) * yb
    z = _dot(y.astype(BF16), wout_ref[0])
    gt1 = mod_ref[0, 0, 0:1, :]
    sh2 = mod_ref[0, 0, 1:2, :]
    sc2 = mod_ref[0, 0, 2:3, :]
    xn = x_ref[0] + gt1 * z
    xn_ref[0] = xn
    f = _rms_rows(xn, gffn_ref[0]) * (1 + sc2) + sh2
    fext_ref[0, :, 0:D_MODEL] = f
    f_hi, f_lo = _split_bf16(f)
    w_hi = wrh_ref[...]
    logits = _dot(f_hi, w_hi) + _dot(f_lo, w_hi) + _dot(f_hi, wrl_ref[...])
    scores_t = jax.nn.sigmoid(logits).T
    scores = scores_t[0:N_EXPERTS, :]
    sel = scores + brc_ref[0:N_EXPERTS, :]
    cls, w_lo, w_hi_gate = _route_meta(*_route_rows(sel, scores))
    t = scores.shape[1]
    row_id = lax.broadcasted_iota(jnp.int32, (LANES, t), 0)
    meta_t = jnp.where(row_id == META_CLS, cls, 0.0)
    meta_t = jnp.where(row_id == META_W_LO, w_lo, meta_t)
    meta_t = jnp.where(row_id == META_W_HI, w_hi_gate, meta_t)
    meta = meta_t.T

    @pl.when((pl.program_id(0) == 0) & (pl.program_id(1) == 0))
    def _():
        run_ref[...] = jnp.zeros_like(run_ref)

    lane = lax.broadcasted_iota(jnp.int32, (t, LANES), 1)
    onehot = jnp.where(lane.astype(F32) == meta[:, META_CLS:META_CLS + 1], 1.0, 0.0)
    earlier = jnp.where(lax.broadcasted_iota(jnp.int32, (t, t), 0)
                        > lax.broadcasted_iota(jnp.int32, (t, t), 1), 1.0, 0.0).astype(BF16)
    before = _dot(earlier, onehot.astype(BF16)) + run_ref[...]
    rank = jnp.sum(before * onehot, axis=-1, keepdims=True)
    fext_ref[0, :, D_MODEL:] = jnp.where(lane == META_RANK, rank, meta)
    run_ref[...] += jnp.sum(onehot, axis=0, keepdims=True)
    cnt_ref[...] = jnp.broadcast_to(run_ref[...], cnt_ref.shape)


def _mid_call(o_lat, o_ctx, gts, xs, modm, layer, n_rows, n_lat, w_oa, w_ob, w_out, g_ffn, wr_hi,
              wr_lo, br_col):
    b = xs.shape[0]
    n = n_rows
    n_lat_tiles = n_lat // ROW_TILE
    t = ROW_TILE

    def lspec(arr):
        shp = arr.shape
        return pl.BlockSpec((1,) + shp[1:], lambda i, j: (layer,) + (0,) * (len(shp) - 1))

    def full(arr):
        return pl.BlockSpec(arr.shape, lambda i, j: (0,) * arr.ndim)

    return pl.pallas_call(
        functools.partial(_mid_kernel, n_lat_tiles=n_lat_tiles),
        grid=(b, n // t),
        in_specs=[
            pl.BlockSpec((1, t, o_lat.shape[2]), lambda i, j: (i, jnp.minimum(j, n_lat_tiles - 1), 0)),
            pl.BlockSpec((1, t, o_ctx.shape[2]), lambda i, j: (i, 0, 0)),
            pl.BlockSpec((1, t, 2 * D_MODEL), lambda i, j: (i, j, 0)),
            pl.BlockSpec((1, t, D_MODEL), lambda i, j: (i, j, 0)),
            pl.BlockSpec((1, 1, 3, D_MODEL), lambda i, j: (i, jnp.where(j >= n_lat_tiles, 1, 0), 0, 0)),
            lspec(w_oa), lspec(w_ob), lspec(w_out), lspec(g_ffn),
            full(wr_hi), full(wr_lo), full(br_col),
        ],
        out_specs=[
            pl.BlockSpec((1, t, D_MODEL), lambda i, j: (i, j, 0)),
            pl.BlockSpec((1, t, F_EXT), lambda i, j: (i, j, 0)),
            pl.BlockSpec((8, LANES), lambda i, j: (0, 0)),
        ],
        out_shape=[
            jax.ShapeDtypeStruct((b, n, D_MODEL), F32),
            jax.ShapeDtypeStruct((b, n, F_EXT), F32),
            jax.ShapeDtypeStruct((8, LANES), F32),
        ],
        scratch_shapes=[pltpu.VMEM((1, LANES), F32)],
        compiler_params=pltpu.CompilerParams(
            dimension_semantics=("arbitrary", "arbitrary"), vmem_limit_bytes=VMEM_LIMIT),
        name="mixer_out_router",
    )(o_lat, o_ctx, gts, xs, modm, w_oa, w_ob, w_out, g_ffn, wr_hi, wr_lo, br_col)


def _class_expert_tables():
    lo = np.zeros((N_CLASSES,), np.int32)
    hi = np.zeros((N_CLASSES,), np.int32)
    for c in range(N_CLASSES):
        g, mask = divmod(c, 1 << EXPERTS_PER_GROUP)
        bits = [i for i in range(EXPERTS_PER_GROUP) if mask >> i & 1]
        if len(bits) == 2:
            lo[c], hi[c] = EXPERTS_PER_GROUP * g + bits[0], EXPERTS_PER_GROUP * g + bits[1]
    return lo, hi


def _route_plan(meta, counts, n_tiles_max):
    cls = meta[:, META_CLS].astype(jnp.int32)
    rank = meta[:, META_RANK].astype(jnp.int32)
    cnt = counts[0, :N_CLASSES].astype(jnp.int32)
    tiles = (cnt + MOE_TILE - 1) // MOE_TILE
    tile_end = jnp.cumsum(tiles)
    pos = (tile_end - tiles)[cls] * MOE_TILE + rank
    n_used = tile_end[-1]
    tile_idx = jnp.minimum(jnp.arange(n_tiles_max, dtype=jnp.int32), n_used - 1)
    tile_cls = jnp.searchsorted(tile_end, tile_idx, side="right").astype(jnp.int32)
    lo, hi = _class_expert_tables()
    return (pos.astype(jnp.int32), tile_idx, jnp.asarray(lo)[tile_cls], jnp.asarray(hi)[tile_cls],
            n_used.reshape(1).astype(jnp.int32))


def _row_copy(src_ref, src_row, dst_ref, dst_row, sem):
    return pltpu.make_async_copy(src_ref.at[pl.ds(src_row, 1)], dst_ref.at[pl.ds(dst_row, 1)], sem)


def _scatter_kernel(pos_ref, f_hbm, fs_in_hbm, fs_hbm, sem):
    del fs_in_hbm
    i = pl.program_id(0)
    base = i * MOE_TILE
    for r in range(MOE_TILE):
        _row_copy(f_hbm, base + r, fs_hbm, pos_ref[base + r], sem).start()

    def drain():
        for _ in range(MOE_TILE):
            _row_copy(f_hbm, 0, fs_hbm, 0, sem).wait()

    pl.when(i > 0)(drain)
    pl.when(i == pl.num_programs(0) - 1)(drain)


def _scatter_call(pos, fext, n_sorted):
    n_tok = fext.shape[0]
    zeros = jnp.zeros((n_sorted, F_EXT), F32)
    return pl.pallas_call(
        _scatter_kernel,
        grid_spec=pltpu.PrefetchScalarGridSpec(
            num_scalar_prefetch=1,
            grid=(n_tok // MOE_TILE,),
            in_specs=[pl.BlockSpec(memory_space=pl.ANY), pl.BlockSpec(memory_space=pl.ANY)],
            out_specs=pl.BlockSpec(memory_space=pl.ANY),
            scratch_shapes=[pltpu.SemaphoreType.DMA(())],
        ),
        out_shape=jax.ShapeDtypeStruct((n_sorted, F_EXT), F32),
        input_output_aliases={2: 0},
        compiler_params=pltpu.CompilerParams(dimension_semantics=("arbitrary",)),
        name="moe_scatter",
    )(pos, fext, zeros)


def _moe_kernel(tidx_ref, elo_ref, ehi_ref, nused_ref, fs_ref, wgs_ref, wus_ref, wds_ref,
                wg1_ref, wu1_ref, wd1_ref, wg2_ref, wu2_ref, wd2_ref, ys_ref):
    del tidx_ref, elo_ref, ehi_ref
    used = pl.program_id(0) < nused_ref[0]

    @pl.when(jnp.logical_not(used))
    def _():
        ys_ref[...] = jnp.zeros_like(ys_ref)

    @pl.when(used)
    def _():
        rows = fs_ref[...]
        x = rows[:, 0:D_MODEL].astype(BF16)
        w_lo = rows[:, D_MODEL + META_W_LO:D_MODEL + META_W_LO + 1]
        w_hi = rows[:, D_MODEL + META_W_HI:D_MODEL + META_W_HI + 1]

        def ffn(wg_ref, wu_ref, wd_ref):
            a = jax.nn.silu(_dot(x, wg_ref[0, 0])) * _dot(x, wu_ref[0, 0])
            return _dot(a.astype(BF16), wd_ref[0, 0])

        ys_ref[...] = (ffn(wgs_ref, wus_ref, wds_ref) + w_lo * ffn(wg1_ref, wu1_ref, wd1_ref)
                       + w_hi * ffn(wg2_ref, wu2_ref, wd2_ref))


def _moe_call(plan, fs, layer, wg_all, wu_all, wd_all):
    _, tile_idx, e_lo, e_hi, n_used = plan
    n_tiles = tile_idx.shape[0]

    def up_spec(which):
        return pl.BlockSpec((1, 1, D_MODEL, FF_EXPERT), lambda j, t, lo, hi, nu: (layer, which(j, lo, hi), 0, 0))

    def down_spec(which):
        return pl.BlockSpec((1, 1, FF_EXPERT, D_MODEL), lambda j, t, lo, hi, nu: (layer, which(j, lo, hi), 0, 0))

    shared = lambda j, lo, hi: N_EXPERTS
    first = lambda j, lo, hi: lo[j]
    second = lambda j, lo, hi: hi[j]
    return pl.pallas_call(
        _moe_kernel,
        grid_spec=pltpu.PrefetchScalarGridSpec(
            num_scalar_prefetch=4,
            grid=(n_tiles,),
            in_specs=[pl.BlockSpec((MOE_TILE, F_EXT), lambda j, t, lo, hi, nu: (t[j], 0)),
                      up_spec(shared), up_spec(shared), down_spec(shared),
                      up_spec(first), up_spec(first), down_spec(first),
                      up_spec(second), up_spec(second), down_spec(second)],
            out_specs=pl.BlockSpec((MOE_TILE, D_MODEL), lambda j, t, lo, hi, nu: (j, 0)),
        ),
        out_shape=jax.ShapeDtypeStruct((n_tiles * MOE_TILE, D_MODEL), F32),
        compiler_params=pltpu.CompilerParams(
            dimension_semantics=("arbitrary",), vmem_limit_bytes=VMEM_LIMIT),
        name="moe",
    )(tile_idx, e_lo, e_hi, n_used, fs, wg_all, wu_all, wd_all, wg_all, wu_all, wd_all,
      wg_all, wu_all, wd_all)


def _gather_kernel(pos_ref, xn_ref, gt_ref, ys_hbm, o_ref, buf, sems):
    nt = pl.num_programs(1)
    step = pl.program_id(0) * nt + pl.program_id(1)
    total = pl.num_programs(0) * nt
    slot = step % 2

    def fetch(s, into):
        for r in range(MOE_TILE):
            _row_copy(ys_hbm, pos_ref[s * MOE_TILE + r], buf.at[into], r, sems.at[into]).start()

    pl.when(step == 0)(lambda: fetch(0, 0))
    pl.when(step + 1 < total)(lambda: fetch(step + 1, 1 - slot))
    for _ in range(MOE_TILE):
        _row_copy(ys_hbm, 0, buf.at[slot], 0, sems.at[slot]).wait()
    o_ref[0] = xn_ref[0] + gt_ref[0, 0] * buf[slot]


def _gather_call(pos, xn, gt2, ys, n_lat):
    b, n, _ = xn.shape
    n_lat_tiles = n_lat // MOE_TILE
    return pl.pallas_call(
        _gather_kernel,
        grid_spec=pltpu.PrefetchScalarGridSpec(
            num_scalar_prefetch=1,
            grid=(b, n // MOE_TILE),
            in_specs=[
                pl.BlockSpec((1, MOE_TILE, D_MODEL), lambda i, j, p: (i, j, 0)),
                pl.BlockSpec((1, 1, 1, D_MODEL),
                             lambda i, j, p: (i, jnp.where(j >= n_lat_tiles, 1, 0), 0, 0)),
                pl.BlockSpec(memory_space=pl.ANY),
            ],
            out_specs=pl.BlockSpec((1, MOE_TILE, D_MODEL), lambda i, j, p: (i, j, 0)),
            scratch_shapes=[pltpu.VMEM((2, MOE_TILE, D_MODEL), F32),
                            pltpu.SemaphoreType.DMA((2,))],
        ),
        out_shape=jax.ShapeDtypeStruct((b, n, D_MODEL), F32),
        compiler_params=pltpu.CompilerParams(
            dimension_semantics=("arbitrary", "arbitrary"), vmem_limit_bytes=VMEM_LIMIT),
        name="moe_gather_residual",
    )(pos, xn, gt2[:, :, None, :], ys)


def _pad_heads(w, n_heads, dim):
    lead = w.shape[:-1]
    w = w.reshape(lead + (n_heads, dim))
    w = jnp.pad(w, [(0, 0)] * len(lead) + [(0, 0), (0, LANES - dim)])
    return w.reshape(lead + (n_heads * LANES,))


def _prep_w_in(w_in):
    offs = np.concatenate([[0], np.cumsum(IN_SIZES)])
    cq, ckv, kr, qb, kb, vb, gts = [w_in[..., int(offs[i]):int(offs[i + 1])] for i in range(7)]
    kr_blk = jnp.pad(kr, [(0, 0), (0, 0), (MLA_NOPE, LANES - MLA_QK)])
    w_main = jnp.concatenate(
        [cq, ckv, kr_blk, _pad_heads(qb, GQA_HEADS, GQA_HD), _pad_heads(kb, GQA_KV_HEADS, GQA_HD),
         gts], axis=-1).astype(BF16)
    w_vbt = jnp.swapaxes(_pad_heads(vb, GQA_KV_HEADS, GQA_HD), 1, 2).astype(BF16)
    return w_main, w_vbt


def _prep_w_ukv(w_ukv):
    l, r, _ = w_ukv.shape
    w = w_ukv.reshape(l, r, MLA_HEADS, MLA_NOPE + MLA_V)
    uk = _pad_heads(w[..., :MLA_NOPE].reshape(l, r, MLA_HEADS * MLA_NOPE), MLA_HEADS, MLA_NOPE)
    uv = _pad_heads(w[..., MLA_NOPE:].reshape(l, r, MLA_HEADS * MLA_V), MLA_HEADS, MLA_V)
    return uk.astype(BF16), jnp.swapaxes(uv, 1, 2).astype(BF16)


def _pad_gain(g, dim):
    return jnp.pad(g, [(0, 0), (0, LANES - dim)])[:, None, :]


def _rope_tables(n_ctx, n_lat, lane0, half_dim):
    pos = jnp.arange(n_lat)
    quarter = half_dim // 2
    inv = ROPE_THETA ** (-jnp.arange(0, half_dim, 2, dtype=F32) / half_dim)
    ang_r = (pos // GRID_W).astype(F32)[:, None] * inv[None, :]
    ang_c = (pos % GRID_W).astype(F32)[:, None] * inv[None, :]
    cos = jnp.concatenate([jnp.cos(ang_r)] * 2 + [jnp.cos(ang_c)] * 2, axis=-1)
    zero = jnp.zeros((n_lat, quarter), F32)
    sin_a = jnp.concatenate([-jnp.sin(ang_r), zero, -jnp.sin(ang_c), zero], axis=-1)
    sin_b = jnp.concatenate([zero, jnp.sin(ang_r), zero, jnp.sin(ang_c)], axis=-1)
    pad = [(0, n_ctx), (lane0, LANES - lane0 - 2 * half_dim)]
    return (jnp.pad(cos, pad, constant_values=1.0), jnp.pad(sin_a, pad), jnp.pad(sin_b, pad))


def kernel(x, c, ctx, c_ctx, w_mod, b_mod, g_attn, g_ffn, w_in, g_cq, w_uq, g_ckv, w_ukv,
           g_qa, g_ka, g_qb, g_kb, w_oa, w_ob, w_out, w_router, b_router,
           w_e_gate, w_e_up, w_e_down, w_s_gate, w_s_up, w_s_down):
    b, n_lat, d = x.shape
    n_ctx = ctx.shape[1]
    n = n_lat + n_ctx
    depth = w_mod.shape[0]
    assert d == D_MODEL and n_ctx == ROW_TILE and n_lat % KV_CHUNK == 0 and n_lat % Q_TILE == 0

    xs = jnp.concatenate([x, ctx], axis=1)
    cs = jnp.concatenate([c, c_ctx[None, :], jnp.zeros((8 - b - 1, d), F32)], axis=0)

    w_in_p, w_vbt = _prep_w_in(w_in)
    w_uq_p = _pad_heads(w_uq, MLA_HEADS, MLA_QK).astype(BF16)
    w_uk_p, w_uvt = _prep_w_ukv(w_ukv)
    g3 = lambda g: g[:, None, :]
    g_qa_p, g_ka_p = _pad_gain(g_qa, MLA_QK), _pad_gain(g_ka, MLA_QK)
    g_qb_p, g_kb_p = _pad_gain(g_qb, GQA_HD), _pad_gain(g_kb, GQA_HD)
    tables = (_rope_tables(n_ctx, n_lat, MLA_NOPE, MLA_ROPE // 2)
              + _rope_tables(n_ctx, n_lat, 0, GQA_HD // 2))
    w_oa_b, w_ob_b, w_out_b = w_oa.astype(BF16), w_ob.astype(BF16), w_out.astype(BF16)
    wr_p = jnp.pad(w_router, [(0, 0), (0, LANES - N_EXPERTS)])
    wr_hi = wr_p.astype(BF16)
    wr_lo = (wr_p - wr_hi.astype(F32)).astype(BF16)
    br_col = jnp.pad(b_router, (0, LANES - N_EXPERTS))[:, None]
    wg_all = jnp.concatenate([w_e_gate, w_s_gate[:, None]], axis=1).astype(BF16)
    wu_all = jnp.concatenate([w_e_up, w_s_up[:, None]], axis=1).astype(BF16)
    wd_all = jnp.concatenate([w_e_down, w_s_down[:, None]], axis=1).astype(BF16)
    b_mod3 = b_mod[:, None, :]

    for layer in range(depth):
        last = layer == depth - 1
        n_rows = n_lat if last else n
        mods = _mod_call(cs, w_mod, b_mod3, layer).reshape(8, 6, d)
        lat, cx = mods[:b], jnp.broadcast_to(mods[b:b + 1], (b, 6, d))
        both = jnp.stack([lat, cx], axis=1)
        mod1, modm, gt2 = both[:, :, 0:2], both[:, :, 2:5], both[:, :, 5]
        q_all, k_all, v_all, gts = _in_call(
            xs, mod1, layer, n_lat, g3(g_attn), w_in_p, g3(g_cq), w_uq_p, g3(g_ckv), w_uk_p, w_uvt,
            w_vbt, g_qa_p, g_ka_p, g_qb_p, g_kb_p, tables)
        o_lat = _attn_lat_call(q_all, k_all, v_all, n_lat)
        o_ctx = o_lat if last else _attn_ctx_call(q_all, k_all, v_all, n_lat)
        xn, fext, counts = _mid_call(o_lat, o_ctx, gts, xs, modm, layer, n_rows, n_lat, w_oa_b,
                                     w_ob_b, w_out_b, g3(g_ffn), wr_hi, wr_lo, br_col)
        n_tok = b * n_rows
        n_tiles_max = n_tok // MOE_TILE + N_PAIR_CLASSES
        fext = fext.reshape(n_tok, F_EXT)
        plan = _route_plan(fext[:, D_MODEL:D_MODEL + 4], counts, n_tiles_max)
        fs = _scatter_call(plan[0], fext, n_tiles_max * MOE_TILE)
        ys = _moe_call(plan, fs, layer, wg_all, wu_all, wd_all)
        xs = _gather_call(plan[0], xn, gt2, ys, n_lat)
    return xs
```

```python
import functools

import numpy as np
import jax
import jax.numpy as jnp
from jax import lax
from jax.experimental import pallas as pl
from jax.experimental.pallas import tpu as pltpu

D_MODEL = 1024
GRID_W = 64
MLA_HEADS = 8
MLA_NOPE = 64
MLA_ROPE = 32
MLA_QK = MLA_NOPE + MLA_ROPE
MLA_V = 64
Q_LORA = 256
KV_LORA = 128
GQA_HEADS = 8
GQA_KV_HEADS = 2
GQA_GROUP = GQA_HEADS // GQA_KV_HEADS
GQA_HD = 64
N_EXPERTS = 16
N_GROUPS = 4
EXPERTS_PER_GROUP = N_EXPERTS // N_GROUPS
FF_EXPERT = 512
ROPE_THETA = 10000.0
EPS = 1e-6
IN_SIZES = (Q_LORA, KV_LORA, MLA_ROPE, GQA_HEADS * GQA_HD, GQA_KV_HEADS * GQA_HD,
            GQA_KV_HEADS * GQA_HD, 2 * D_MODEL)

LANES = 128
ROW_TILE = 256
Q_TILE = 1024
KV_CHUNK = 512
N_HEADS_ALL = MLA_HEADS + GQA_HEADS
N_KHEADS_ALL = MLA_HEADS + GQA_KV_HEADS
LOG2_E = 1.4426950408889634
ONES_ROW = MLA_V
VMEM_LIMIT = 56 * 1024 * 1024
MOE_TILE = 256
N_CLASSES = N_GROUPS << EXPERTS_PER_GROUP
N_PAIR_CLASSES = N_GROUPS * 6
F_EXT = D_MODEL + LANES
META_CLS, META_RANK, META_W_LO, META_W_HI = 0, 1, 2, 3

C_CQ = 0
C_CKV = C_CQ + Q_LORA
C_KR = C_CKV + KV_LORA
C_QB = C_KR + LANES
C_KB = C_QB + GQA_HEADS * LANES
C_G = C_KB + GQA_KV_HEADS * LANES
C_END = C_G + 2 * D_MODEL

BF16 = jnp.bfloat16
F32 = jnp.float32


def _dot(a, b):
    return jnp.dot(a, b, preferred_element_type=F32)


def _dot_nt(a, b):
    return lax.dot_general(a, b, (((1,), (1,)), ((), ())), preferred_element_type=F32)


def _split_bf16(x):
    hi = x.astype(BF16)
    lo = (x - hi.astype(F32)).astype(BF16)
    return hi, lo


def _rms_rows(x, g):
    ms = jnp.mean(x * x, axis=-1, keepdims=True)
    return x * lax.rsqrt(ms + EPS) * g


def _head_norm(blk, g, dim):
    ms = jnp.sum(blk * blk, axis=-1, keepdims=True) * (1.0 / dim)
    return blk * lax.rsqrt(ms + EPS) * g


def _rope(x, cos, sin_a, sin_b, shift):
    up = pltpu.roll(x, LANES - shift, 1)
    down = pltpu.roll(x, shift, 1)
    return x * cos + up * sin_a + down * sin_b


def _mod_kernel(c_ref, w_ref, b_ref, o_ref):
    c = c_ref[...]
    s_hi, s_lo = _split_bf16(c * jax.nn.sigmoid(c))
    w_hi, w_lo = _split_bf16(w_ref[0])
    o_ref[...] = _dot(s_hi, w_hi) + _dot(s_lo, w_hi) + _dot(s_hi, w_lo) + b_ref[0]


def _mod_call(cs, w_mod, b_mod, layer):
    n_cols = w_mod.shape[2]
    tn = 1536
    return pl.pallas_call(
        _mod_kernel,
        grid=(n_cols // tn,),
        in_specs=[
            pl.BlockSpec(cs.shape, lambda j: (0, 0)),
            pl.BlockSpec((1, D_MODEL, tn), lambda j: (layer, 0, j)),
            pl.BlockSpec((1, 1, tn), lambda j: (layer, 0, j)),
        ],
        out_specs=pl.BlockSpec((cs.shape[0], tn), lambda j: (0, j)),
        out_shape=jax.ShapeDtypeStruct((cs.shape[0], n_cols), F32),
        compiler_params=pltpu.CompilerParams(
            dimension_semantics=("arbitrary",), vmem_limit_bytes=VMEM_LIMIT),
        name="mod",
    )(cs, w_mod, b_mod)


def _in_kernel(x_ref, mod_ref, gattn_ref, win_ref, gcq_ref, wuq_ref, gckv_ref, wuk_ref, wuvt_ref,
               wvbt_ref, gqa_ref, gka_ref, gqb_ref, gkb_ref,
               cosa_ref, sinaa_ref, sinba_ref, cosb_ref, sinab_ref, sinbb_ref,
               q_ref, k_ref, v_ref, g_ref):
    x = x_ref[0]
    shift = mod_ref[0, 0, 0:1, :]
    scale = mod_ref[0, 0, 1:2, :]
    h = (_rms_rows(x, gattn_ref[0]) * (1 + scale) + shift).astype(BF16)

    g_ref[0] = _dot(h, win_ref[0, :, C_G:C_END])

    cos_a, sin_aa, sin_ba = cosa_ref[...], sinaa_ref[...], sinba_ref[...]
    cos_b, sin_ab, sin_bb = cosb_ref[...], sinab_ref[...], sinbb_ref[...]
    scale_a = MLA_QK ** -0.5 * LOG2_E
    scale_b = GQA_HD ** -0.5 * LOG2_E
    half_a = MLA_ROPE // 4
    half_b = GQA_HD // 4
    t = x.shape[0]
    ones_row = lax.broadcasted_iota(jnp.int32, (LANES, t), 0) == ONES_ROW

    p0 = _dot(h, win_ref[0, :, C_CQ:C_QB])
    cq = p0[:, C_CQ:C_CKV]
    ckv = p0[:, C_CKV:C_KR]
    kr_blk = p0[:, C_KR:C_QB]
    qa = _dot(_rms_rows(cq, gcq_ref[0]).astype(BF16), wuq_ref[0])
    ckv_n = _rms_rows(ckv, gckv_ref[0]).astype(BF16)
    ka = _dot(ckv_n, wuk_ref[0])
    va_t = _dot_nt(wuvt_ref[0], ckv_n)
    g_qa, g_ka = gqa_ref[0], gka_ref[0]
    for hd in range(MLA_HEADS):
        qh = _head_norm(qa[:, hd * LANES:(hd + 1) * LANES], g_qa, MLA_QK)
        q_ref[0, hd] = (_rope(qh, cos_a, sin_aa, sin_ba, half_a) * scale_a).astype(BF16)
        kh = _head_norm(ka[:, hd * LANES:(hd + 1) * LANES] + kr_blk, g_ka, MLA_QK)
        k_ref[0, hd] = _rope(kh, cos_a, sin_aa, sin_ba, half_a).astype(BF16)
        v_ref[0, hd, 0] = jnp.where(ones_row, 1.0, va_t[hd * LANES:(hd + 1) * LANES, :]).astype(BF16)

    pq = _dot(h, win_ref[0, :, C_QB:C_KB])
    g_qb, g_kb = gqb_ref[0], gkb_ref[0]
    for hd in range(GQA_HEADS):
        qh = _head_norm(pq[:, hd * LANES:(hd + 1) * LANES], g_qb, GQA_HD)
        q_ref[0, MLA_HEADS + hd] = (
            _rope(qh, cos_b, sin_ab, sin_bb, half_b) * scale_b).astype(BF16)
    pk = _dot(h, win_ref[0, :, C_KB:C_G])
    vb_t = _dot_nt(wvbt_ref[0], h)
    for hd in range(GQA_KV_HEADS):
        kh = _head_norm(pk[:, hd * LANES:(hd + 1) * LANES], g_kb, GQA_HD)
        k_ref[0, MLA_HEADS + hd] = _rope(kh, cos_b, sin_ab, sin_bb, half_b).astype(BF16)
        v_ref[0, MLA_HEADS + hd, 0] = jnp.where(
            ones_row, 1.0, vb_t[hd * LANES:(hd + 1) * LANES, :]).astype(BF16)


def _in_call(xs, mod1, layer, n_lat, g_attn, w_in_p, g_cq, w_uq_p, g_ckv, w_uk_p, w_uvt, w_vbt,
             g_qa, g_ka, g_qb, g_kb, tables):
    b, n, _ = xs.shape
    nt = n // ROW_TILE
    n_lat_tiles = n_lat // ROW_TILE
    t = ROW_TILE

    def lspec(arr):
        shp = arr.shape
        return pl.BlockSpec((1,) + shp[1:], lambda i, j: (layer,) + (0,) * (len(shp) - 1))

    tab_spec = pl.BlockSpec((t, LANES), lambda i, j: (j, 0))
    return pl.pallas_call(
        _in_kernel,
        grid=(b, nt),
        in_specs=[
            pl.BlockSpec((1, t, D_MODEL), lambda i, j: (i, j, 0)),
            pl.BlockSpec((1, 1, 2, D_MODEL), lambda i, j: (i, jnp.where(j >= n_lat_tiles, 1, 0), 0, 0)),
            lspec(g_attn), lspec(w_in_p), lspec(g_cq), lspec(w_uq_p), lspec(g_ckv),
            lspec(w_uk_p), lspec(w_uvt), lspec(w_vbt), lspec(g_qa), lspec(g_ka), lspec(g_qb),
            lspec(g_kb),
        ] + [tab_spec] * 6,
        out_specs=[
            pl.BlockSpec((1, N_HEADS_ALL, t, LANES), lambda i, j: (i, 0, j, 0)),
            pl.BlockSpec((1, N_KHEADS_ALL, t, LANES), lambda i, j: (i, 0, j, 0)),
            pl.BlockSpec((1, N_KHEADS_ALL, 1, LANES, t), lambda i, j: (i, 0, j, 0, 0)),
            pl.BlockSpec((1, t, 2 * D_MODEL), lambda i, j: (i, j, 0)),
        ],
        out_shape=[
            jax.ShapeDtypeStruct((b, N_HEADS_ALL, n, LANES), BF16),
            jax.ShapeDtypeStruct((b, N_KHEADS_ALL, n, LANES), BF16),
            jax.ShapeDtypeStruct((b, N_KHEADS_ALL, nt, LANES, t), BF16),
            jax.ShapeDtypeStruct((b, n, 2 * D_MODEL), F32),
        ],
        compiler_params=pltpu.CompilerParams(
            dimension_semantics=("arbitrary", "arbitrary"), vmem_limit_bytes=VMEM_LIMIT),
        name="mixer_in",
    )(xs, mod1, g_attn, w_in_p, g_cq, w_uq_p, g_ckv, w_uk_p, w_uvt, w_vbt,
      g_qa, g_ka, g_qb, g_kb, *tables)


def _attn_kernel(q_ref, k0_ref, k1_ref, v0_ref, v1_ref, *rest, n_main, tail_chunk, pipelined):
    if pipelined:
        o_ref, s_scr, acc_scr = rest[-3:]
    else:
        o_ref, acc_scr = rest[-2:]
    qs = (q_ref[0, 0], q_ref[0, 1])
    k_refs = (k0_ref, k1_ref)
    v_refs = (v0_ref, v1_ref)
    tq = qs[0].shape[0]
    per = KV_CHUNK // ROW_TILE

    def fold(hh, m, cmax, st, chunk0, n_sub, first=False):
        m_new = cmax if first else jnp.maximum(m, cmax)
        pt = jnp.exp2(st - m_new).astype(BF16)
        pv = _dot(v_refs[hh][0, 0, chunk0], pt[0:ROW_TILE])
        for u in range(1, n_sub):
            pv = pv + _dot(v_refs[hh][0, 0, chunk0 + u], pt[u * ROW_TILE:(u + 1) * ROW_TILE])
        if first:
            acc_scr[hh] = pv
        else:
            acc_scr[hh] = jnp.exp2(m - m_new) * acc_scr[hh] + pv
        return m_new

    def produce(c, slot):
        cmax = []
        for hh in range(2):
            off = c * KV_CHUNK
            if not isinstance(off, int):
                off = pl.multiple_of(off, KV_CHUNK)
            st = _dot_nt(k_refs[hh][0, 0, pl.ds(off, KV_CHUNK), :], qs[hh])
            s_scr[slot, hh] = st
            cmax.append(jnp.max(st, axis=0, keepdims=True))
        return tuple(cmax)

    def consume(c, slot, ms, cmax):
        return tuple(fold(hh, ms[hh], cmax[hh], s_scr[slot, hh], c * per, per) for hh in range(2))

    ms = []
    for hh in range(2):
        st = _dot_nt(k_refs[hh][0, 0, pl.ds(tail_chunk * ROW_TILE, ROW_TILE), :], qs[hh])
        ms.append(fold(hh, None, jnp.max(st, axis=0, keepdims=True), st, tail_chunk, 1, first=True))
    ms = tuple(ms)

    if pipelined:
        def body(i, carry):
            ms, cmax = carry
            cmax1 = produce(2 * i + 1, 1)
            ms = consume(2 * i, 0, ms, cmax)
            cmax2 = produce(2 * i + 2, 0)
            ms = consume(2 * i + 1, 1, ms, cmax1)
            return ms, cmax2

        ms, cmax = lax.fori_loop(0, (n_main - 2) // 2, body, (ms, produce(0, 0)))
        cmax1 = produce(n_main - 1, 1)
        ms = consume(n_main - 2, 0, ms, cmax)
        ms = consume(n_main - 1, 1, ms, cmax1)

    outs = []
    for hh in range(2):
        acc = acc_scr[hh]
        outs.append((acc / acc[ONES_ROW:ONES_ROW + 1, :]).T)
    low = lax.broadcasted_iota(jnp.int32, (tq, LANES), 1) < MLA_V
    o_ref[0] = jnp.where(low, outs[0], pltpu.roll(outs[1], MLA_V, 1)).astype(o_ref.dtype)


def _kv_head(p):
    mla_pairs = MLA_HEADS // 2
    pairs_per_kv = GQA_GROUP // 2
    gqa = MLA_HEADS + (p - mla_pairs) // pairs_per_kv
    return jnp.where(p < mla_pairs, 2 * p, gqa), jnp.where(p < mla_pairs, 2 * p + 1, gqa)


def _attn_lat_call(q_all, k_all, v_all, n_lat):
    b, _, n, _ = q_all.shape
    n_pairs = N_HEADS_ALL // 2
    n_chunks = n // ROW_TILE
    n_main = n_lat // KV_CHUNK
    assert n_main >= 2 and n_main % 2 == 0
    kern = functools.partial(_attn_kernel, n_main=n_main, tail_chunk=n_lat // ROW_TILE,
                             pipelined=True)
    k_spec = lambda which: pl.BlockSpec(
        (1, 1, n, LANES), lambda i, p, j: (i, _kv_head(p)[which], 0, 0))
    v_spec = lambda which: pl.BlockSpec(
        (1, 1, n_chunks, LANES, ROW_TILE), lambda i, p, j: (i, _kv_head(p)[which], 0, 0, 0))
    return pl.pallas_call(
        kern,
        grid=(b, n_pairs, n_lat // Q_TILE),
        in_specs=[pl.BlockSpec((1, 2, Q_TILE, LANES), lambda i, p, j: (i, p, j, 0)),
                  k_spec(0), k_spec(1), v_spec(0), v_spec(1)],
        out_specs=pl.BlockSpec((1, Q_TILE, LANES), lambda i, p, j: (i, j, p)),
        out_shape=jax.ShapeDtypeStruct((b, n_lat, n_pairs * LANES), BF16),
        scratch_shapes=[pltpu.VMEM((2, 2, KV_CHUNK, Q_TILE), F32),
                        pltpu.VMEM((2, LANES, Q_TILE), F32)],
        compiler_params=pltpu.CompilerParams(
            dimension_semantics=("arbitrary", "arbitrary", "arbitrary"),
            vmem_limit_bytes=VMEM_LIMIT),
        name="attention",
    )(q_all, k_all, k_all, v_all, v_all)


def _attn_ctx_call(q_all, k_all, v_all, n_lat):
    b, _, n, _ = q_all.shape
    n_pairs = N_HEADS_ALL // 2
    c = n_lat // ROW_TILE
    assert n - n_lat == ROW_TILE
    kern = functools.partial(_attn_kernel, n_main=0, tail_chunk=0, pipelined=False)
    k_spec = lambda which: pl.BlockSpec(
        (1, 1, ROW_TILE, LANES), lambda i, p: (i, _kv_head(p)[which], c, 0))
    v_spec = lambda which: pl.BlockSpec(
        (1, 1, 1, LANES, ROW_TILE), lambda i, p: (i, _kv_head(p)[which], c, 0, 0))
    return pl.pallas_call(
        kern,
        grid=(b, n_pairs),
        in_specs=[pl.BlockSpec((1, 2, ROW_TILE, LANES), lambda i, p: (i, p, c, 0)),
                  k_spec(0), k_spec(1), v_spec(0), v_spec(1)],
        out_specs=pl.BlockSpec((1, ROW_TILE, LANES), lambda i, p: (i, 0, p)),
        out_shape=jax.ShapeDtypeStruct((b, ROW_TILE, n_pairs * LANES), BF16),
        scratch_shapes=[pltpu.VMEM((2, LANES, ROW_TILE), F32)],
        compiler_params=pltpu.CompilerParams(
            dimension_semantics=("arbitrary", "arbitrary"), vmem_limit_bytes=VMEM_LIMIT),
        name="attention_ctx",
    )(q_all, k_all, k_all, v_all, v_all)


def _route_rows(sel, scores):
    rows = [sel[e:e + 1, :] for e in range(N_EXPERTS)]
    srow = [scores[e:e + 1, :] for e in range(N_EXPERTS)]
    grp = []
    for g in range(N_GROUPS):
        a, b, c, d = rows[4 * g:4 * g + 4]
        hi1, lo1 = jnp.maximum(a, b), jnp.minimum(a, b)
        hi2, lo2 = jnp.maximum(c, d), jnp.minimum(c, d)
        top1 = jnp.maximum(hi1, hi2)
        top2 = jnp.maximum(jnp.minimum(hi1, hi2), jnp.maximum(lo1, lo2))
        grp.append(top1 + top2)
    best = jnp.zeros_like(grp[0], dtype=jnp.int32)
    best_v = grp[0]
    for g in range(1, N_GROUPS):
        upd = grp[g] > best_v
        best = jnp.where(upd, g, best)
        best_v = jnp.where(upd, grp[g], best_v)
    picked = []
    keeps = []
    for e in range(N_EXPERTS):
        g = e // EXPERTS_PER_GROUP
        rank = jnp.zeros_like(best)
        for o in range(4 * g, 4 * g + 4):
            if o == e:
                continue
            ahead = rows[o] > rows[e]
            if o < e:
                ahead = ahead | (rows[o] == rows[e])
            rank = rank + ahead.astype(jnp.int32)
        keep = (rank < 2) & (best == g)
        picked.append(jnp.where(keep, srow[e], 0.0))
        keeps.append(keep)
    total = picked[0]
    for e in range(1, N_EXPERTS):
        total = total + picked[e]
    return [p / total for p in picked], best, keeps


def _route_meta(gates, best, keeps):
    mask = jnp.zeros_like(best)
    w_lo = jnp.zeros_like(gates[0])
    w_hi = jnp.zeros_like(gates[0])
    for e in range(N_EXPERTS):
        g, i = divmod(e, EXPERTS_PER_GROUP)
        mask = mask + jnp.where(keeps[e], 1 << i, 0)
        if i == 0:
            w_lo = w_lo + jnp.where(keeps[e], gates[e], 0.0)
            continue
        below = keeps[4 * g]
        for o in range(4 * g + 1, e):
            below = below | keeps[o]
        w_lo = w_lo + jnp.where(keeps[e] & ~below, gates[e], 0.0)
        w_hi = w_hi + jnp.where(keeps[e] & below, gates[e], 0.0)
    cls = (best * (1 << EXPERTS_PER_GROUP) + mask).astype(F32)
    return cls, w_lo, w_hi


def _mid_kernel(o_ref, octx_ref, g_ref, x_ref, mod_ref, woa_ref, wob_ref, wout_ref, gffn_ref,
                wrh_ref, wrl_ref, brc_ref, xn_ref, fext_ref, cnt_ref, run_ref, *, n_lat_tiles):
    o = jnp.where(pl.program_id(1) >= n_lat_tiles, octx_ref[0], o_ref[0])
    n_a = MLA_HEADS * MLA_V
    ya = _dot(o[:, 0:n_a], woa_ref[0])
    yb = _dot(o[:, n_a:], wob_ref[0])
    g = g_ref[0]
    y = jax.nn.sigmoid(g[:, 0:D_MODEL]) * ya + jax.nn.sigmoid(g[:, D_MODEL:]) * yb
    z = _dot(y.astype(BF16), wout_ref[0])
    gt1 = mod_ref[0, 0, 0:1, :]
    sh2 = mod_ref[0, 0, 1:2, :]
    sc2 = mod_ref[0, 0, 2:3, :]
    xn = x_ref[0] + gt1 * z
    xn_ref[0] = xn
    f = _rms_rows(xn, gffn_ref[0]) * (1 + sc2) + sh2
    fext_ref[0, :, 0:D_MODEL] = f
    f_hi, f_lo = _split_bf16(f)
    w_hi = wrh_ref[...]
    logits = _dot(f_hi, w_hi) + _dot(f_lo, w_hi) + _dot(f_hi, wrl_ref[...])
    scores_t = jax.nn.sigmoid(logits).T
    scores = scores_t[0:N_EXPERTS, :]
    sel = scores + brc_ref[0:N_EXPERTS, :]
    cls, w_lo, w_hi_gate = _route_meta(*_route_rows(sel, scores))
    t = scores.shape[1]
    row_id = lax.broadcasted_iota(jnp.int32, (LANES, t), 0)
    meta_t = jnp.where(row_id == META_CLS, cls, 0.0)
    meta_t = jnp.where(row_id == META_W_LO, w_lo, meta_t)
    meta_t = jnp.where(row_id == META_W_HI, w_hi_gate, meta_t)
    meta = meta_t.T

    @pl.when((pl.program_id(0) == 0) & (pl.program_id(1) == 0))
    def _():
        run_ref[...] = jnp.zeros_like(run_ref)

    lane = lax.broadcasted_iota(jnp.int32, (t, LANES), 1)
    onehot = jnp.where(lane.astype(F32) == meta[:, META_CLS:META_CLS + 1], 1.0, 0.0)
    earlier = jnp.where(lax.broadcasted_iota(jnp.int32, (t, t), 0)
                        > lax.broadcasted_iota(jnp.int32, (t, t), 1), 1.0, 0.0).astype(BF16)
    before = _dot(earlier, onehot.astype(BF16)) + run_ref[...]
    rank = jnp.sum(before * onehot, axis=-1, keepdims=True)
    fext_ref[0, :, D_MODEL:] = jnp.where(lane == META_RANK, rank, meta)
    run_ref[...] += jnp.sum(onehot, axis=0, keepdims=True)
    cnt_ref[...] = jnp.broadcast_to(run_ref[...], cnt_ref.shape)


def _mid_call(o_lat, o_ctx, gts, xs, modm, layer, n_rows, n_lat, w_oa, w_ob, w_out, g_ffn, wr_hi,
              wr_lo, br_col):
    b = xs.shape[0]
    n = n_rows
    n_lat_tiles = n_lat // ROW_TILE
    t = ROW_TILE

    def lspec(arr):
        shp = arr.shape
        return pl.BlockSpec((1,) + shp[1:], lambda i, j: (layer,) + (0,) * (len(shp) - 1))

    def full(arr):
        return pl.BlockSpec(arr.shape, lambda i, j: (0,) * arr.ndim)

    return pl.pallas_call(
        functools.partial(_mid_kernel, n_lat_tiles=n_lat_tiles),
        grid=(b, n // t),
        in_specs=[
            pl.BlockSpec((1, t, o_lat.shape[2]), lambda i, j: (i, jnp.minimum(j, n_lat_tiles - 1), 0)),
            pl.BlockSpec((1, t, o_ctx.shape[2]), lambda i, j: (i, 0, 0)),
            pl.BlockSpec((1, t, 2 * D_MODEL), lambda i, j: (i, j, 0)),
            pl.BlockSpec((1, t, D_MODEL), lambda i, j: (i, j, 0)),
            pl.BlockSpec((1, 1, 3, D_MODEL), lambda i, j: (i, jnp.where(j >= n_lat_tiles, 1, 0), 0, 0)),
            lspec(w_oa), lspec(w_ob), lspec(w_out), lspec(g_ffn),
            full(wr_hi), full(wr_lo), full(br_col),
        ],
        out_specs=[
            pl.BlockSpec((1, t, D_MODEL), lambda i, j: (i, j, 0)),
            pl.BlockSpec((1, t, F_EXT), lambda i, j: (i, j, 0)),
            pl.BlockSpec((8, LANES), lambda i, j: (0, 0)),
        ],
        out_shape=[
            jax.ShapeDtypeStruct((b, n, D_MODEL), F32),
            jax.ShapeDtypeStruct((b, n, F_EXT), F32),
            jax.ShapeDtypeStruct((8, LANES), F32),
        ],
        scratch_shapes=[pltpu.VMEM((1, LANES), F32)],
        compiler_params=pltpu.CompilerParams(
            dimension_semantics=("arbitrary", "arbitrary"), vmem_limit_bytes=VMEM_LIMIT),
        name="mixer_out_router",
    )(o_lat, o_ctx, gts, xs, modm, w_oa, w_ob, w_out, g_ffn, wr_hi, wr_lo, br_col)


def _class_expert_tables():
    lo = np.zeros((N_CLASSES,), np.int32)
    hi = np.zeros((N_CLASSES,), np.int32)
    for c in range(N_CLASSES):
        g, mask = divmod(c, 1 << EXPERTS_PER_GROUP)
        bits = [i for i in range(EXPERTS_PER_GROUP) if mask >> i & 1]
        if len(bits) == 2:
            lo[c], hi[c] = EXPERTS_PER_GROUP * g + bits[0], EXPERTS_PER_GROUP * g + bits[1]
    return lo, hi


def _route_plan(meta, counts, n_tiles_max):
    cls = meta[:, META_CLS].astype(jnp.int32)
    rank = meta[:, META_RANK].astype(jnp.int32)
    cnt = counts[0, :N_CLASSES].astype(jnp.int32)
    tiles = (cnt + MOE_TILE - 1) // MOE_TILE
    tile_end = jnp.cumsum(tiles)
    pos = (tile_end - tiles)[cls] * MOE_TILE + rank
    n_used = tile_end[-1]
    tile_idx = jnp.minimum(jnp.arange(n_tiles_max, dtype=jnp.int32), n_used - 1)
    tile_cls = jnp.sum(tile_end[None, :] <= tile_idx[:, None], axis=1).astype(jnp.int32)
    lo, hi = _class_expert_tables()
    src = jnp.zeros((n_tiles_max * MOE_TILE,), jnp.int32).at[pos].set(
        jnp.arange(pos.shape[0], dtype=jnp.int32))
    return (pos.astype(jnp.int32), src, tile_idx, jnp.asarray(lo)[tile_cls],
            jnp.asarray(hi)[tile_cls], n_used.reshape(1).astype(jnp.int32))


def _row_copy(src_ref, src_row, dst_ref, dst_row, sem):
    return pltpu.make_async_copy(src_ref.at[pl.ds(src_row, 1)], dst_ref.at[pl.ds(dst_row, 1)], sem)


def _moe_kernel(src_ref, tidx_ref, elo_ref, ehi_ref, nused_ref, f_hbm, wgs_ref, wus_ref, wds_ref,
                wg1_ref, wu1_ref, wd1_ref, wg2_ref, wu2_ref, wd2_ref, ys_ref, buf, sems):
    del elo_ref, ehi_ref
    j = pl.program_id(0)
    slot = j % 2

    def fetch(step, into):
        base = tidx_ref[step] * MOE_TILE
        for r in range(MOE_TILE):
            _row_copy(f_hbm, src_ref[base + r], buf.at[into], r, sems.at[into]).start()

    pl.when(j == 0)(lambda: fetch(0, 0))
    pl.when(j + 1 < pl.num_programs(0))(lambda: fetch(j + 1, 1 - slot))
    for _ in range(MOE_TILE):
        _row_copy(f_hbm, 0, buf.at[slot], 0, sems.at[slot]).wait()
    used = j < nused_ref[0]

    @pl.when(jnp.logical_not(used))
    def _():
        ys_ref[...] = jnp.zeros_like(ys_ref)

    @pl.when(used)
    def _():
        rows = buf[slot]
        x = rows[:, 0:D_MODEL].astype(BF16)
        w_lo = rows[:, D_MODEL + META_W_LO:D_MODEL + META_W_LO + 1]
        w_hi = rows[:, D_MODEL + META_W_HI:D_MODEL + META_W_HI + 1]

        def ffn(wg_ref, wu_ref, wd_ref):
            a = jax.nn.silu(_dot(x, wg_ref[0, 0])) * _dot(x, wu_ref[0, 0])
            return _dot(a.astype(BF16), wd_ref[0, 0])

        ys_ref[...] = (ffn(wgs_ref, wus_ref, wds_ref) + w_lo * ffn(wg1_ref, wu1_ref, wd1_ref)
                       + w_hi * ffn(wg2_ref, wu2_ref, wd2_ref))


def _moe_call(plan, fext, layer, wg_all, wu_all, wd_all):
    _, src, tile_idx, e_lo, e_hi, n_used = plan
    n_tiles = tile_idx.shape[0]

    def up_spec(which):
        return pl.BlockSpec((1, 1, D_MODEL, FF_EXPERT),
                            lambda j, s, t, lo, hi, nu: (layer, which(j, lo, hi), 0, 0))

    def down_spec(which):
        return pl.BlockSpec((1, 1, FF_EXPERT, D_MODEL),
                            lambda j, s, t, lo, hi, nu: (layer, which(j, lo, hi), 0, 0))

    shared = lambda j, lo, hi: N_EXPERTS
    first = lambda j, lo, hi: lo[j]
    second = lambda j, lo, hi: hi[j]
    return pl.pallas_call(
        _moe_kernel,
        grid_spec=pltpu.PrefetchScalarGridSpec(
            num_scalar_prefetch=5,
            grid=(n_tiles,),
            in_specs=[pl.BlockSpec(memory_space=pl.ANY),
                      up_spec(shared), up_spec(shared), down_spec(shared),
                      up_spec(first), up_spec(first), down_spec(first),
                      up_spec(second), up_spec(second), down_spec(second)],
            out_specs=pl.BlockSpec((MOE_TILE, D_MODEL), lambda j, s, t, lo, hi, nu: (j, 0)),
            scratch_shapes=[pltpu.VMEM((2, MOE_TILE, F_EXT), F32),
                            pltpu.SemaphoreType.DMA((2,))],
        ),
        out_shape=jax.ShapeDtypeStruct((n_tiles * MOE_TILE, D_MODEL), F32),
        compiler_params=pltpu.CompilerParams(
            dimension_semantics=("arbitrary",), vmem_limit_bytes=VMEM_LIMIT),
        name="moe",
    )(src, tile_idx, e_lo, e_hi, n_used, fext, wg_all, wu_all, wd_all, wg_all, wu_all, wd_all,
      wg_all, wu_all, wd_all)


def _gather_kernel(pos_ref, xn_ref, gt_ref, ys_hbm, o_ref, buf, sems):
    nt = pl.num_programs(1)
    step = pl.program_id(0) * nt + pl.program_id(1)
    total = pl.num_programs(0) * nt
    slot = step % 2

    def fetch(s, into):
        for r in range(MOE_TILE):
            _row_copy(ys_hbm, pos_ref[s * MOE_TILE + r], buf.at[into], r, sems.at[into]).start()

    pl.when(step == 0)(lambda: fetch(0, 0))
    pl.when(step + 1 < total)(lambda: fetch(step + 1, 1 - slot))
    for _ in range(MOE_TILE):
        _row_copy(ys_hbm, 0, buf.at[slot], 0, sems.at[slot]).wait()
    o_ref[0] = xn_ref[0] + gt_ref[0, 0] * buf[slot]


def _gather_call(pos, xn, gt2, ys, n_lat):
    b, n, _ = xn.shape
    n_lat_tiles = n_lat // MOE_TILE
    return pl.pallas_call(
        _gather_kernel,
        grid_spec=pltpu.PrefetchScalarGridSpec(
            num_scalar_prefetch=1,
            grid=(b, n // MOE_TILE),
            in_specs=[
                pl.BlockSpec((1, MOE_TILE, D_MODEL), lambda i, j, p: (i, j, 0)),
                pl.BlockSpec((1, 1, 1, D_MODEL),
                             lambda i, j, p: (i, jnp.where(j >= n_lat_tiles, 1, 0), 0, 0)),
                pl.BlockSpec(memory_space=pl.ANY),
            ],
            out_specs=pl.BlockSpec((1, MOE_TILE, D_MODEL), lambda i, j, p: (i, j, 0)),
            scratch_shapes=[pltpu.VMEM((2, MOE_TILE, D_MODEL), F32),
                            pltpu.SemaphoreType.DMA((2,))],
        ),
        out_shape=jax.ShapeDtypeStruct((b, n, D_MODEL), F32),
        compiler_params=pltpu.CompilerParams(
            dimension_semantics=("arbitrary", "arbitrary"), vmem_limit_bytes=VMEM_LIMIT),
        name="moe_gather_residual",
    )(pos, xn, gt2[:, :, None, :], ys)


def _pad_heads(w, n_heads, dim):
    lead = w.shape[:-1]
    w = w.reshape(lead + (n_heads, dim))
    w = jnp.pad(w, [(0, 0)] * len(lead) + [(0, 0), (0, LANES - dim)])
    return w.reshape(lead + (n_heads * LANES,))


def _prep_w_in(w_in):
    offs = np.concatenate([[0], np.cumsum(IN_SIZES)])
    cq, ckv, kr, qb, kb, vb, gts = [w_in[..., int(offs[i]):int(offs[i + 1])] for i in range(7)]
    kr_blk = jnp.pad(kr, [(0, 0), (0, 0), (MLA_NOPE, LANES - MLA_QK)])
    w_main = jnp.concatenate(
        [cq, ckv, kr_blk, _pad_heads(qb, GQA_HEADS, GQA_HD), _pad_heads(kb, GQA_KV_HEADS, GQA_HD),
         gts], axis=-1).astype(BF16)
    w_vbt = jnp.swapaxes(_pad_heads(vb, GQA_KV_HEADS, GQA_HD), 1, 2).astype(BF16)
    return w_main, w_vbt


def _prep_w_ukv(w_ukv):
    l, r, _ = w_ukv.shape
    w = w_ukv.reshape(l, r, MLA_HEADS, MLA_NOPE + MLA_V)
    uk = _pad_heads(w[..., :MLA_NOPE].reshape(l, r, MLA_HEADS * MLA_NOPE), MLA_HEADS, MLA_NOPE)
    uv = _pad_heads(w[..., MLA_NOPE:].reshape(l, r, MLA_HEADS * MLA_V), MLA_HEADS, MLA_V)
    return uk.astype(BF16), jnp.swapaxes(uv, 1, 2).astype(BF16)


def _pad_gain(g, dim):
    return jnp.pad(g, [(0, 0), (0, LANES - dim)])[:, None, :]


def _rope_tables(n_ctx, n_lat, lane0, half_dim):
    pos = jnp.arange(n_lat)
    quarter = half_dim // 2
    inv = ROPE_THETA ** (-jnp.arange(0, half_dim, 2, dtype=F32) / half_dim)
    ang_r = (pos // GRID_W).astype(F32)[:, None] * inv[None, :]
    ang_c = (pos % GRID_W).astype(F32)[:, None] * inv[None, :]
    cos = jnp.concatenate([jnp.cos(ang_r)] * 2 + [jnp.cos(ang_c)] * 2, axis=-1)
    zero = jnp.zeros((n_lat, quarter), F32)
    sin_a = jnp.concatenate([-jnp.sin(ang_r), zero, -jnp.sin(ang_c), zero], axis=-1)
    sin_b = jnp.concatenate([zero, jnp.sin(ang_r), zero, jnp.sin(ang_c)], axis=-1)
    pad = [(0, n_ctx), (lane0, LANES - lane0 - 2 * half_dim)]
    return (jnp.pad(cos, pad, constant_values=1.0), jnp.pad(sin_a, pad), jnp.pad(sin_b, pad))


def kernel(x, c, ctx, c_ctx, w_mod, b_mod, g_attn, g_ffn, w_in, g_cq, w_uq, g_ckv, w_ukv,
           g_qa, g_ka, g_qb, g_kb, w_oa, w_ob, w_out, w_router, b_router,
           w_e_gate, w_e_up, w_e_down, w_s_gate, w_s_up, w_s_down):
    b, n_lat, d = x.shape
    n_ctx = ctx.shape[1]
    n = n_lat + n_ctx
    depth = w_mod.shape[0]
    assert d == D_MODEL and n_ctx == ROW_TILE and n_lat % KV_CHUNK == 0 and n_lat % Q_TILE == 0

    xs = jnp.concatenate([x, ctx], axis=1)
    cs = jnp.concatenate([c, c_ctx[None, :], jnp.zeros((8 - b - 1, d), F32)], axis=0)

    w_in_p, w_vbt = _prep_w_in(w_in)
    w_uq_p = _pad_heads(w_uq, MLA_HEADS, MLA_QK).astype(BF16)
    w_uk_p, w_uvt = _prep_w_ukv(w_ukv)
    g3 = lambda g: g[:, None, :]
    g_qa_p, g_ka_p = _pad_gain(g_qa, MLA_QK), _pad_gain(g_ka, MLA_QK)
    g_qb_p, g_kb_p = _pad_gain(g_qb, GQA_HD), _pad_gain(g_kb, GQA_HD)
    tables = (_rope_tables(n_ctx, n_lat, MLA_NOPE, MLA_ROPE // 2)
              + _rope_tables(n_ctx, n_lat, 0, GQA_HD // 2))
    w_oa_b, w_ob_b, w_out_b = w_oa.astype(BF16), w_ob.astype(BF16), w_out.astype(BF16)
    wr_p = jnp.pad(w_router, [(0, 0), (0, LANES - N_EXPERTS)])
    wr_hi = wr_p.astype(BF16)
    wr_lo = (wr_p - wr_hi.astype(F32)).astype(BF16)
    br_col = jnp.pad(b_router, (0, LANES - N_EXPERTS))[:, None]
    wg_all = jnp.concatenate([w_e_gate, w_s_gate[:, None]], axis=1).astype(BF16)
    wu_all = jnp.concatenate([w_e_up, w_s_up[:, None]], axis=1).astype(BF16)
    wd_all = jnp.concatenate([w_e_down, w_s_down[:, None]], axis=1).astype(BF16)
    b_mod3 = b_mod[:, None, :]

    for layer in range(depth):
        last = layer == depth - 1
        n_rows = n_lat if last else n
        mods = _mod_call(cs, w_mod, b_mod3, layer).reshape(8, 6, d)
        lat, cx = mods[:b], jnp.broadcast_to(mods[b:b + 1], (b, 6, d))
        both = jnp.stack([lat, cx], axis=1)
        mod1, modm, gt2 = both[:, :, 0:2], both[:, :, 2:5], both[:, :, 5]
        q_all, k_all, v_all, gts = _in_call(
            xs, mod1, layer, n_lat, g3(g_attn), w_in_p, g3(g_cq), w_uq_p, g3(g_ckv), w_uk_p, w_uvt,
            w_vbt, g_qa_p, g_ka_p, g_qb_p, g_kb_p, tables)
        o_lat = _attn_lat_call(q_all, k_all, v_all, n_lat)
        o_ctx = o_lat if last else _attn_ctx_call(q_all, k_all, v_all, n_lat)
        xn, fext, counts = _mid_call(o_lat, o_ctx, gts, xs, modm, layer, n_rows, n_lat, w_oa_b,
                                     w_ob_b, w_out_b, g3(g_ffn), wr_hi, wr_lo, br_col)
        n_tok = b * n_rows
        n_tiles_max = n_tok // MOE_TILE + N_PAIR_CLASSES
        fext = fext.reshape(n_tok, F_EXT)
        plan = _route_plan(fext[:, D_MODEL:D_MODEL + 4], counts, n_tiles_max)
        ys = _moe_call(plan, fext, layer, wg_all, wu_all, wd_all)
        xs = _gather_call(plan[0], xn, gt2, ys, n_lat)
    return xs
```

```python
import functools

import numpy as np
import jax
import jax.numpy as jnp
from jax import lax
from jax.experimental import pallas as pl
from jax.experimental.pallas import tpu as pltpu

D_MODEL = 1024
GRID_W = 64
MLA_HEADS = 8
MLA_NOPE = 64
MLA_ROPE = 32
MLA_QK = MLA_NOPE + MLA_ROPE
MLA_V = 64
Q_LORA = 256
KV_LORA = 128
GQA_HEADS = 8
GQA_KV_HEADS = 2
GQA_GROUP = GQA_HEADS // GQA_KV_HEADS
GQA_HD = 64
N_EXPERTS = 16
N_GROUPS = 4
EXPERTS_PER_GROUP = N_EXPERTS // N_GROUPS
FF_EXPERT = 512
ROPE_THETA = 10000.0
EPS = 1e-6
IN_SIZES = (Q_LORA, KV_LORA, MLA_ROPE, GQA_HEADS * GQA_HD, GQA_KV_HEADS * GQA_HD,
            GQA_KV_HEADS * GQA_HD, 2 * D_MODEL)

LANES = 128
ROW_TILE = 256
Q_TILE = 1024
KV_CHUNK = 512
N_HEADS_ALL = MLA_HEADS + GQA_HEADS
N_KHEADS_ALL = MLA_HEADS + GQA_KV_HEADS
LOG2_E = 1.4426950408889634
ONES_ROW = MLA_V
VMEM_LIMIT = 56 * 1024 * 1024
MOE_TILE = 256
N_CLASSES = N_GROUPS << EXPERTS_PER_GROUP
N_PAIR_CLASSES = N_GROUPS * 6
F_EXT = D_MODEL + LANES
META_CLS, META_RANK, META_W_LO, META_W_HI = 0, 1, 2, 3

C_CQ = 0
C_CKV = C_CQ + Q_LORA
C_KR = C_CKV + KV_LORA
C_QB = C_KR + 2 * LANES
C_KB = C_QB + 2 * GQA_HEADS * LANES
C_G = C_KB + 2 * GQA_KV_HEADS * LANES
C_END = C_G + 2 * D_MODEL

BF16 = jnp.bfloat16
F32 = jnp.float32


def _dot(a, b):
    return jnp.dot(a, b, preferred_element_type=F32)


def _dot_nt(a, b):
    return lax.dot_general(a, b, (((1,), (1,)), ((), ())), preferred_element_type=F32)


def _split_bf16(x):
    hi = x.astype(BF16)
    lo = (x - hi.astype(F32)).astype(BF16)
    return hi, lo


def _rms_rows(x, g):
    ms = jnp.mean(x * x, axis=-1, keepdims=True)
    return x * lax.rsqrt(ms + EPS) * g


def _norm_rope(raw, swapped, gain_cos, gain_sin, dim):
    ms = jnp.sum(raw * raw, axis=-1, keepdims=True) * (1.0 / dim)
    return (raw * gain_cos + swapped * gain_sin) * lax.rsqrt(ms + EPS)


def _mod_kernel(c_ref, w_ref, b_ref, o_ref):
    c = c_ref[...]
    s_hi, s_lo = _split_bf16(c * jax.nn.sigmoid(c))
    w_hi, w_lo = _split_bf16(w_ref[0])
    o_ref[...] = _dot(s_hi, w_hi) + _dot(s_lo, w_hi) + _dot(s_hi, w_lo) + b_ref[0]


def _mod_call(cs, w_mod, b_mod, layer):
    n_cols = w_mod.shape[2]
    tn = 1536
    return pl.pallas_call(
        _mod_kernel,
        grid=(n_cols // tn,),
        in_specs=[
            pl.BlockSpec(cs.shape, lambda j: (0, 0)),
            pl.BlockSpec((1, D_MODEL, tn), lambda j: (layer, 0, j)),
            pl.BlockSpec((1, 1, tn), lambda j: (layer, 0, j)),
        ],
        out_specs=pl.BlockSpec((cs.shape[0], tn), lambda j: (0, j)),
        out_shape=jax.ShapeDtypeStruct((cs.shape[0], n_cols), F32),
        compiler_params=pltpu.CompilerParams(
            dimension_semantics=("arbitrary",), vmem_limit_bytes=VMEM_LIMIT),
        name="mod",
    )(cs, w_mod, b_mod)


def _in_kernel(x_ref, mod_ref, gattn_ref, win_ref, gcq_ref, wuq_ref, gckv_ref, wuk_ref, wuvt_ref,
               wvbt_ref, gains_ref, cosa_ref, sina_ref, cosb_ref, sinb_ref,
               q_ref, k_ref, v_ref, g_ref):
    x = x_ref[0]
    shift = mod_ref[0, 0, 0:1, :]
    scale = mod_ref[0, 0, 1:2, :]
    h = (_rms_rows(x, gattn_ref[0]) * (1 + scale) + shift).astype(BF16)

    g_ref[0] = _dot(h, win_ref[0, :, C_G:C_END])

    gains = gains_ref[0]
    cos_a, sin_a, cos_b, sin_b = cosa_ref[...], sina_ref[...], cosb_ref[...], sinb_ref[...]
    scale_a = MLA_QK ** -0.5 * LOG2_E
    scale_b = GQA_HD ** -0.5 * LOG2_E
    qa_cos, qa_sin = cos_a * (gains[0:1] * scale_a), sin_a * (gains[1:2] * scale_a)
    ka_cos, ka_sin = cos_a * gains[2:3], sin_a * gains[3:4]
    qb_cos, qb_sin = cos_b * (gains[4:5] * scale_b), sin_b * (gains[5:6] * scale_b)
    kb_cos, kb_sin = cos_b * gains[6:7], sin_b * gains[7:8]
    t = x.shape[0]
    ones_row = lax.broadcasted_iota(jnp.int32, (LANES, t), 0) == ONES_ROW
    blk = lambda a, i: a[:, i * LANES:(i + 1) * LANES]

    p0 = _dot(h, win_ref[0, :, C_CQ:C_QB])
    cq = p0[:, C_CQ:C_CKV]
    ckv = p0[:, C_CKV:C_KR]
    kr_blk = p0[:, C_KR:C_KR + LANES]
    kr_swapped = p0[:, C_KR + LANES:C_QB]
    qa = _dot(_rms_rows(cq, gcq_ref[0]).astype(BF16), wuq_ref[0])
    ckv_n = _rms_rows(ckv, gckv_ref[0]).astype(BF16)
    ka = _dot(ckv_n, wuk_ref[0])
    va_t = _dot_nt(wuvt_ref[0], ckv_n)
    for hd in range(MLA_HEADS):
        q_ref[0, hd] = _norm_rope(blk(qa, hd), blk(qa, MLA_HEADS + hd), qa_cos, qa_sin,
                                  MLA_QK).astype(BF16)
        k_ref[0, hd] = _norm_rope(blk(ka, hd) + kr_blk, kr_swapped, ka_cos, ka_sin,
                                  MLA_QK).astype(BF16)
        v_ref[0, hd, 0] = jnp.where(ones_row, 1.0, va_t[hd * LANES:(hd + 1) * LANES, :]).astype(BF16)

    pq = _dot(h, win_ref[0, :, C_QB:C_KB])
    for hd in range(GQA_HEADS):
        q_ref[0, MLA_HEADS + hd] = _norm_rope(blk(pq, hd), blk(pq, GQA_HEADS + hd), qb_cos, qb_sin,
                                              GQA_HD).astype(BF16)
    pk = _dot(h, win_ref[0, :, C_KB:C_G])
    vb_t = _dot_nt(wvbt_ref[0], h)
    for hd in range(GQA_KV_HEADS):
        k_ref[0, MLA_HEADS + hd] = _norm_rope(blk(pk, hd), blk(pk, GQA_KV_HEADS + hd), kb_cos,
                                              kb_sin, GQA_HD).astype(BF16)
        v_ref[0, MLA_HEADS + hd, 0] = jnp.where(
            ones_row, 1.0, vb_t[hd * LANES:(hd + 1) * LANES, :]).astype(BF16)


def _in_call(xs, mod1, layer, n_lat, g_attn, w_in_p, g_cq, w_uq_p, g_ckv, w_uk_p, w_uvt, w_vbt,
             gains, tables):
    b, n, _ = xs.shape
    nt = n // ROW_TILE
    n_lat_tiles = n_lat // ROW_TILE
    t = ROW_TILE

    def lspec(arr):
        shp = arr.shape
        return pl.BlockSpec((1,) + shp[1:], lambda i, j: (layer,) + (0,) * (len(shp) - 1))

    tab_spec = pl.BlockSpec((t, LANES), lambda i, j: (j, 0))
    return pl.pallas_call(
        _in_kernel,
        grid=(b, nt),
        in_specs=[
            pl.BlockSpec((1, t, D_MODEL), lambda i, j: (i, j, 0)),
            pl.BlockSpec((1, 1, 2, D_MODEL), lambda i, j: (i, jnp.where(j >= n_lat_tiles, 1, 0), 0, 0)),
            lspec(g_attn), lspec(w_in_p), lspec(g_cq), lspec(w_uq_p), lspec(g_ckv),
            lspec(w_uk_p), lspec(w_uvt), lspec(w_vbt), lspec(gains),
        ] + [tab_spec] * 4,
        out_specs=[
            pl.BlockSpec((1, N_HEADS_ALL, t, LANES), lambda i, j: (i, 0, j, 0)),
            pl.BlockSpec((1, N_KHEADS_ALL, t, LANES), lambda i, j: (i, 0, j, 0)),
            pl.BlockSpec((1, N_KHEADS_ALL, 1, LANES, t), lambda i, j: (i, 0, j, 0, 0)),
            pl.BlockSpec((1, t, 2 * D_MODEL), lambda i, j: (i, j, 0)),
        ],
        out_shape=[
            jax.ShapeDtypeStruct((b, N_HEADS_ALL, n, LANES), BF16),
            jax.ShapeDtypeStruct((b, N_KHEADS_ALL, n, LANES), BF16),
            jax.ShapeDtypeStruct((b, N_KHEADS_ALL, nt, LANES, t), BF16),
            jax.ShapeDtypeStruct((b, n, 2 * D_MODEL), F32),
        ],
        compiler_params=pltpu.CompilerParams(
            dimension_semantics=("arbitrary", "arbitrary"), vmem_limit_bytes=VMEM_LIMIT),
        name="mixer_in",
    )(xs, mod1, g_attn, w_in_p, g_cq, w_uq_p, g_ckv, w_uk_p, w_uvt, w_vbt, gains, *tables)


def _attn_kernel(q_ref, k0_ref, k1_ref, v0_ref, v1_ref, *rest, n_main, tail_chunk, pipelined):
    if pipelined:
        o_ref, s_scr, acc_scr = rest[-3:]
    else:
        o_ref, acc_scr = rest[-2:]
    qs = (q_ref[0, 0], q_ref[0, 1])
    k_refs = (k0_ref, k1_ref)
    v_refs = (v0_ref, v1_ref)
    tq = qs[0].shape[0]
    per = KV_CHUNK // ROW_TILE

    def fold(hh, m, cmax, st, chunk0, n_sub, first=False):
        m_new = cmax if first else jnp.maximum(m, cmax)
        pt = jnp.exp2(st - m_new).astype(BF16)
        pv = _dot(v_refs[hh][0, 0, chunk0], pt[0:ROW_TILE])
        for u in range(1, n_sub):
            pv = pv + _dot(v_refs[hh][0, 0, chunk0 + u], pt[u * ROW_TILE:(u + 1) * ROW_TILE])
        if first:
            acc_scr[hh] = pv
        else:
            acc_scr[hh] = jnp.exp2(m - m_new) * acc_scr[hh] + pv
        return m_new

    def produce(c, slot):
        cmax = []
        for hh in range(2):
            off = c * KV_CHUNK
            if not isinstance(off, int):
                off = pl.multiple_of(off, KV_CHUNK)
            st = _dot_nt(k_refs[hh][0, 0, pl.ds(off, KV_CHUNK), :], qs[hh])
            s_scr[slot, hh] = st
            cmax.append(jnp.max(st, axis=0, keepdims=True))
        return tuple(cmax)

    def consume(c, slot, ms, cmax):
        return tuple(fold(hh, ms[hh], cmax[hh], s_scr[slot, hh], c * per, per) for hh in range(2))

    ms = []
    for hh in range(2):
        st = _dot_nt(k_refs[hh][0, 0, pl.ds(tail_chunk * ROW_TILE, ROW_TILE), :], qs[hh])
        ms.append(fold(hh, None, jnp.max(st, axis=0, keepdims=True), st, tail_chunk, 1, first=True))
    ms = tuple(ms)

    if pipelined:
        def body(i, carry):
            ms, cmax = carry
            cmax1 = produce(2 * i + 1, 1)
            ms = consume(2 * i, 0, ms, cmax)
            cmax2 = produce(2 * i + 2, 0)
            ms = consume(2 * i + 1, 1, ms, cmax1)
            return ms, cmax2

        ms, cmax = lax.fori_loop(0, (n_main - 2) // 2, body, (ms, produce(0, 0)))
        cmax1 = produce(n_main - 1, 1)
        ms = consume(n_main - 2, 0, ms, cmax)
        ms = consume(n_main - 1, 1, ms, cmax1)

    outs = []
    for hh in range(2):
        acc = acc_scr[hh]
        outs.append((acc / acc[ONES_ROW:ONES_ROW + 1, :]).T)
    low = lax.broadcasted_iota(jnp.int32, (tq, LANES), 1) < MLA_V
    o_ref[0] = jnp.where(low, outs[0], pltpu.roll(outs[1], MLA_V, 1)).astype(o_ref.dtype)


def _kv_head(p):
    mla_pairs = MLA_HEADS // 2
    pairs_per_kv = GQA_GROUP // 2
    gqa = MLA_HEADS + (p - mla_pairs) // pairs_per_kv
    return jnp.where(p < mla_pairs, 2 * p, gqa), jnp.where(p < mla_pairs, 2 * p + 1, gqa)


def _attn_lat_call(q_all, k_all, v_all, n_lat):
    b, _, n, _ = q_all.shape
    n_pairs = N_HEADS_ALL // 2
    n_chunks = n // ROW_TILE
    n_main = n_lat // KV_CHUNK
    assert n_main >= 2 and n_main % 2 == 0
    kern = functools.partial(_attn_kernel, n_main=n_main, tail_chunk=n_lat // ROW_TILE,
                             pipelined=True)
    k_spec = lambda which: pl.BlockSpec(
        (1, 1, n, LANES), lambda i, p, j: (i, _kv_head(p)[which], 0, 0))
    v_spec = lambda which: pl.BlockSpec(
        (1, 1, n_chunks, LANES, ROW_TILE), lambda i, p, j: (i, _kv_head(p)[which], 0, 0, 0))
    return pl.pallas_call(
        kern,
        grid=(b, n_pairs, n_lat // Q_TILE),
        in_specs=[pl.BlockSpec((1, 2, Q_TILE, LANES), lambda i, p, j: (i, p, j, 0)),
                  k_spec(0), k_spec(1), v_spec(0), v_spec(1)],
        out_specs=pl.BlockSpec((1, Q_TILE, LANES), lambda i, p, j: (i, j, p)),
        out_shape=jax.ShapeDtypeStruct((b, n_lat, n_pairs * LANES), BF16),
        scratch_shapes=[pltpu.VMEM((2, 2, KV_CHUNK, Q_TILE), F32),
                        pltpu.VMEM((2, LANES, Q_TILE), F32)],
        compiler_params=pltpu.CompilerParams(
            dimension_semantics=("arbitrary", "arbitrary", "arbitrary"),
            vmem_limit_bytes=VMEM_LIMIT),
        name="attention",
    )(q_all, k_all, k_all, v_all, v_all)


def _attn_ctx_call(q_all, k_all, v_all, n_lat):
    b, _, n, _ = q_all.shape
    n_pairs = N_HEADS_ALL // 2
    c = n_lat // ROW_TILE
    assert n - n_lat == ROW_TILE
    kern = functools.partial(_attn_kernel, n_main=0, tail_chunk=0, pipelined=False)
    k_spec = lambda which: pl.BlockSpec(
        (1, 1, ROW_TILE, LANES), lambda i, p: (i, _kv_head(p)[which], c, 0))
    v_spec = lambda which: pl.BlockSpec(
        (1, 1, 1, LANES, ROW_TILE), lambda i, p: (i, _kv_head(p)[which], c, 0, 0))
    return pl.pallas_call(
        kern,
        grid=(b, n_pairs),
        in_specs=[pl.BlockSpec((1, 2, ROW_TILE, LANES), lambda i, p: (i, p, c, 0)),
                  k_spec(0), k_spec(1), v_spec(0), v_spec(1)],
        out_specs=pl.BlockSpec((1, ROW_TILE, LANES), lambda i, p: (i, 0, p)),
        out_shape=jax.ShapeDtypeStruct((b, ROW_TILE, n_pairs * LANES), BF16),
        scratch_shapes=[pltpu.VMEM((2, LANES, ROW_TILE), F32)],
        compiler_params=pltpu.CompilerParams(
            dimension_semantics=("arbitrary", "arbitrary"), vmem_limit_bytes=VMEM_LIMIT),
        name="attention_ctx",
    )(q_all, k_all, k_all, v_all, v_all)


def _route_rows(sel, scores):
    rows = [sel[e:e + 1, :] for e in range(N_EXPERTS)]
    srow = [scores[e:e + 1, :] for e in range(N_EXPERTS)]
    grp = []
    for g in range(N_GROUPS):
        a, b, c, d = rows[4 * g:4 * g + 4]
        hi1, lo1 = jnp.maximum(a, b), jnp.minimum(a, b)
        hi2, lo2 = jnp.maximum(c, d), jnp.minimum(c, d)
        top1 = jnp.maximum(hi1, hi2)
        top2 = jnp.maximum(jnp.minimum(hi1, hi2), jnp.maximum(lo1, lo2))
        grp.append(top1 + top2)
    best = jnp.zeros_like(grp[0], dtype=jnp.int32)
    best_v = grp[0]
    for g in range(1, N_GROUPS):
        upd = grp[g] > best_v
        best = jnp.where(upd, g, best)
        best_v = jnp.where(upd, grp[g], best_v)
    picked = []
    keeps = []
    for e in range(N_EXPERTS):
        g = e // EXPERTS_PER_GROUP
        rank = jnp.zeros_like(best)
        for o in range(4 * g, 4 * g + 4):
            if o == e:
                continue
            ahead = rows[o] > rows[e]
            if o < e:
                ahead = ahead | (rows[o] == rows[e])
            rank = rank + ahead.astype(jnp.int32)
        keep = (rank < 2) & (best == g)
        picked.append(jnp.where(keep, srow[e], 0.0))
        keeps.append(keep)
    total = picked[0]
    for e in range(1, N_EXPERTS):
        total = total + picked[e]
    return [p / total for p in picked], best, keeps


def _route_meta(gates, best, keeps):
    mask = jnp.zeros_like(best)
    w_lo = jnp.zeros_like(gates[0])
    w_hi = jnp.zeros_like(gates[0])
    for e in range(N_EXPERTS):
        g, i = divmod(e, EXPERTS_PER_GROUP)
        mask = mask + jnp.where(keeps[e], 1 << i, 0)
        if i == 0:
            w_lo = w_lo + jnp.where(keeps[e], gates[e], 0.0)
            continue
        below = keeps[4 * g]
        for o in range(4 * g + 1, e):
            below = below | keeps[o]
        w_lo = w_lo + jnp.where(keeps[e] & ~below, gates[e], 0.0)
        w_hi = w_hi + jnp.where(keeps[e] & below, gates[e], 0.0)
    cls = (best * (1 << EXPERTS_PER_GROUP) + mask).astype(F32)
    return cls, w_lo, w_hi


def _mid_kernel(o_ref, octx_ref, g_ref, x_ref, mod_ref, woa_ref, wob_ref, wout_ref, gffn_ref,
                wrh_ref, wrl_ref, brc_ref, xn_ref, fext_ref, cnt_ref, run_ref, *, n_lat_tiles):
    o = jnp.where(pl.program_id(1) >= n_lat_tiles, octx_ref[0], o_ref[0])
    n_a = MLA_HEADS * MLA_V
    ya = _dot(o[:, 0:n_a], woa_ref[0])
    yb = _dot(o[:, n_a:], wob_ref[0])
    g = g_ref[0]
    y = jax.nn.sigmoid(g[:, 0:D_MODEL]) * ya + jax.nn.sigmoid(g[:, D_MODEL:]) * yb
    z = _dot(y.astype(BF16), wout_ref[0])
    gt1 = mod_ref[0, 0, 0:1, :]
    sh2 = mod_ref[0, 0, 1:2, :]
    sc2 = mod_ref[0, 0, 2:3, :]
    xn = x_ref[0] + gt1 * z
    xn_ref[0] = xn
    f = _rms_rows(xn, gffn_ref[0]) * (1 + sc2) + sh2
    fext_ref[0, :, 0:D_MODEL] = f
    f_hi, f_lo = _split_bf16(f)
    w_hi = wrh_ref[...]
    logits = _dot(f_hi, w_hi) + _dot(f_lo, w_hi) + _dot(f_hi, wrl_ref[...])
    scores_t = jax.nn.sigmoid(logits).T
    scores = scores_t[0:N_EXPERTS, :]
    sel = scores + brc_ref[0:N_EXPERTS, :]
    cls, w_lo, w_hi_gate = _route_meta(*_route_rows(sel, scores))
    t = scores.shape[1]
    row_id = lax.broadcasted_iota(jnp.int32, (LANES, t), 0)
    meta_t = jnp.where(row_id == META_CLS, cls, 0.0)
    meta_t = jnp.where(row_id == META_W_LO, w_lo, meta_t)
    meta_t = jnp.where(row_id == META_W_HI, w_hi_gate, meta_t)
    meta = meta_t.T

    @pl.when((pl.program_id(0) == 0) & (pl.program_id(1) == 0))
    def _():
        run_ref[...] = jnp.zeros_like(run_ref)

    lane = lax.broadcasted_iota(jnp.int32, (t, LANES), 1)
    onehot = jnp.where(lane.astype(F32) == meta[:, META_CLS:META_CLS + 1], 1.0, 0.0)
    earlier = jnp.where(lax.broadcasted_iota(jnp.int32, (t, t), 0)
                        > lax.broadcasted_iota(jnp.int32, (t, t), 1), 1.0, 0.0).astype(BF16)
    before = _dot(earlier, onehot.astype(BF16)) + run_ref[...]
    rank = jnp.sum(before * onehot, axis=-1, keepdims=True)
    fext_ref[0, :, D_MODEL:] = jnp.where(lane == META_RANK, rank, meta)
    run_ref[...] += jnp.sum(onehot, axis=0, keepdims=True)
    cnt_ref[...] = jnp.broadcast_to(run_ref[...], cnt_ref.shape)


def _mid_call(o_lat, o_ctx, gts, xs, modm, layer, n_rows, n_lat, w_oa, w_ob, w_out, g_ffn, wr_hi,
              wr_lo, br_col):
    b = xs.shape[0]
    n = n_rows
    n_lat_tiles = n_lat // ROW_TILE
    t = ROW_TILE

    def lspec(arr):
        shp = arr.shape
        return pl.BlockSpec((1,) + shp[1:], lambda i, j: (layer,) + (0,) * (len(shp) - 1))

    def full(arr):
        return pl.BlockSpec(arr.shape, lambda i, j: (0,) * arr.ndim)

    return pl.pallas_call(
        functools.partial(_mid_kernel, n_lat_tiles=n_lat_tiles),
        grid=(b, n // t),
        in_specs=[
            pl.BlockSpec((1, t, o_lat.shape[2]), lambda i, j: (i, jnp.minimum(j, n_lat_tiles - 1), 0)),
            pl.BlockSpec((1, t, o_ctx.shape[2]), lambda i, j: (i, 0, 0)),
            pl.BlockSpec((1, t, 2 * D_MODEL), lambda i, j: (i, j, 0)),
            pl.BlockSpec((1, t, D_MODEL), lambda i, j: (i, j, 0)),
            pl.BlockSpec((1, 1, 3, D_MODEL), lambda i, j: (i, jnp.where(j >= n_lat_tiles, 1, 0), 0, 0)),
            lspec(w_oa), lspec(w_ob), lspec(w_out), lspec(g_ffn),
            full(wr_hi), full(wr_lo), full(br_col),
        ],
        out_specs=[
            pl.BlockSpec((1, t, D_MODEL), lambda i, j: (i, j, 0)),
            pl.BlockSpec((1, t, F_EXT), lambda i, j: (i, j, 0)),
            pl.BlockSpec((8, LANES), lambda i, j: (0, 0)),
        ],
        out_shape=[
            jax.ShapeDtypeStruct((b, n, D_MODEL), F32),
            jax.ShapeDtypeStruct((b, n, F_EXT), F32),
            jax.ShapeDtypeStruct((8, LANES), F32),
        ],
        scratch_shapes=[pltpu.VMEM((1, LANES), F32)],
        compiler_params=pltpu.CompilerParams(
            dimension_semantics=("arbitrary", "arbitrary"), vmem_limit_bytes=VMEM_LIMIT),
        name="mixer_out_router",
    )(o_lat, o_ctx, gts, xs, modm, w_oa, w_ob, w_out, g_ffn, wr_hi, wr_lo, br_col)


def _class_expert_tables():
    lo = np.zeros((N_CLASSES,), np.int32)
    hi = np.zeros((N_CLASSES,), np.int32)
    for c in range(N_CLASSES):
        g, mask = divmod(c, 1 << EXPERTS_PER_GROUP)
        bits = [i for i in range(EXPERTS_PER_GROUP) if mask >> i & 1]
        if len(bits) == 2:
            lo[c], hi[c] = EXPERTS_PER_GROUP * g + bits[0], EXPERTS_PER_GROUP * g + bits[1]
    return lo, hi


def _route_plan(meta, counts, n_tiles_max):
    cls = meta[:, META_CLS].astype(jnp.int32)
    rank = meta[:, META_RANK].astype(jnp.int32)
    cnt = counts[0, :N_CLASSES].astype(jnp.int32)
    tiles = (cnt + MOE_TILE - 1) // MOE_TILE
    tile_end = jnp.cumsum(tiles)
    start = (tile_end - tiles) * MOE_TILE
    hit = cls[None, :] == jnp.arange(N_CLASSES, dtype=jnp.int32)[:, None]
    pos = jnp.sum(jnp.where(hit, start[:, None], 0), axis=0) + rank
    n_used = tile_end[-1]
    tile_idx = jnp.minimum(jnp.arange(n_tiles_max, dtype=jnp.int32), n_used - 1)
    tile_cls = jnp.sum(tile_end[None, :] <= tile_idx[:, None], axis=1).astype(jnp.int32)
    lo, hi = _class_expert_tables()
    src = jnp.zeros((n_tiles_max * MOE_TILE,), jnp.int32).at[pos].set(
        jnp.arange(pos.shape[0], dtype=jnp.int32))
    return (pos.astype(jnp.int32), src, tile_idx, jnp.asarray(lo)[tile_cls],
            jnp.asarray(hi)[tile_cls], n_used.reshape(1).astype(jnp.int32))


def _row_copy(src_ref, src_row, dst_ref, dst_row, sem):
    return pltpu.make_async_copy(src_ref.at[pl.ds(src_row, 1)], dst_ref.at[pl.ds(dst_row, 1)], sem)


def _moe_kernel(src_ref, tidx_ref, elo_ref, ehi_ref, nused_ref, f_hbm, wgs_ref, wus_ref, wds_ref,
                wg1_ref, wu1_ref, wd1_ref, wg2_ref, wu2_ref, wd2_ref, ys_ref, buf, sems):
    del elo_ref, ehi_ref
    j = pl.program_id(0)
    slot = j % 2

    def fetch(step, into):
        base = tidx_ref[step] * MOE_TILE
        for r in range(MOE_TILE):
            _row_copy(f_hbm, src_ref[base + r], buf.at[into], r, sems.at[into]).start()

    pl.when(j == 0)(lambda: fetch(0, 0))
    pl.when(j + 1 < pl.num_programs(0))(lambda: fetch(j + 1, 1 - slot))
    for _ in range(MOE_TILE):
        _row_copy(f_hbm, 0, buf.at[slot], 0, sems.at[slot]).wait()
    used = j < nused_ref[0]

    @pl.when(jnp.logical_not(used))
    def _():
        ys_ref[...] = jnp.zeros_like(ys_ref)

    @pl.when(used)
    def _():
        rows = buf[slot]
        x = rows[:, 0:D_MODEL].astype(BF16)
        w_lo = rows[:, D_MODEL + META_W_LO:D_MODEL + META_W_LO + 1]
        w_hi = rows[:, D_MODEL + META_W_HI:D_MODEL + META_W_HI + 1]

        def ffn(wg_ref, wu_ref, wd_ref):
            a = jax.nn.silu(_dot(x, wg_ref[0, 0])) * _dot(x, wu_ref[0, 0])
            return _dot(a.astype(BF16), wd_ref[0, 0])

        ys_ref[...] = (ffn(wgs_ref, wus_ref, wds_ref) + w_lo * ffn(wg1_ref, wu1_ref, wd1_ref)
                       + w_hi * ffn(wg2_ref, wu2_ref, wd2_ref))


def _moe_call(plan, fext, layer, wg_all, wu_all, wd_all):
    _, src, tile_idx, e_lo, e_hi, n_used = plan
    n_tiles = tile_idx.shape[0]

    def up_spec(which):
        return pl.BlockSpec((1, 1, D_MODEL, FF_EXPERT),
                            lambda j, s, t, lo, hi, nu: (layer, which(j, lo, hi), 0, 0))

    def down_spec(which):
        return pl.BlockSpec((1, 1, FF_EXPERT, D_MODEL),
                            lambda j, s, t, lo, hi, nu: (layer, which(j, lo, hi), 0, 0))

    shared = lambda j, lo, hi: N_EXPERTS
    first = lambda j, lo, hi: lo[j]
    second = lambda j, lo, hi: hi[j]
    return pl.pallas_call(
        _moe_kernel,
        grid_spec=pltpu.PrefetchScalarGridSpec(
            num_scalar_prefetch=5,
            grid=(n_tiles,),
            in_specs=[pl.BlockSpec(memory_space=pl.ANY),
                      up_spec(shared), up_spec(shared), down_spec(shared),
                      up_spec(first), up_spec(first), down_spec(first),
                      up_spec(second), up_spec(second), down_spec(second)],
            out_specs=pl.BlockSpec((MOE_TILE, D_MODEL), lambda j, s, t, lo, hi, nu: (j, 0)),
            scratch_shapes=[pltpu.VMEM((2, MOE_TILE, F_EXT), F32),
                            pltpu.SemaphoreType.DMA((2,))],
        ),
        out_shape=jax.ShapeDtypeStruct((n_tiles * MOE_TILE, D_MODEL), F32),
        compiler_params=pltpu.CompilerParams(
            dimension_semantics=("arbitrary",), vmem_limit_bytes=VMEM_LIMIT),
        name="moe",
    )(src, tile_idx, e_lo, e_hi, n_used, fext, wg_all, wu_all, wd_all, wg_all, wu_all, wd_all,
      wg_all, wu_all, wd_all)


def _gather_kernel(pos_ref, xn_ref, gt_ref, ys_hbm, o_ref, buf, sems):
    nt = pl.num_programs(1)
    step = pl.program_id(0) * nt + pl.program_id(1)
    total = pl.num_programs(0) * nt
    slot = step % 2

    def fetch(s, into):
        for r in range(MOE_TILE):
            _row_copy(ys_hbm, pos_ref[s * MOE_TILE + r], buf.at[into], r, sems.at[into]).start()

    pl.when(step == 0)(lambda: fetch(0, 0))
    pl.when(step + 1 < total)(lambda: fetch(step + 1, 1 - slot))
    for _ in range(MOE_TILE):
        _row_copy(ys_hbm, 0, buf.at[slot], 0, sems.at[slot]).wait()
    o_ref[0] = xn_ref[0] + gt_ref[0, 0] * buf[slot]


def _gather_call(pos, xn, gt2, ys, n_lat):
    b, n, _ = xn.shape
    n_lat_tiles = n_lat // MOE_TILE
    return pl.pallas_call(
        _gather_kernel,
        grid_spec=pltpu.PrefetchScalarGridSpec(
            num_scalar_prefetch=1,
            grid=(b, n // MOE_TILE),
            in_specs=[
                pl.BlockSpec((1, MOE_TILE, D_MODEL), lambda i, j, p: (i, j, 0)),
                pl.BlockSpec((1, 1, 1, D_MODEL),
                             lambda i, j, p: (i, jnp.where(j >= n_lat_tiles, 1, 0), 0, 0)),
                pl.BlockSpec(memory_space=pl.ANY),
            ],
            out_specs=pl.BlockSpec((1, MOE_TILE, D_MODEL), lambda i, j, p: (i, j, 0)),
            scratch_shapes=[pltpu.VMEM((2, MOE_TILE, D_MODEL), F32),
                            pltpu.SemaphoreType.DMA((2,))],
        ),
        out_shape=jax.ShapeDtypeStruct((b, n, D_MODEL), F32),
        compiler_params=pltpu.CompilerParams(
            dimension_semantics=("arbitrary", "arbitrary"), vmem_limit_bytes=VMEM_LIMIT),
        name="moe_gather_residual",
    )(pos, xn, gt2[:, :, None, :], ys)


def _pad_heads(w, n_heads, dim):
    lead = w.shape[:-1]
    w = w.reshape(lead + (n_heads, dim))
    w = jnp.pad(w, [(0, 0)] * len(lead) + [(0, 0), (0, LANES - dim)])
    return w.reshape(lead + (n_heads * LANES,))


def _partner_lanes(lane0, half_dim):
    perm = np.arange(LANES)
    quarter = half_dim // 2
    for part in range(2):
        lo = lane0 + part * half_dim
        perm[lo:lo + quarter] = np.arange(lo + quarter, lo + half_dim)
        perm[lo + quarter:lo + half_dim] = np.arange(lo, lo + quarter)
    return perm


PARTNER_MLA = _partner_lanes(MLA_NOPE, MLA_ROPE // 2)
PARTNER_GQA = _partner_lanes(0, GQA_HD // 2)


def _swap_partners(w, perm):
    lead = w.shape[:-1]
    return w.reshape(lead + (-1, LANES))[..., perm].reshape(w.shape)


def _prep_w_in(w_in):
    offs = np.concatenate([[0], np.cumsum(IN_SIZES)])
    cq, ckv, kr, qb, kb, vb, gts = [w_in[..., int(offs[i]):int(offs[i + 1])] for i in range(7)]
    kr_blk = jnp.pad(kr, [(0, 0), (0, 0), (MLA_NOPE, LANES - MLA_QK)])
    qb_p, kb_p = _pad_heads(qb, GQA_HEADS, GQA_HD), _pad_heads(kb, GQA_KV_HEADS, GQA_HD)
    w_main = jnp.concatenate(
        [cq, ckv, kr_blk, _swap_partners(kr_blk, PARTNER_MLA), qb_p, _swap_partners(qb_p, PARTNER_GQA),
         kb_p, _swap_partners(kb_p, PARTNER_GQA), gts], axis=-1).astype(BF16)
    w_vbt = jnp.swapaxes(_pad_heads(vb, GQA_KV_HEADS, GQA_HD), 1, 2).astype(BF16)
    return w_main, w_vbt


def _prep_w_ukv(w_ukv):
    l, r, _ = w_ukv.shape
    w = w_ukv.reshape(l, r, MLA_HEADS, MLA_NOPE + MLA_V)
    uk = _pad_heads(w[..., :MLA_NOPE].reshape(l, r, MLA_HEADS * MLA_NOPE), MLA_HEADS, MLA_NOPE)
    uv = _pad_heads(w[..., MLA_NOPE:].reshape(l, r, MLA_HEADS * MLA_V), MLA_HEADS, MLA_V)
    return uk.astype(BF16), jnp.swapaxes(uv, 1, 2).astype(BF16)


def _pad_gain(g, dim):
    return jnp.pad(g, [(0, 0), (0, LANES - dim)])


def _gain_rows(g_qa, g_ka, g_qb, g_kb):
    rows = []
    for g, dim, perm in ((g_qa, MLA_QK, PARTNER_MLA), (g_ka, MLA_QK, PARTNER_MLA),
                         (g_qb, GQA_HD, PARTNER_GQA), (g_kb, GQA_HD, PARTNER_GQA)):
        g_p = _pad_gain(g, dim)
        rows += [g_p, g_p[:, perm]]
    return jnp.stack(rows, axis=1)


def _rope_tables(n_ctx, n_lat, lane0, half_dim):
    pos = jnp.arange(n_lat)
    inv = ROPE_THETA ** (-jnp.arange(0, half_dim, 2, dtype=F32) / half_dim)
    ang_r = (pos // GRID_W).astype(F32)[:, None] * inv[None, :]
    ang_c = (pos % GRID_W).astype(F32)[:, None] * inv[None, :]
    cos = jnp.concatenate([jnp.cos(ang_r)] * 2 + [jnp.cos(ang_c)] * 2, axis=-1)
    sin = jnp.concatenate([-jnp.sin(ang_r), jnp.sin(ang_r), -jnp.sin(ang_c), jnp.sin(ang_c)], axis=-1)
    pad = [(0, n_ctx), (lane0, LANES - lane0 - 2 * half_dim)]
    return (jnp.pad(cos, pad, constant_values=1.0), jnp.pad(sin, pad))


def kernel(x, c, ctx, c_ctx, w_mod, b_mod, g_attn, g_ffn, w_in, g_cq, w_uq, g_ckv, w_ukv,
           g_qa, g_ka, g_qb, g_kb, w_oa, w_ob, w_out, w_router, b_router,
           w_e_gate, w_e_up, w_e_down, w_s_gate, w_s_up, w_s_down):
    b, n_lat, d = x.shape
    n_ctx = ctx.shape[1]
    n = n_lat + n_ctx
    depth = w_mod.shape[0]
    assert d == D_MODEL and n_ctx == ROW_TILE and n_lat % KV_CHUNK == 0 and n_lat % Q_TILE == 0

    xs = jnp.concatenate([x, ctx], axis=1)
    cs = jnp.concatenate([c, c_ctx[None, :], jnp.zeros((8 - b - 1, d), F32)], axis=0)

    w_in_p, w_vbt = _prep_w_in(w_in)
    w_uq_p = _pad_heads(w_uq, MLA_HEADS, MLA_QK)
    w_uq_p = jnp.concatenate([w_uq_p, _swap_partners(w_uq_p, PARTNER_MLA)], axis=-1).astype(BF16)
    w_uk_p, w_uvt = _prep_w_ukv(w_ukv)
    g3 = lambda g: g[:, None, :]
    gains = _gain_rows(g_qa, g_ka, g_qb, g_kb)
    tables = (_rope_tables(n_ctx, n_lat, MLA_NOPE, MLA_ROPE // 2)
              + _rope_tables(n_ctx, n_lat, 0, GQA_HD // 2))
    w_oa_b, w_ob_b, w_out_b = w_oa.astype(BF16), w_ob.astype(BF16), w_out.astype(BF16)
    wr_p = jnp.pad(w_router, [(0, 0), (0, LANES - N_EXPERTS)])
    wr_hi = wr_p.astype(BF16)
    wr_lo = (wr_p - wr_hi.astype(F32)).astype(BF16)
    br_col = jnp.pad(b_router, (0, LANES - N_EXPERTS))[:, None]
    wg_all = jnp.concatenate([w_e_gate, w_s_gate[:, None]], axis=1).astype(BF16)
    wu_all = jnp.concatenate([w_e_up, w_s_up[:, None]], axis=1).astype(BF16)
    wd_all = jnp.concatenate([w_e_down, w_s_down[:, None]], axis=1).astype(BF16)
    b_mod3 = b_mod[:, None, :]

    for layer in range(depth):
        last = layer == depth - 1
        n_rows = n_lat if last else n
        mods = _mod_call(cs, w_mod, b_mod3, layer).reshape(8, 6, d)
        lat, cx = mods[:b], jnp.broadcast_to(mods[b:b + 1], (b, 6, d))
        both = jnp.stack([lat, cx], axis=1)
        mod1, modm, gt2 = both[:, :, 0:2], both[:, :, 2:5], both[:, :, 5]
        q_all, k_all, v_all, gts = _in_call(
            xs, mod1, layer, n_lat, g3(g_attn), w_in_p, g3(g_cq), w_uq_p, g3(g_ckv), w_uk_p, w_uvt,
            w_vbt, gains, tables)
        o_lat = _attn_lat_call(q_all, k_all, v_all, n_lat)
        o_ctx = o_lat if last else _attn_ctx_call(q_all, k_all, v_all, n_lat)
        xn, fext, counts = _mid_call(o_lat, o_ctx, gts, xs, modm, layer, n_rows, n_lat, w_oa_b,
                                     w_ob_b, w_out_b, g3(g_ffn), wr_hi, wr_lo, br_col)
        n_tok = b * n_rows
        n_tiles_max = n_tok // MOE_TILE + N_PAIR_CLASSES
        fext = fext.reshape(n_tok, F_EXT)
        plan = _route_plan(fext[:, D_MODEL:D_MODEL + 4], counts, n_tiles_max)
        ys = _moe_call(plan, fext, layer, wg_all, wu_all, wd_all)
        xs = _gather_call(plan[0], xn, gt2, ys, n_lat)
    return xs
```

```python
import functools

import numpy as np
import jax
import jax.numpy as jnp
from jax import lax
from jax.experimental import pallas as pl
from jax.experimental.pallas import tpu as pltpu

D_MODEL = 1024
GRID_W = 64
MLA_HEADS = 8
MLA_NOPE = 64
MLA_ROPE = 32
MLA_QK = MLA_NOPE + MLA_ROPE
MLA_V = 64
Q_LORA = 256
KV_LORA = 128
GQA_HEADS = 8
GQA_KV_HEADS = 2
GQA_GROUP = GQA_HEADS // GQA_KV_HEADS
GQA_HD = 64
N_EXPERTS = 16
N_GROUPS = 4
EXPERTS_PER_GROUP = N_EXPERTS // N_GROUPS
FF_EXPERT = 512
ROPE_THETA = 10000.0
EPS = 1e-6
IN_SIZES = (Q_LORA, KV_LORA, MLA_ROPE, GQA_HEADS * GQA_HD, GQA_KV_HEADS * GQA_HD,
            GQA_KV_HEADS * GQA_HD, 2 * D_MODEL)

LANES = 128
ROW_TILE = 256
Q_TILE = 1024
KV_CHUNK = 512
N_HEADS_ALL = MLA_HEADS + GQA_HEADS
N_KHEADS_ALL = MLA_HEADS + GQA_KV_HEADS
LOG2_E = 1.4426950408889634
ONES_ROW = MLA_V
V_ROWS = MLA_V + 16
VMEM_LIMIT = 56 * 1024 * 1024
MOE_TILE = 256
N_CLASSES = N_GROUPS << EXPERTS_PER_GROUP
N_PAIR_CLASSES = N_GROUPS * 6
F_EXT = D_MODEL + LANES
META_CLS, META_RANK, META_W_LO, META_W_HI = 0, 1, 2, 3

C_CQ = 0
C_CKV = C_CQ + Q_LORA
C_KR = C_CKV + KV_LORA
C_QB = C_KR + 2 * LANES
C_KB = C_QB + 2 * GQA_HEADS * LANES
C_G = C_KB + 2 * GQA_KV_HEADS * LANES
C_END = C_G + 2 * D_MODEL

BF16 = jnp.bfloat16
F32 = jnp.float32


def _dot(a, b):
    return jnp.dot(a, b, preferred_element_type=F32)


def _dot_nt(a, b):
    return lax.dot_general(a, b, (((1,), (1,)), ((), ())), preferred_element_type=F32)


def _split_bf16(x):
    hi = x.astype(BF16)
    lo = (x - hi.astype(F32)).astype(BF16)
    return hi, lo


def _rms_rows(x, g):
    ms = jnp.mean(x * x, axis=-1, keepdims=True)
    return x * lax.rsqrt(ms + EPS) * g


def _norm_rope(raw, swapped, gain_cos, gain_sin, dim):
    ms = jnp.sum(raw * raw, axis=-1, keepdims=True) * (1.0 / dim)
    return (raw * gain_cos + swapped * gain_sin) * lax.rsqrt(ms + EPS)


def _mod_kernel(c_ref, w_ref, b_ref, o_ref):
    c = c_ref[...]
    s_hi, s_lo = _split_bf16(c * jax.nn.sigmoid(c))
    w_hi, w_lo = _split_bf16(w_ref[0])
    o_ref[...] = _dot(s_hi, w_hi) + _dot(s_lo, w_hi) + _dot(s_hi, w_lo) + b_ref[0]


def _mod_call(cs, w_mod, b_mod, layer):
    n_cols = w_mod.shape[2]
    tn = 1536
    return pl.pallas_call(
        _mod_kernel,
        grid=(n_cols // tn,),
        in_specs=[
            pl.BlockSpec(cs.shape, lambda j: (0, 0)),
            pl.BlockSpec((1, D_MODEL, tn), lambda j: (layer, 0, j)),
            pl.BlockSpec((1, 1, tn), lambda j: (layer, 0, j)),
        ],
        out_specs=pl.BlockSpec((cs.shape[0], tn), lambda j: (0, j)),
        out_shape=jax.ShapeDtypeStruct((cs.shape[0], n_cols), F32),
        compiler_params=pltpu.CompilerParams(
            dimension_semantics=("arbitrary",), vmem_limit_bytes=VMEM_LIMIT),
        name="mod",
    )(cs, w_mod, b_mod)


def _in_kernel(x_ref, mod_ref, gattn_ref, win_ref, gcq_ref, wuq_ref, gckv_ref, wuk_ref, wuvt_ref,
               wvbt_ref, gains_ref, cosa_ref, sina_ref, cosb_ref, sinb_ref,
               q_ref, k_ref, v_ref, g_ref):
    x = x_ref[0]
    shift = mod_ref[0, 0, 0:1, :]
    scale = mod_ref[0, 0, 1:2, :]
    h = (_rms_rows(x, gattn_ref[0]) * (1 + scale) + shift).astype(BF16)

    g_ref[0] = _dot(h, win_ref[0, :, C_G:C_END])

    gains = gains_ref[0]
    cos_a, sin_a, cos_b, sin_b = cosa_ref[...], sina_ref[...], cosb_ref[...], sinb_ref[...]
    scale_a = MLA_QK ** -0.5 * LOG2_E
    scale_b = GQA_HD ** -0.5 * LOG2_E
    qa_cos, qa_sin = cos_a * (gains[0:1] * scale_a), sin_a * (gains[1:2] * scale_a)
    ka_cos, ka_sin = cos_a * gains[2:3], sin_a * gains[3:4]
    qb_cos, qb_sin = cos_b * (gains[4:5] * scale_b), sin_b * (gains[5:6] * scale_b)
    kb_cos, kb_sin = cos_b * gains[6:7], sin_b * gains[7:8]
    t = x.shape[0]
    ones_row = lax.broadcasted_iota(jnp.int32, (V_ROWS, t), 0) == ONES_ROW
    blk = lambda a, i: a[:, i * LANES:(i + 1) * LANES]

    p0 = _dot(h, win_ref[0, :, C_CQ:C_QB])
    cq = p0[:, C_CQ:C_CKV]
    ckv = p0[:, C_CKV:C_KR]
    kr_blk = p0[:, C_KR:C_KR + LANES]
    kr_swapped = p0[:, C_KR + LANES:C_QB]
    qa = _dot(_rms_rows(cq, gcq_ref[0]).astype(BF16), wuq_ref[0])
    ckv_n = _rms_rows(ckv, gckv_ref[0]).astype(BF16)
    ka = _dot(ckv_n, wuk_ref[0])
    va_t = _dot_nt(wuvt_ref[0], ckv_n)
    for hd in range(MLA_HEADS):
        q_ref[0, hd] = _norm_rope(blk(qa, hd), blk(qa, MLA_HEADS + hd), qa_cos, qa_sin,
                                  MLA_QK).astype(BF16)
        k_ref[0, hd] = _norm_rope(blk(ka, hd) + kr_blk, kr_swapped, ka_cos, ka_sin,
                                  MLA_QK).astype(BF16)
        v_ref[0, hd, 0] = jnp.where(ones_row, 1.0, va_t[hd * V_ROWS:(hd + 1) * V_ROWS, :]).astype(BF16)

    pq = _dot(h, win_ref[0, :, C_QB:C_KB])
    for hd in range(GQA_HEADS):
        q_ref[0, MLA_HEADS + hd] = _norm_rope(blk(pq, hd), blk(pq, GQA_HEADS + hd), qb_cos, qb_sin,
                                              GQA_HD).astype(BF16)
    pk = _dot(h, win_ref[0, :, C_KB:C_G])
    vb_t = _dot_nt(wvbt_ref[0], h)
    for hd in range(GQA_KV_HEADS):
        k_ref[0, MLA_HEADS + hd] = _norm_rope(blk(pk, hd), blk(pk, GQA_KV_HEADS + hd), kb_cos,
                                              kb_sin, GQA_HD).astype(BF16)
        v_ref[0, MLA_HEADS + hd, 0] = jnp.where(
            ones_row, 1.0, vb_t[hd * V_ROWS:(hd + 1) * V_ROWS, :]).astype(BF16)


def _in_call(xs, mod1, layer, n_lat, g_attn, w_in_p, g_cq, w_uq_p, g_ckv, w_uk_p, w_uvt, w_vbt,
             gains, tables):
    b, n, _ = xs.shape
    nt = n // ROW_TILE
    n_lat_tiles = n_lat // ROW_TILE
    t = ROW_TILE

    def lspec(arr):
        shp = arr.shape
        return pl.BlockSpec((1,) + shp[1:], lambda i, j: (layer,) + (0,) * (len(shp) - 1))

    tab_spec = pl.BlockSpec((t, LANES), lambda i, j: (j, 0))
    return pl.pallas_call(
        _in_kernel,
        grid=(b, nt),
        in_specs=[
            pl.BlockSpec((1, t, D_MODEL), lambda i, j: (i, j, 0)),
            pl.BlockSpec((1, 1, 2, D_MODEL), lambda i, j: (i, jnp.where(j >= n_lat_tiles, 1, 0), 0, 0)),
            lspec(g_attn), lspec(w_in_p), lspec(g_cq), lspec(w_uq_p), lspec(g_ckv),
            lspec(w_uk_p), lspec(w_uvt), lspec(w_vbt), lspec(gains),
        ] + [tab_spec] * 4,
        out_specs=[
            pl.BlockSpec((1, N_HEADS_ALL, t, LANES), lambda i, j: (i, 0, j, 0)),
            pl.BlockSpec((1, N_KHEADS_ALL, t, LANES), lambda i, j: (i, 0, j, 0)),
            pl.BlockSpec((1, N_KHEADS_ALL, 1, V_ROWS, t), lambda i, j: (i, 0, j, 0, 0)),
            pl.BlockSpec((1, t, 2 * D_MODEL), lambda i, j: (i, j, 0)),
        ],
        out_shape=[
            jax.ShapeDtypeStruct((b, N_HEADS_ALL, n, LANES), BF16),
            jax.ShapeDtypeStruct((b, N_KHEADS_ALL, n, LANES), BF16),
            jax.ShapeDtypeStruct((b, N_KHEADS_ALL, nt, V_ROWS, t), BF16),
            jax.ShapeDtypeStruct((b, n, 2 * D_MODEL), F32),
        ],
        compiler_params=pltpu.CompilerParams(
            dimension_semantics=("arbitrary", "arbitrary"), vmem_limit_bytes=VMEM_LIMIT),
        name="mixer_in",
    )(xs, mod1, g_attn, w_in_p, g_cq, w_uq_p, g_ckv, w_uk_p, w_uvt, w_vbt, gains, *tables)


def _attn_kernel(q_ref, k0_ref, k1_ref, v0_ref, v1_ref, *rest, n_main, tail_chunk, pipelined):
    if pipelined:
        o_ref, s_scr, acc_scr = rest[-3:]
    else:
        o_ref, acc_scr = rest[-2:]
    qs = (q_ref[0, 0], q_ref[0, 1])
    k_refs = (k0_ref, k1_ref)
    v_refs = (v0_ref, v1_ref)
    tq = qs[0].shape[0]
    per = KV_CHUNK // ROW_TILE

    def fold(hh, m, cmax, st, chunk0, n_sub, first=False):
        m_new = cmax if first else jnp.maximum(m, cmax)
        pt = jnp.exp2(st - m_new).astype(BF16)
        pv = _dot(v_refs[hh][0, 0, chunk0], pt[0:ROW_TILE])
        for u in range(1, n_sub):
            pv = pv + _dot(v_refs[hh][0, 0, chunk0 + u], pt[u * ROW_TILE:(u + 1) * ROW_TILE])
        if first:
            acc_scr[hh] = pv
        else:
            acc_scr[hh] = jnp.exp2(m - m_new) * acc_scr[hh] + pv
        return m_new

    def produce(c, slot):
        cmax = []
        for hh in range(2):
            off = c * KV_CHUNK
            if not isinstance(off, int):
                off = pl.multiple_of(off, KV_CHUNK)
            st = _dot_nt(k_refs[hh][0, 0, pl.ds(off, KV_CHUNK), :], qs[hh])
            s_scr[slot, hh] = st
            cmax.append(jnp.max(st, axis=0, keepdims=True))
        return tuple(cmax)

    def consume(c, slot, ms, cmax):
        return tuple(fold(hh, ms[hh], cmax[hh], s_scr[slot, hh], c * per, per) for hh in range(2))

    ms = []
    for hh in range(2):
        st = _dot_nt(k_refs[hh][0, 0, pl.ds(tail_chunk * ROW_TILE, ROW_TILE), :], qs[hh])
        ms.append(fold(hh, None, jnp.max(st, axis=0, keepdims=True), st, tail_chunk, 1, first=True))
    ms = tuple(ms)

    if pipelined:
        def body(i, carry):
            ms, cmax = carry
            cmax1 = produce(2 * i + 1, 1)
            ms = consume(2 * i, 0, ms, cmax)
            cmax2 = produce(2 * i + 2, 0)
            ms = consume(2 * i + 1, 1, ms, cmax1)
            return ms, cmax2

        ms, cmax = lax.fori_loop(0, (n_main - 2) // 2, body, (ms, produce(0, 0)))
        cmax1 = produce(n_main - 1, 1)
        ms = consume(n_main - 2, 0, ms, cmax)
        ms = consume(n_main - 1, 1, ms, cmax1)

    outs = []
    for hh in range(2):
        acc = acc_scr[hh]
        o_t = jnp.concatenate([acc / acc[ONES_ROW:ONES_ROW + 1, :],
                               jnp.zeros((LANES - V_ROWS, tq), F32)], axis=0)
        outs.append(o_t.T)
    low = lax.broadcasted_iota(jnp.int32, (tq, LANES), 1) < MLA_V
    o_ref[0] = jnp.where(low, outs[0], pltpu.roll(outs[1], MLA_V, 1)).astype(o_ref.dtype)


def _kv_head(p):
    mla_pairs = MLA_HEADS // 2
    pairs_per_kv = GQA_GROUP // 2
    gqa = MLA_HEADS + (p - mla_pairs) // pairs_per_kv
    return jnp.where(p < mla_pairs, 2 * p, gqa), jnp.where(p < mla_pairs, 2 * p + 1, gqa)


def _attn_lat_call(q_all, k_all, v_all, n_lat):
    b, _, n, _ = q_all.shape
    n_pairs = N_HEADS_ALL // 2
    n_chunks = n // ROW_TILE
    n_main = n_lat // KV_CHUNK
    assert n_main >= 2 and n_main % 2 == 0
    kern = functools.partial(_attn_kernel, n_main=n_main, tail_chunk=n_lat // ROW_TILE,
                             pipelined=True)
    k_spec = lambda which: pl.BlockSpec(
        (1, 1, n, LANES), lambda i, p, j: (i, _kv_head(p)[which], 0, 0))
    v_spec = lambda which: pl.BlockSpec(
        (1, 1, n_chunks, V_ROWS, ROW_TILE), lambda i, p, j: (i, _kv_head(p)[which], 0, 0, 0))
    return pl.pallas_call(
        kern,
        grid=(b, n_pairs, n_lat // Q_TILE),
        in_specs=[pl.BlockSpec((1, 2, Q_TILE, LANES), lambda i, p, j: (i, p, j, 0)),
                  k_spec(0), k_spec(1), v_spec(0), v_spec(1)],
        out_specs=pl.BlockSpec((1, Q_TILE, LANES), lambda i, p, j: (i, j, p)),
        out_shape=jax.ShapeDtypeStruct((b, n_lat, n_pairs * LANES), BF16),
        scratch_shapes=[pltpu.VMEM((2, 2, KV_CHUNK, Q_TILE), F32),
                        pltpu.VMEM((2, V_ROWS, Q_TILE), F32)],
        compiler_params=pltpu.CompilerParams(
            dimension_semantics=("arbitrary", "arbitrary", "arbitrary"),
            vmem_limit_bytes=VMEM_LIMIT),
        name="attention",
    )(q_all, k_all, k_all, v_all, v_all)


def _attn_ctx_call(q_all, k_all, v_all, n_lat):
    b, _, n, _ = q_all.shape
    n_pairs = N_HEADS_ALL // 2
    c = n_lat // ROW_TILE
    assert n - n_lat == ROW_TILE
    kern = functools.partial(_attn_kernel, n_main=0, tail_chunk=0, pipelined=False)
    k_spec = lambda which: pl.BlockSpec(
        (1, 1, ROW_TILE, LANES), lambda i, p: (i, _kv_head(p)[which], c, 0))
    v_spec = lambda which: pl.BlockSpec(
        (1, 1, 1, V_ROWS, ROW_TILE), lambda i, p: (i, _kv_head(p)[which], c, 0, 0))
    return pl.pallas_call(
        kern,
        grid=(b, n_pairs),
        in_specs=[pl.BlockSpec((1, 2, ROW_TILE, LANES), lambda i, p: (i, p, c, 0)),
                  k_spec(0), k_spec(1), v_spec(0), v_spec(1)],
        out_specs=pl.BlockSpec((1, ROW_TILE, LANES), lambda i, p: (i, 0, p)),
        out_shape=jax.ShapeDtypeStruct((b, ROW_TILE, n_pairs * LANES), BF16),
        scratch_shapes=[pltpu.VMEM((2, V_ROWS, ROW_TILE), F32)],
        compiler_params=pltpu.CompilerParams(
            dimension_semantics=("arbitrary", "arbitrary"), vmem_limit_bytes=VMEM_LIMIT),
        name="attention_ctx",
    )(q_all, k_all, k_all, v_all, v_all)


def _route_rows(sel, scores):
    rows = [sel[e:e + 1, :] for e in range(N_EXPERTS)]
    srow = [scores[e:e + 1, :] for e in range(N_EXPERTS)]
    grp = []
    for g in range(N_GROUPS):
        a, b, c, d = rows[4 * g:4 * g + 4]
        hi1, lo1 = jnp.maximum(a, b), jnp.minimum(a, b)
        hi2, lo2 = jnp.maximum(c, d), jnp.minimum(c, d)
        top1 = jnp.maximum(hi1, hi2)
        top2 = jnp.maximum(jnp.minimum(hi1, hi2), jnp.maximum(lo1, lo2))
        grp.append(top1 + top2)
    best = jnp.zeros_like(grp[0], dtype=jnp.int32)
    best_v = grp[0]
    for g in range(1, N_GROUPS):
        upd = grp[g] > best_v
        best = jnp.where(upd, g, best)
        best_v = jnp.where(upd, grp[g], best_v)
    picked = []
    keeps = []
    for e in range(N_EXPERTS):
        g = e // EXPERTS_PER_GROUP
        rank = jnp.zeros_like(best)
        for o in range(4 * g, 4 * g + 4):
            if o == e:
                continue
            ahead = rows[o] > rows[e]
            if o < e:
                ahead = ahead | (rows[o] == rows[e])
            rank = rank + ahead.astype(jnp.int32)
        keep = (rank < 2) & (best == g)
        picked.append(jnp.where(keep, srow[e], 0.0))
        keeps.append(keep)
    total = picked[0]
    for e in range(1, N_EXPERTS):
        total = total + picked[e]
    return [p / total for p in picked], best, keeps


def _route_meta(gates, best, keeps):
    mask = jnp.zeros_like(best)
    w_lo = jnp.zeros_like(gates[0])
    w_hi = jnp.zeros_like(gates[0])
    for e in range(N_EXPERTS):
        g, i = divmod(e, EXPERTS_PER_GROUP)
        mask = mask + jnp.where(keeps[e], 1 << i, 0)
        if i == 0:
            w_lo = w_lo + jnp.where(keeps[e], gates[e], 0.0)
            continue
        below = keeps[4 * g]
        for o in range(4 * g + 1, e):
            below = below | keeps[o]
        w_lo = w_lo + jnp.where(keeps[e] & ~below, gates[e], 0.0)
        w_hi = w_hi + jnp.where(keeps[e] & below, gates[e], 0.0)
    cls = (best * (1 << EXPERTS_PER_GROUP) + mask).astype(F32)
    return cls, w_lo, w_hi


def _mid_kernel(o_ref, octx_ref, g_ref, x_ref, mod_ref, woa_ref, wob_ref, wout_ref, gffn_ref,
                wrh_ref, wrl_ref, brc_ref, xn_ref, fext_ref, cnt_ref, run_ref, *, n_lat_tiles):
    o = jnp.where(pl.program_id(1) >= n_lat_tiles, octx_ref[0], o_ref[0])
    n_a = MLA_HEADS * MLA_V
    ya = _dot(o[:, 0:n_a], woa_ref[0])
    yb = _dot(o[:, n_a:], wob_ref[0])
    g = g_ref[0]
    y = jax.nn.sigmoid(g[:, 0:D_MODEL]) * ya + jax.nn.sigmoid(g[:, D_MODEL:]) * yb
    z = _dot(y.astype(BF16), wout_ref[0])
    gt1 = mod_ref[0, 0, 0:1, :]
    sh2 = mod_ref[0, 0, 1:2, :]
    sc2 = mod_ref[0, 0, 2:3, :]
    xn = x_ref[0] + gt1 * z
    xn_ref[0] = xn
    f = _rms_rows(xn, gffn_ref[0]) * (1 + sc2) + sh2
    fext_ref[0, :, 0:D_MODEL] = f
    f_hi, f_lo = _split_bf16(f)
    w_hi = wrh_ref[...]
    logits = _dot(f_hi, w_hi) + _dot(f_lo, w_hi) + _dot(f_hi, wrl_ref[...])
    scores_t = jax.nn.sigmoid(logits).T
    scores = scores_t[0:N_EXPERTS, :]
    sel = scores + brc_ref[0:N_EXPERTS, :]
    cls, w_lo, w_hi_gate = _route_meta(*_route_rows(sel, scores))
    t = scores.shape[1]
    row_id = lax.broadcasted_iota(jnp.int32, (LANES, t), 0)
    meta_t = jnp.where(row_id == META_CLS, cls, 0.0)
    meta_t = jnp.where(row_id == META_W_LO, w_lo, meta_t)
    meta_t = jnp.where(row_id == META_W_HI, w_hi_gate, meta_t)
    meta = meta_t.T

    @pl.when((pl.program_id(0) == 0) & (pl.program_id(1) == 0))
    def _():
        run_ref[...] = jnp.zeros_like(run_ref)

    lane = lax.broadcasted_iota(jnp.int32, (t, LANES), 1)
    onehot = jnp.where(lane.astype(F32) == meta[:, META_CLS:META_CLS + 1], 1.0, 0.0)
    earlier = jnp.where(lax.broadcasted_iota(jnp.int32, (t, t), 0)
                        > lax.broadcasted_iota(jnp.int32, (t, t), 1), 1.0, 0.0).astype(BF16)
    before = _dot(earlier, onehot.astype(BF16)) + run_ref[...]
    rank = jnp.sum(before * onehot, axis=-1, keepdims=True)
    fext_ref[0, :, D_MODEL:] = jnp.where(lane == META_RANK, rank, meta)
    run_ref[...] += jnp.sum(onehot, axis=0, keepdims=True)
    cnt_ref[...] = jnp.broadcast_to(run_ref[...], cnt_ref.shape)


def _mid_call(o_lat, o_ctx, gts, xs, modm, layer, n_rows, n_lat, w_oa, w_ob, w_out, g_ffn, wr_hi,
              wr_lo, br_col):
    b = xs.shape[0]
    n = n_rows
    n_lat_tiles = n_lat // ROW_TILE
    t = ROW_TILE

    def lspec(arr):
        shp = arr.shape
        return pl.BlockSpec((1,) + shp[1:], lambda i, j: (layer,) + (0,) * (len(shp) - 1))

    def full(arr):
        return pl.BlockSpec(arr.shape, lambda i, j: (0,) * arr.ndim)

    return pl.pallas_call(
        functools.partial(_mid_kernel, n_lat_tiles=n_lat_tiles),
        grid=(b, n // t),
        in_specs=[
            pl.BlockSpec((1, t, o_lat.shape[2]), lambda i, j: (i, jnp.minimum(j, n_lat_tiles - 1), 0)),
            pl.BlockSpec((1, t, o_ctx.shape[2]), lambda i, j: (i, 0, 0)),
            pl.BlockSpec((1, t, 2 * D_MODEL), lambda i, j: (i, j, 0)),
            pl.BlockSpec((1, t, D_MODEL), lambda i, j: (i, j, 0)),
            pl.BlockSpec((1, 1, 3, D_MODEL), lambda i, j: (i, jnp.where(j >= n_lat_tiles, 1, 0), 0, 0)),
            lspec(w_oa), lspec(w_ob), lspec(w_out), lspec(g_ffn),
            full(wr_hi), full(wr_lo), full(br_col),
        ],
        out_specs=[
            pl.BlockSpec((1, t, D_MODEL), lambda i, j: (i, j, 0)),
            pl.BlockSpec((1, t, F_EXT), lambda i, j: (i, j, 0)),
            pl.BlockSpec((8, LANES), lambda i, j: (0, 0)),
        ],
        out_shape=[
            jax.ShapeDtypeStruct((b, n, D_MODEL), F32),
            jax.ShapeDtypeStruct((b, n, F_EXT), F32),
            jax.ShapeDtypeStruct((8, LANES), F32),
        ],
        scratch_shapes=[pltpu.VMEM((1, LANES), F32)],
        compiler_params=pltpu.CompilerParams(
            dimension_semantics=("arbitrary", "arbitrary"), vmem_limit_bytes=VMEM_LIMIT),
        name="mixer_out_router",
    )(o_lat, o_ctx, gts, xs, modm, w_oa, w_ob, w_out, g_ffn, wr_hi, wr_lo, br_col)


def _class_expert_tables():
    lo = np.zeros((N_CLASSES,), np.int32)
    hi = np.zeros((N_CLASSES,), np.int32)
    for c in range(N_CLASSES):
        g, mask = divmod(c, 1 << EXPERTS_PER_GROUP)
        bits = [i for i in range(EXPERTS_PER_GROUP) if mask >> i & 1]
        if len(bits) == 2:
            lo[c], hi[c] = EXPERTS_PER_GROUP * g + bits[0], EXPERTS_PER_GROUP * g + bits[1]
    return lo, hi


def _route_plan(meta, counts, n_tiles_max):
    cls = meta[:, META_CLS].astype(jnp.int32)
    rank = meta[:, META_RANK].astype(jnp.int32)
    cnt = counts[0, :N_CLASSES].astype(jnp.int32)
    tiles = (cnt + MOE_TILE - 1) // MOE_TILE
    tile_end = jnp.cumsum(tiles)
    start = (tile_end - tiles) * MOE_TILE
    hit = cls[None, :] == jnp.arange(N_CLASSES, dtype=jnp.int32)[:, None]
    pos = jnp.sum(jnp.where(hit, start[:, None], 0), axis=0) + rank
    n_used = tile_end[-1]
    tile_idx = jnp.minimum(jnp.arange(n_tiles_max, dtype=jnp.int32), n_used - 1)
    tile_cls = jnp.sum(tile_end[None, :] <= tile_idx[:, None], axis=1).astype(jnp.int32)
    lo, hi = _class_expert_tables()
    src = jnp.zeros((n_tiles_max * MOE_TILE,), jnp.int32).at[pos].set(
        jnp.arange(pos.shape[0], dtype=jnp.int32))
    return (pos.astype(jnp.int32), src, tile_idx, jnp.asarray(lo)[tile_cls],
            jnp.asarray(hi)[tile_cls], n_used.reshape(1).astype(jnp.int32))


def _row_copy(src_ref, src_row, dst_ref, dst_row, sem):
    return pltpu.make_async_copy(src_ref.at[pl.ds(src_row, 1)], dst_ref.at[pl.ds(dst_row, 1)], sem)


def _moe_kernel(src_ref, tidx_ref, elo_ref, ehi_ref, nused_ref, f_hbm, wgs_ref, wus_ref, wds_ref,
                wg1_ref, wu1_ref, wd1_ref, wg2_ref, wu2_ref, wd2_ref, ys_ref, buf, sems):
    del elo_ref, ehi_ref
    j = pl.program_id(0)
    slot = j % 2

    def fetch(step, into):
        base = tidx_ref[step] * MOE_TILE
        for r in range(MOE_TILE):
            _row_copy(f_hbm, src_ref[base + r], buf.at[into], r, sems.at[into]).start()

    pl.when(j == 0)(lambda: fetch(0, 0))
    pl.when(j + 1 < pl.num_programs(0))(lambda: fetch(j + 1, 1 - slot))
    for _ in range(MOE_TILE):
        _row_copy(f_hbm, 0, buf.at[slot], 0, sems.at[slot]).wait()
    used = j < nused_ref[0]

    @pl.when(jnp.logical_not(used))
    def _():
        ys_ref[...] = jnp.zeros_like(ys_ref)

    @pl.when(used)
    def _():
        rows = buf[slot]
        x = rows[:, 0:D_MODEL].astype(BF16)
        w_lo = rows[:, D_MODEL + META_W_LO:D_MODEL + META_W_LO + 1]
        w_hi = rows[:, D_MODEL + META_W_HI:D_MODEL + META_W_HI + 1]

        def ffn(wg_ref, wu_ref, wd_ref):
            wg, wu, wd = (r[(0,) * (len(r.shape) - 2)].astype(BF16) for r in (wg_ref, wu_ref, wd_ref))
            a = jax.nn.silu(_dot(x, wg)) * _dot(x, wu)
            return _dot(a.astype(BF16), wd)

        ys_ref[...] = (ffn(wgs_ref, wus_ref, wds_ref) + w_lo * ffn(wg1_ref, wu1_ref, wd1_ref)
                       + w_hi * ffn(wg2_ref, wu2_ref, wd2_ref))


def _moe_call(plan, fext, layer, w_s, w_e):
    _, src, tile_idx, e_lo, e_hi, n_used = plan
    n_tiles = tile_idx.shape[0]

    def routed_spec(w, which):
        return pl.BlockSpec((1, 1) + w.shape[2:], lambda j, s, t, lo, hi, nu: (layer, which(lo, hi)[j], 0, 0))

    def shared_spec(w):
        return pl.BlockSpec((1,) + w.shape[1:], lambda j, s, t, lo, hi, nu: (layer, 0, 0))

    first = lambda lo, hi: lo
    second = lambda lo, hi: hi
    return pl.pallas_call(
        _moe_kernel,
        grid_spec=pltpu.PrefetchScalarGridSpec(
            num_scalar_prefetch=5,
            grid=(n_tiles,),
            in_specs=([pl.BlockSpec(memory_space=pl.ANY)] + [shared_spec(w) for w in w_s]
                      + [routed_spec(w, first) for w in w_e] + [routed_spec(w, second) for w in w_e]),
            out_specs=pl.BlockSpec((MOE_TILE, D_MODEL), lambda j, s, t, lo, hi, nu: (j, 0)),
            scratch_shapes=[pltpu.VMEM((2, MOE_TILE, F_EXT), F32),
                            pltpu.SemaphoreType.DMA((2,))],
        ),
        out_shape=jax.ShapeDtypeStruct((n_tiles * MOE_TILE, D_MODEL), F32),
        compiler_params=pltpu.CompilerParams(
            dimension_semantics=("arbitrary",), vmem_limit_bytes=VMEM_LIMIT),
        name="moe",
    )(src, tile_idx, e_lo, e_hi, n_used, fext, *w_s, *w_e, *w_e)


def _gather_kernel(pos_ref, xn_ref, gt_ref, ys_hbm, o_ref, buf, sems):
    nt = pl.num_programs(1)
    step = pl.program_id(0) * nt + pl.program_id(1)
    total = pl.num_programs(0) * nt
    slot = step % 2

    def fetch(s, into):
        for r in range(MOE_TILE):
            _row_copy(ys_hbm, pos_ref[s * MOE_TILE + r], buf.at[into], r, sems.at[into]).start()

    pl.when(step == 0)(lambda: fetch(0, 0))
    pl.when(step + 1 < total)(lambda: fetch(step + 1, 1 - slot))
    for _ in range(MOE_TILE):
        _row_copy(ys_hbm, 0, buf.at[slot], 0, sems.at[slot]).wait()
    o_ref[0] = xn_ref[0] + gt_ref[0, 0] * buf[slot]


def _gather_call(pos, xn, gt2, ys, n_lat):
    b, n, _ = xn.shape
    n_lat_tiles = n_lat // MOE_TILE
    return pl.pallas_call(
        _gather_kernel,
        grid_spec=pltpu.PrefetchScalarGridSpec(
            num_scalar_prefetch=1,
            grid=(b, n // MOE_TILE),
            in_specs=[
                pl.BlockSpec((1, MOE_TILE, D_MODEL), lambda i, j, p: (i, j, 0)),
                pl.BlockSpec((1, 1, 1, D_MODEL),
                             lambda i, j, p: (i, jnp.where(j >= n_lat_tiles, 1, 0), 0, 0)),
                pl.BlockSpec(memory_space=pl.ANY),
            ],
            out_specs=pl.BlockSpec((1, MOE_TILE, D_MODEL), lambda i, j, p: (i, j, 0)),
            scratch_shapes=[pltpu.VMEM((2, MOE_TILE, D_MODEL), F32),
                            pltpu.SemaphoreType.DMA((2,))],
        ),
        out_shape=jax.ShapeDtypeStruct((b, n, D_MODEL), F32),
        compiler_params=pltpu.CompilerParams(
            dimension_semantics=("arbitrary", "arbitrary"), vmem_limit_bytes=VMEM_LIMIT),
        name="moe_gather_residual",
    )(pos, xn, gt2[:, :, None, :], ys)


def _pad_heads(w, n_heads, dim, width=LANES):
    lead = w.shape[:-1]
    w = w.reshape(lead + (n_heads, dim))
    w = jnp.pad(w, [(0, 0)] * len(lead) + [(0, 0), (0, width - dim)])
    return w.reshape(lead + (n_heads * width,))


def _partner_lanes(lane0, half_dim):
    perm = np.arange(LANES)
    quarter = half_dim // 2
    for part in range(2):
        lo = lane0 + part * half_dim
        perm[lo:lo + quarter] = np.arange(lo + quarter, lo + half_dim)
        perm[lo + quarter:lo + half_dim] = np.arange(lo, lo + quarter)
    return perm


PARTNER_MLA = _partner_lanes(MLA_NOPE, MLA_ROPE // 2)
PARTNER_GQA = _partner_lanes(0, GQA_HD // 2)


def _swap_partners(w, perm):
    lead = w.shape[:-1]
    return w.reshape(lead + (-1, LANES))[..., perm].reshape(w.shape)


def _prep_w_in(w_in):
    offs = np.concatenate([[0], np.cumsum(IN_SIZES)])
    cq, ckv, kr, qb, kb, vb, gts = [w_in[..., int(offs[i]):int(offs[i + 1])] for i in range(7)]
    kr_blk = jnp.pad(kr, [(0, 0), (0, 0), (MLA_NOPE, LANES - MLA_QK)])
    qb_p, kb_p = _pad_heads(qb, GQA_HEADS, GQA_HD), _pad_heads(kb, GQA_KV_HEADS, GQA_HD)
    w_main = jnp.concatenate(
        [cq, ckv, kr_blk, _swap_partners(kr_blk, PARTNER_MLA), qb_p, _swap_partners(qb_p, PARTNER_GQA),
         kb_p, _swap_partners(kb_p, PARTNER_GQA), gts], axis=-1).astype(BF16)
    w_vbt = jnp.swapaxes(_pad_heads(vb, GQA_KV_HEADS, GQA_HD, V_ROWS), 1, 2).astype(BF16)
    return w_main, w_vbt


def _prep_w_ukv(w_ukv):
    l, r, _ = w_ukv.shape
    w = w_ukv.reshape(l, r, MLA_HEADS, MLA_NOPE + MLA_V)
    uk = _pad_heads(w[..., :MLA_NOPE].reshape(l, r, MLA_HEADS * MLA_NOPE), MLA_HEADS, MLA_NOPE)
    uv = _pad_heads(w[..., MLA_NOPE:].reshape(l, r, MLA_HEADS * MLA_V), MLA_HEADS, MLA_V, V_ROWS)
    return uk.astype(BF16), jnp.swapaxes(uv, 1, 2).astype(BF16)


def _pad_gain(g, dim):
    return jnp.pad(g, [(0, 0), (0, LANES - dim)])


def _gain_rows(g_qa, g_ka, g_qb, g_kb):
    rows = []
    for g, dim, perm in ((g_qa, MLA_QK, PARTNER_MLA), (g_ka, MLA_QK, PARTNER_MLA),
                         (g_qb, GQA_HD, PARTNER_GQA), (g_kb, GQA_HD, PARTNER_GQA)):
        g_p = _pad_gain(g, dim)
        rows += [g_p, g_p[:, perm]]
    return jnp.stack(rows, axis=1)


def _rope_tables(n_ctx, n_lat, lane0, half_dim):
    pos = jnp.arange(n_lat)
    inv = ROPE_THETA ** (-jnp.arange(0, half_dim, 2, dtype=F32) / half_dim)
    ang_r = (pos // GRID_W).astype(F32)[:, None] * inv[None, :]
    ang_c = (pos % GRID_W).astype(F32)[:, None] * inv[None, :]
    cos = jnp.concatenate([jnp.cos(ang_r)] * 2 + [jnp.cos(ang_c)] * 2, axis=-1)
    sin = jnp.concatenate([-jnp.sin(ang_r), jnp.sin(ang_r), -jnp.sin(ang_c), jnp.sin(ang_c)], axis=-1)
    pad = [(0, n_ctx), (lane0, LANES - lane0 - 2 * half_dim)]
    return (jnp.pad(cos, pad, constant_values=1.0), jnp.pad(sin, pad))


def kernel(x, c, ctx, c_ctx, w_mod, b_mod, g_attn, g_ffn, w_in, g_cq, w_uq, g_ckv, w_ukv,
           g_qa, g_ka, g_qb, g_kb, w_oa, w_ob, w_out, w_router, b_router,
           w_e_gate, w_e_up, w_e_down, w_s_gate, w_s_up, w_s_down):
    b, n_lat, d = x.shape
    n_ctx = ctx.shape[1]
    n = n_lat + n_ctx
    depth = w_mod.shape[0]
    assert d == D_MODEL and n_ctx == ROW_TILE and n_lat % KV_CHUNK == 0 and n_lat % Q_TILE == 0

    xs = jnp.concatenate([x, ctx], axis=1)
    cs = jnp.concatenate([c, c_ctx[None, :], jnp.zeros((8 - b - 1, d), F32)], axis=0)

    w_in_p, w_vbt = _prep_w_in(w_in)
    w_uq_p = _pad_heads(w_uq, MLA_HEADS, MLA_QK)
    w_uq_p = jnp.concatenate([w_uq_p, _swap_partners(w_uq_p, PARTNER_MLA)], axis=-1).astype(BF16)
    w_uk_p, w_uvt = _prep_w_ukv(w_ukv)
    g3 = lambda g: g[:, None, :]
    gains = _gain_rows(g_qa, g_ka, g_qb, g_kb)
    tables = (_rope_tables(n_ctx, n_lat, MLA_NOPE, MLA_ROPE // 2)
              + _rope_tables(n_ctx, n_lat, 0, GQA_HD // 2))
    w_oa_b, w_ob_b, w_out_b = w_oa.astype(BF16), w_ob.astype(BF16), w_out.astype(BF16)
    wr_p = jnp.pad(w_router, [(0, 0), (0, LANES - N_EXPERTS)])
    wr_hi = wr_p.astype(BF16)
    wr_lo = (wr_p - wr_hi.astype(F32)).astype(BF16)
    br_col = jnp.pad(b_router, (0, LANES - N_EXPERTS))[:, None]
    w_shared = (w_s_gate, w_s_up, w_s_down)
    w_routed = (w_e_gate, w_e_up, w_e_down)
    b_mod3 = b_mod[:, None, :]

    for layer in range(depth):
        last = layer == depth - 1
        n_rows = n_lat if last else n
        mods = _mod_call(cs, w_mod, b_mod3, layer).reshape(8, 6, d)
        lat, cx = mods[:b], jnp.broadcast_to(mods[b:b + 1], (b, 6, d))
        both = jnp.stack([lat, cx], axis=1)
        mod1, modm, gt2 = both[:, :, 0:2], both[:, :, 2:5], both[:, :, 5]
        q_all, k_all, v_all, gts = _in_call(
            xs, mod1, layer, n_lat, g3(g_attn), w_in_p, g3(g_cq), w_uq_p, g3(g_ckv), w_uk_p, w_uvt,
            w_vbt, gains, tables)
        o_lat = _attn_lat_call(q_all, k_all, v_all, n_lat)
        o_ctx = o_lat if last else _attn_ctx_call(q_all, k_all, v_all, n_lat)
        xn, fext, counts = _mid_call(o_lat, o_ctx, gts, xs, modm, layer, n_rows, n_lat, w_oa_b,
                                     w_ob_b, w_out_b, g3(g_ffn), wr_hi, wr_lo, br_col)
        n_tok = b * n_rows
        n_tiles_max = n_tok // MOE_TILE + N_PAIR_CLASSES
        fext = fext.reshape(n_tok, F_EXT)
        plan = _route_plan(fext[:, D_MODEL:D_MODEL + 4], counts, n_tiles_max)
        ys = _moe_call(plan, fext, layer, w_shared, w_routed)
        xs = _gather_call(plan[0], xn, gt2, ys, n_lat)
    return xs
```

```python
import functools

import numpy as np
import jax
import jax.numpy as jnp
from jax import lax
from jax.experimental import pallas as pl
from jax.experimental.pallas import tpu as pltpu

D_MODEL = 1024
GRID_W = 64
MLA_HEADS = 8
MLA_NOPE = 64
MLA_ROPE = 32
MLA_QK = MLA_NOPE + MLA_ROPE
MLA_V = 64
Q_LORA = 256
KV_LORA = 128
GQA_HEADS = 8
GQA_KV_HEADS = 2
GQA_GROUP = GQA_HEADS // GQA_KV_HEADS
GQA_HD = 64
N_EXPERTS = 16
N_GROUPS = 4
EXPERTS_PER_GROUP = N_EXPERTS // N_GROUPS
FF_EXPERT = 512
ROPE_THETA = 10000.0
EPS = 1e-6
IN_SIZES = (Q_LORA, KV_LORA, MLA_ROPE, GQA_HEADS * GQA_HD, GQA_KV_HEADS * GQA_HD,
            GQA_KV_HEADS * GQA_HD, 2 * D_MODEL)

LANES = 128
ROW_TILE = 256
Q_TILE = 1024
KV_CHUNK = 512
N_HEADS_ALL = MLA_HEADS + GQA_HEADS
N_KHEADS_ALL = MLA_HEADS + GQA_KV_HEADS
LOG2_E = 1.4426950408889634
ONES_ROW = MLA_V
V_ROWS = MLA_V + 16
VMEM_LIMIT = 56 * 1024 * 1024
MOE_TILE = 256
MOE_AHEAD = 2
N_CLASSES = N_GROUPS << EXPERTS_PER_GROUP
N_PAIR_CLASSES = N_GROUPS * 6
F_EXT = D_MODEL + LANES
META_CLS, META_RANK, META_W_LO, META_W_HI = 0, 1, 2, 3

C_CQ = 0
C_CKV = C_CQ + Q_LORA
C_KR = C_CKV + KV_LORA
C_QB = C_KR + 2 * LANES
C_KB = C_QB + 2 * GQA_HEADS * LANES
C_G = C_KB + 2 * GQA_KV_HEADS * LANES
C_END = C_G + 2 * D_MODEL

BF16 = jnp.bfloat16
F32 = jnp.float32


def _dot(a, b):
    return jnp.dot(a, b, preferred_element_type=F32)


def _dot_nt(a, b):
    return lax.dot_general(a, b, (((1,), (1,)), ((), ())), preferred_element_type=F32)


def _split_bf16(x):
    hi = x.astype(BF16)
    lo = (x - hi.astype(F32)).astype(BF16)
    return hi, lo


def _rms_rows(x, g):
    ms = jnp.mean(x * x, axis=-1, keepdims=True)
    return x * lax.rsqrt(ms + EPS) * g


def _norm_rope(raw, swapped, gain_cos, gain_sin, dim):
    ms = jnp.sum(raw * raw, axis=-1, keepdims=True) * (1.0 / dim)
    return (raw * gain_cos + swapped * gain_sin) * lax.rsqrt(ms + EPS)


def _mod_kernel(c_ref, w_ref, b_ref, o_ref):
    c = c_ref[...]
    s_hi, s_lo = _split_bf16(c * jax.nn.sigmoid(c))
    w_hi, w_lo = _split_bf16(w_ref[0])
    o_ref[...] = _dot(s_hi, w_hi) + _dot(s_lo, w_hi) + _dot(s_hi, w_lo) + b_ref[0]


def _mod_call(cs, w_mod, b_mod, layer):
    n_cols = w_mod.shape[2]
    tn = 1536
    return pl.pallas_call(
        _mod_kernel,
        grid=(n_cols // tn,),
        in_specs=[
            pl.BlockSpec(cs.shape, lambda j: (0, 0)),
            pl.BlockSpec((1, D_MODEL, tn), lambda j: (layer, 0, j)),
            pl.BlockSpec((1, 1, tn), lambda j: (layer, 0, j)),
        ],
        out_specs=pl.BlockSpec((cs.shape[0], tn), lambda j: (0, j)),
        out_shape=jax.ShapeDtypeStruct((cs.shape[0], n_cols), F32),
        compiler_params=pltpu.CompilerParams(
            dimension_semantics=("arbitrary",), vmem_limit_bytes=VMEM_LIMIT),
        name="mod",
    )(cs, w_mod, b_mod)


def _in_kernel(x_ref, xc_ref, mod_ref, gattn_ref, win_ref, gcq_ref, wuq_ref, gckv_ref, wuk_ref,
               wuvt_ref, wvbt_ref, gains_ref, cosa_ref, sina_ref, cosb_ref, sinb_ref,
               q_ref, k_ref, v_ref, g_ref, *, n_lat_tiles):
    x = jnp.where(pl.program_id(1) >= n_lat_tiles, xc_ref[0], x_ref[0])
    shift = mod_ref[0, 0, 0:1, :]
    scale = mod_ref[0, 0, 1:2, :]
    h = (_rms_rows(x, gattn_ref[0]) * (1 + scale) + shift).astype(BF16)

    g_ref[0] = _dot(h, win_ref[0, :, C_G:C_END])

    gains = gains_ref[0]
    cos_a, sin_a, cos_b, sin_b = cosa_ref[...], sina_ref[...], cosb_ref[...], sinb_ref[...]
    scale_a = MLA_QK ** -0.5 * LOG2_E
    scale_b = GQA_HD ** -0.5 * LOG2_E
    qa_cos, qa_sin = cos_a * (gains[0:1] * scale_a), sin_a * (gains[1:2] * scale_a)
    ka_cos, ka_sin = cos_a * gains[2:3], sin_a * gains[3:4]
    qb_cos, qb_sin = cos_b * (gains[4:5] * scale_b), sin_b * (gains[5:6] * scale_b)
    kb_cos, kb_sin = cos_b * gains[6:7], sin_b * gains[7:8]
    t = x.shape[0]
    ones_row = lax.broadcasted_iota(jnp.int32, (V_ROWS, t), 0) == ONES_ROW
    blk = lambda a, i: a[:, i * LANES:(i + 1) * LANES]

    p0 = _dot(h, win_ref[0, :, C_CQ:C_QB])
    cq = p0[:, C_CQ:C_CKV]
    ckv = p0[:, C_CKV:C_KR]
    kr_blk = p0[:, C_KR:C_KR + LANES]
    kr_swapped = p0[:, C_KR + LANES:C_QB]
    qa = _dot(_rms_rows(cq, gcq_ref[0]).astype(BF16), wuq_ref[0])
    ckv_n = _rms_rows(ckv, gckv_ref[0]).astype(BF16)
    ka = _dot(ckv_n, wuk_ref[0])
    va_t = _dot_nt(wuvt_ref[0], ckv_n)
    for hd in range(MLA_HEADS):
        q_ref[0, hd] = _norm_rope(blk(qa, hd), blk(qa, MLA_HEADS + hd), qa_cos, qa_sin,
                                  MLA_QK).astype(BF16)
        k_ref[0, hd] = _norm_rope(blk(ka, hd) + kr_blk, kr_swapped, ka_cos, ka_sin,
                                  MLA_QK).astype(BF16)
        v_ref[0, hd, 0] = jnp.where(ones_row, 1.0, va_t[hd * V_ROWS:(hd + 1) * V_ROWS, :]).astype(BF16)

    pq = _dot(h, win_ref[0, :, C_QB:C_KB])
    for hd in range(GQA_HEADS):
        q_ref[0, MLA_HEADS + hd] = _norm_rope(blk(pq, hd), blk(pq, GQA_HEADS + hd), qb_cos, qb_sin,
                                              GQA_HD).astype(BF16)
    pk = _dot(h, win_ref[0, :, C_KB:C_G])
    vb_t = _dot_nt(wvbt_ref[0], h)
    for hd in range(GQA_KV_HEADS):
        k_ref[0, MLA_HEADS + hd] = _norm_rope(blk(pk, hd), blk(pk, GQA_KV_HEADS + hd), kb_cos,
                                              kb_sin, GQA_HD).astype(BF16)
        v_ref[0, MLA_HEADS + hd, 0] = jnp.where(
            ones_row, 1.0, vb_t[hd * V_ROWS:(hd + 1) * V_ROWS, :]).astype(BF16)


def _stream_specs(n_lat_tiles, ctx_tile):
    t = ROW_TILE
    return [pl.BlockSpec((1, t, D_MODEL), lambda i, j: (i, jnp.minimum(j, n_lat_tiles - 1), 0)),
            pl.BlockSpec((1, t, D_MODEL), lambda i, j: (i, ctx_tile, 0))]


def _in_call(x_lat, x_ctx, ctx_tile, mod1, layer, n, n_lat, g_attn, w_in_p, g_cq, w_uq_p, g_ckv,
             w_uk_p, w_uvt, w_vbt, gains, tables):
    b = x_lat.shape[0]
    nt = n // ROW_TILE
    n_lat_tiles = n_lat // ROW_TILE
    t = ROW_TILE

    def lspec(arr):
        shp = arr.shape
        return pl.BlockSpec((1,) + shp[1:], lambda i, j: (layer,) + (0,) * (len(shp) - 1))

    tab_spec = pl.BlockSpec((t, LANES), lambda i, j: (j, 0))
    return pl.pallas_call(
        functools.partial(_in_kernel, n_lat_tiles=n_lat_tiles),
        grid=(b, nt),
        in_specs=_stream_specs(n_lat_tiles, ctx_tile) + [
            pl.BlockSpec((1, 1, 2, D_MODEL), lambda i, j: (i, jnp.where(j >= n_lat_tiles, 1, 0), 0, 0)),
            lspec(g_attn), lspec(w_in_p), lspec(g_cq), lspec(w_uq_p), lspec(g_ckv),
            lspec(w_uk_p), lspec(w_uvt), lspec(w_vbt), lspec(gains),
        ] + [tab_spec] * 4,
        out_specs=[
            pl.BlockSpec((1, N_HEADS_ALL, t, LANES), lambda i, j: (i, 0, j, 0)),
            pl.BlockSpec((1, N_KHEADS_ALL, t, LANES), lambda i, j: (i, 0, j, 0)),
            pl.BlockSpec((1, N_KHEADS_ALL, 1, V_ROWS, t), lambda i, j: (i, 0, j, 0, 0)),
            pl.BlockSpec((1, t, 2 * D_MODEL), lambda i, j: (i, j, 0)),
        ],
        out_shape=[
            jax.ShapeDtypeStruct((b, N_HEADS_ALL, n, LANES), BF16),
            jax.ShapeDtypeStruct((b, N_KHEADS_ALL, n, LANES), BF16),
            jax.ShapeDtypeStruct((b, N_KHEADS_ALL, nt, V_ROWS, t), BF16),
            jax.ShapeDtypeStruct((b, n, 2 * D_MODEL), F32),
        ],
        compiler_params=pltpu.CompilerParams(
            dimension_semantics=("arbitrary", "arbitrary"), vmem_limit_bytes=VMEM_LIMIT),
        name="mixer_in",
    )(x_lat, x_ctx, mod1, g_attn, w_in_p, g_cq, w_uq_p, g_ckv, w_uk_p, w_uvt, w_vbt, gains, *tables)


def _attn_kernel(q_ref, k0_ref, k1_ref, v0_ref, v1_ref, *rest, n_main, tail_chunk, pipelined):
    if pipelined:
        o_ref, s_scr, acc_scr = rest[-3:]
    else:
        o_ref, acc_scr = rest[-2:]
    qs = (q_ref[0, 0], q_ref[0, 1])
    k_refs = (k0_ref, k1_ref)
    v_refs = (v0_ref, v1_ref)
    tq = qs[0].shape[0]
    per = KV_CHUNK // ROW_TILE

    def fold(hh, m, cmax, st, chunk0, n_sub, first=False):
        m_new = cmax if first else jnp.maximum(m, cmax)
        pt = jnp.exp2(st - m_new).astype(BF16)
        pv = _dot(v_refs[hh][0, 0, chunk0], pt[0:ROW_TILE])
        for u in range(1, n_sub):
            pv = pv + _dot(v_refs[hh][0, 0, chunk0 + u], pt[u * ROW_TILE:(u + 1) * ROW_TILE])
        if first:
            acc_scr[hh] = pv
        else:
            acc_scr[hh] = jnp.exp2(m - m_new) * acc_scr[hh] + pv
        return m_new

    def produce(c, slot):
        cmax = []
        for hh in range(2):
            off = c * KV_CHUNK
            if not isinstance(off, int):
                off = pl.multiple_of(off, KV_CHUNK)
            st = _dot_nt(k_refs[hh][0, 0, pl.ds(off, KV_CHUNK), :], qs[hh])
            s_scr[slot, hh] = st
            cmax.append(jnp.max(st, axis=0, keepdims=True))
        return tuple(cmax)

    def consume(c, slot, ms, cmax):
        return tuple(fold(hh, ms[hh], cmax[hh], s_scr[slot, hh], c * per, per) for hh in range(2))

    ms = []
    for hh in range(2):
        st = _dot_nt(k_refs[hh][0, 0, pl.ds(tail_chunk * ROW_TILE, ROW_TILE), :], qs[hh])
        ms.append(fold(hh, None, jnp.max(st, axis=0, keepdims=True), st, tail_chunk, 1, first=True))
    ms = tuple(ms)

    if pipelined:
        def body(i, carry):
            ms, cmax = carry
            cmax1 = produce(2 * i + 1, 1)
            ms = consume(2 * i, 0, ms, cmax)
            cmax2 = produce(2 * i + 2, 0)
            ms = consume(2 * i + 1, 1, ms, cmax1)
            return ms, cmax2

        ms, cmax = lax.fori_loop(0, (n_main - 2) // 2, body, (ms, produce(0, 0)))
        cmax1 = produce(n_main - 1, 1)
        ms = consume(n_main - 2, 0, ms, cmax)
        ms = consume(n_main - 1, 1, ms, cmax1)

    outs = []
    for hh in range(2):
        acc = acc_scr[hh]
        o_t = jnp.concatenate([acc / acc[ONES_ROW:ONES_ROW + 1, :],
                               jnp.zeros((LANES - V_ROWS, tq), F32)], axis=0)
        outs.append(o_t.T)
    low = lax.broadcasted_iota(jnp.int32, (tq, LANES), 1) < MLA_V
    o_ref[0] = jnp.where(low, outs[0], pltpu.roll(outs[1], MLA_V, 1)).astype(o_ref.dtype)


def _kv_head(p):
    mla_pairs = MLA_HEADS // 2
    pairs_per_kv = GQA_GROUP // 2
    gqa = MLA_HEADS + (p - mla_pairs) // pairs_per_kv
    return jnp.where(p < mla_pairs, 2 * p, gqa), jnp.where(p < mla_pairs, 2 * p + 1, gqa)


def _attn_lat_call(q_all, k_all, v_all, n_lat):
    b, _, n, _ = q_all.shape
    n_pairs = N_HEADS_ALL // 2
    n_chunks = n // ROW_TILE
    n_main = n_lat // KV_CHUNK
    assert n_main >= 2 and n_main % 2 == 0
    kern = functools.partial(_attn_kernel, n_main=n_main, tail_chunk=n_lat // ROW_TILE,
                             pipelined=True)
    k_spec = lambda which: pl.BlockSpec(
        (1, 1, n, LANES), lambda i, p, j: (i, _kv_head(p)[which], 0, 0))
    v_spec = lambda which: pl.BlockSpec(
        (1, 1, n_chunks, V_ROWS, ROW_TILE), lambda i, p, j: (i, _kv_head(p)[which], 0, 0, 0))
    return pl.pallas_call(
        kern,
        grid=(b, n_pairs, n_lat // Q_TILE),
        in_specs=[pl.BlockSpec((1, 2, Q_TILE, LANES), lambda i, p, j: (i, p, j, 0)),
                  k_spec(0), k_spec(1), v_spec(0), v_spec(1)],
        out_specs=pl.BlockSpec((1, Q_TILE, LANES), lambda i, p, j: (i, j, p)),
        out_shape=jax.ShapeDtypeStruct((b, n_lat, n_pairs * LANES), BF16),
        scratch_shapes=[pltpu.VMEM((2, 2, KV_CHUNK, Q_TILE), F32),
                        pltpu.VMEM((2, V_ROWS, Q_TILE), F32)],
        compiler_params=pltpu.CompilerParams(
            dimension_semantics=("arbitrary", "arbitrary", "arbitrary"),
            vmem_limit_bytes=VMEM_LIMIT),
        name="attention",
    )(q_all, k_all, k_all, v_all, v_all)


def _attn_ctx_call(q_all, k_all, v_all, n_lat):
    b, _, n, _ = q_all.shape
    n_pairs = N_HEADS_ALL // 2
    c = n_lat // ROW_TILE
    assert n - n_lat == ROW_TILE
    kern = functools.partial(_attn_kernel, n_main=0, tail_chunk=0, pipelined=False)
    k_spec = lambda which: pl.BlockSpec(
        (1, 1, ROW_TILE, LANES), lambda i, p: (i, _kv_head(p)[which], c, 0))
    v_spec = lambda which: pl.BlockSpec(
        (1, 1, 1, V_ROWS, ROW_TILE), lambda i, p: (i, _kv_head(p)[which], c, 0, 0))
    return pl.pallas_call(
        kern,
        grid=(b, n_pairs),
        in_specs=[pl.BlockSpec((1, 2, ROW_TILE, LANES), lambda i, p: (i, p, c, 0)),
                  k_spec(0), k_spec(1), v_spec(0), v_spec(1)],
        out_specs=pl.BlockSpec((1, ROW_TILE, LANES), lambda i, p: (i, 0, p)),
        out_shape=jax.ShapeDtypeStruct((b, ROW_TILE, n_pairs * LANES), BF16),
        scratch_shapes=[pltpu.VMEM((2, V_ROWS, ROW_TILE), F32)],
        compiler_params=pltpu.CompilerParams(
            dimension_semantics=("arbitrary", "arbitrary"), vmem_limit_bytes=VMEM_LIMIT),
        name="attention_ctx",
    )(q_all, k_all, k_all, v_all, v_all)


def _route_rows(sel, scores):
    rows = [sel[e:e + 1, :] for e in range(N_EXPERTS)]
    srow = [scores[e:e + 1, :] for e in range(N_EXPERTS)]
    grp = []
    for g in range(N_GROUPS):
        a, b, c, d = rows[4 * g:4 * g + 4]
        hi1, lo1 = jnp.maximum(a, b), jnp.minimum(a, b)
        hi2, lo2 = jnp.maximum(c, d), jnp.minimum(c, d)
        top1 = jnp.maximum(hi1, hi2)
        top2 = jnp.maximum(jnp.minimum(hi1, hi2), jnp.maximum(lo1, lo2))
        grp.append(top1 + top2)
    best = jnp.zeros_like(grp[0], dtype=jnp.int32)
    best_v = grp[0]
    for g in range(1, N_GROUPS):
        upd = grp[g] > best_v
        best = jnp.where(upd, g, best)
        best_v = jnp.where(upd, grp[g], best_v)
    picked = []
    keeps = []
    for e in range(N_EXPERTS):
        g = e // EXPERTS_PER_GROUP
        rank = jnp.zeros_like(best)
        for o in range(4 * g, 4 * g + 4):
            if o == e:
                continue
            ahead = rows[o] > rows[e]
            if o < e:
                ahead = ahead | (rows[o] == rows[e])
            rank = rank + ahead.astype(jnp.int32)
        keep = (rank < 2) & (best == g)
        picked.append(jnp.where(keep, srow[e], 0.0))
        keeps.append(keep)
    total = picked[0]
    for e in range(1, N_EXPERTS):
        total = total + picked[e]
    return [p / total for p in picked], best, keeps


def _route_meta(gates, best, keeps):
    mask = jnp.zeros_like(best)
    w_lo = jnp.zeros_like(gates[0])
    w_hi = jnp.zeros_like(gates[0])
    for e in range(N_EXPERTS):
        g, i = divmod(e, EXPERTS_PER_GROUP)
        mask = mask + jnp.where(keeps[e], 1 << i, 0)
        if i == 0:
            w_lo = w_lo + jnp.where(keeps[e], gates[e], 0.0)
            continue
        below = keeps[4 * g]
        for o in range(4 * g + 1, e):
            below = below | keeps[o]
        w_lo = w_lo + jnp.where(keeps[e] & ~below, gates[e], 0.0)
        w_hi = w_hi + jnp.where(keeps[e] & below, gates[e], 0.0)
    cls = (best * (1 << EXPERTS_PER_GROUP) + mask).astype(F32)
    return cls, w_lo, w_hi


def _mid_kernel(o_ref, octx_ref, g_ref, x_ref, xc_ref, mod_ref, woa_ref, wob_ref, wout_ref, gffn_ref,
                wrh_ref, wrl_ref, brc_ref, xn_ref, fext_ref, cnt_ref, run_ref, *, n_lat_tiles):
    is_ctx = pl.program_id(1) >= n_lat_tiles
    o = jnp.where(is_ctx, octx_ref[0], o_ref[0])
    n_a = MLA_HEADS * MLA_V
    ya = _dot(o[:, 0:n_a], woa_ref[0])
    yb = _dot(o[:, n_a:], wob_ref[0])
    g = g_ref[0]
    y = jax.nn.sigmoid(g[:, 0:D_MODEL]) * ya + jax.nn.sigmoid(g[:, D_MODEL:]) * yb
    z = _dot(y.astype(BF16), wout_ref[0])
    gt1 = mod_ref[0, 0, 0:1, :]
    sh2 = mod_ref[0, 0, 1:2, :]
    sc2 = mod_ref[0, 0, 2:3, :]
    xn = jnp.where(is_ctx, xc_ref[0], x_ref[0]) + gt1 * z
    xn_ref[0] = xn
    f = _rms_rows(xn, gffn_ref[0]) * (1 + sc2) + sh2
    fext_ref[0, :, 0:D_MODEL] = f
    f_hi, f_lo = _split_bf16(f)
    w_hi = wrh_ref[...]
    logits = _dot(f_hi, w_hi) + _dot(f_lo, w_hi) + _dot(f_hi, wrl_ref[...])
    scores_t = jax.nn.sigmoid(logits).T
    scores = scores_t[0:N_EXPERTS, :]
    sel = scores + brc_ref[0:N_EXPERTS, :]
    cls, w_lo, w_hi_gate = _route_meta(*_route_rows(sel, scores))
    t = scores.shape[1]
    row_id = lax.broadcasted_iota(jnp.int32, (LANES, t), 0)
    meta_t = jnp.where(row_id == META_CLS, cls, 0.0)
    meta_t = jnp.where(row_id == META_W_LO, w_lo, meta_t)
    meta_t = jnp.where(row_id == META_W_HI, w_hi_gate, meta_t)
    meta = meta_t.T

    @pl.when((pl.program_id(0) == 0) & (pl.program_id(1) == 0))
    def _():
        run_ref[...] = jnp.zeros_like(run_ref)

    lane = lax.broadcasted_iota(jnp.int32, (t, LANES), 1)
    onehot = jnp.where(lane.astype(F32) == meta[:, META_CLS:META_CLS + 1], 1.0, 0.0)
    earlier = jnp.where(lax.broadcasted_iota(jnp.int32, (t, t), 0)
                        > lax.broadcasted_iota(jnp.int32, (t, t), 1), 1.0, 0.0).astype(BF16)
    before = _dot(earlier, onehot.astype(BF16)) + run_ref[...]
    rank = jnp.sum(before * onehot, axis=-1, keepdims=True)
    fext_ref[0, :, D_MODEL:] = jnp.where(lane == META_RANK, rank, meta)
    run_ref[...] += jnp.sum(onehot, axis=0, keepdims=True)
    cnt_ref[...] = jnp.broadcast_to(run_ref[...], cnt_ref.shape)


def _mid_call(o_lat, o_ctx, gts, x_lat, x_ctx, ctx_tile, modm, layer, n_rows, n_lat, w_oa, w_ob, w_out,
              g_ffn, wr_hi, wr_lo, br_col):
    b = x_lat.shape[0]
    n = n_rows
    n_lat_tiles = n_lat // ROW_TILE
    t = ROW_TILE

    def lspec(arr):
        shp = arr.shape
        return pl.BlockSpec((1,) + shp[1:], lambda i, j: (layer,) + (0,) * (len(shp) - 1))

    def full(arr):
        return pl.BlockSpec(arr.shape, lambda i, j: (0,) * arr.ndim)

    return pl.pallas_call(
        functools.partial(_mid_kernel, n_lat_tiles=n_lat_tiles),
        grid=(b, n // t),
        in_specs=[
            pl.BlockSpec((1, t, o_lat.shape[2]), lambda i, j: (i, jnp.minimum(j, n_lat_tiles - 1), 0)),
            pl.BlockSpec((1, t, o_ctx.shape[2]), lambda i, j: (i, 0, 0)),
            pl.BlockSpec((1, t, 2 * D_MODEL), lambda i, j: (i, j, 0)),
        ] + _stream_specs(n_lat_tiles, ctx_tile) + [
            pl.BlockSpec((1, 1, 3, D_MODEL), lambda i, j: (i, jnp.where(j >= n_lat_tiles, 1, 0), 0, 0)),
            lspec(w_oa), lspec(w_ob), lspec(w_out), lspec(g_ffn),
            full(wr_hi), full(wr_lo), full(br_col),
        ],
        out_specs=[
            pl.BlockSpec((1, t, D_MODEL), lambda i, j: (i, j, 0)),
            pl.BlockSpec((1, t, F_EXT), lambda i, j: (i, j, 0)),
            pl.BlockSpec((8, LANES), lambda i, j: (0, 0)),
        ],
        out_shape=[
            jax.ShapeDtypeStruct((b, n, D_MODEL), F32),
            jax.ShapeDtypeStruct((b, n, F_EXT), F32),
            jax.ShapeDtypeStruct((8, LANES), F32),
        ],
        scratch_shapes=[pltpu.VMEM((1, LANES), F32)],
        compiler_params=pltpu.CompilerParams(
            dimension_semantics=("arbitrary", "arbitrary"), vmem_limit_bytes=VMEM_LIMIT),
        name="mixer_out_router",
    )(o_lat, o_ctx, gts, x_lat, x_ctx, modm, w_oa, w_ob, w_out, g_ffn, wr_hi, wr_lo, br_col)


def _class_expert_tables():
    lo = np.zeros((N_CLASSES,), np.int32)
    hi = np.zeros((N_CLASSES,), np.int32)
    for c in range(N_CLASSES):
        g, mask = divmod(c, 1 << EXPERTS_PER_GROUP)
        bits = [i for i in range(EXPERTS_PER_GROUP) if mask >> i & 1]
        if len(bits) == 2:
            lo[c], hi[c] = EXPERTS_PER_GROUP * g + bits[0], EXPERTS_PER_GROUP * g + bits[1]
    return lo, hi


def _route_plan(meta, counts, n_tiles_max):
    cls = meta[:, META_CLS].astype(jnp.int32)
    rank = meta[:, META_RANK].astype(jnp.int32)
    cnt = counts[0, :N_CLASSES].astype(jnp.int32)
    tiles = (cnt + MOE_TILE - 1) // MOE_TILE
    tile_end = jnp.cumsum(tiles)
    start = (tile_end - tiles) * MOE_TILE
    hit = cls[None, :] == jnp.arange(N_CLASSES, dtype=jnp.int32)[:, None]
    pos = jnp.sum(jnp.where(hit, start[:, None], 0), axis=0) + rank
    n_used = tile_end[-1]
    tile_idx = jnp.minimum(jnp.arange(n_tiles_max, dtype=jnp.int32), n_used - 1)
    tile_cls = jnp.sum(tile_end[None, :] <= tile_idx[:, None], axis=1).astype(jnp.int32)
    lo, hi = _class_expert_tables()
    src = jnp.zeros((n_tiles_max * MOE_TILE,), jnp.int32).at[pos].set(
        jnp.arange(pos.shape[0], dtype=jnp.int32))
    return (pos.astype(jnp.int32), src, tile_idx, jnp.asarray(lo)[tile_cls],
            jnp.asarray(hi)[tile_cls], n_used.reshape(1).astype(jnp.int32))


def _row_copy(src_ref, src_row, dst_ref, dst_row, sem):
    return pltpu.make_async_copy(src_ref.at[pl.ds(src_row, 1)], dst_ref.at[pl.ds(dst_row, 1)], sem)


def _moe_kernel(src_ref, tidx_ref, elo_ref, ehi_ref, nused_ref, f_hbm, wgs_ref, wus_ref, wds_ref,
                wg1_ref, wu1_ref, wd1_ref, wg2_ref, wu2_ref, wd2_ref, ys_ref, buf, sems):
    del elo_ref, ehi_ref
    j = pl.program_id(0)
    n_used = nused_ref[0]
    slot = j % (MOE_AHEAD + 1)

    def fetch(step):
        into = step % (MOE_AHEAD + 1)
        base = tidx_ref[step] * MOE_TILE
        for r in range(MOE_TILE):
            _row_copy(f_hbm, src_ref[base + r], buf.at[into], r, sems.at[into]).start()

    for first in range(MOE_AHEAD):
        pl.when((j == 0) & (first < n_used))(functools.partial(fetch, first))
    pl.when(j + MOE_AHEAD < n_used)(lambda: fetch(j + MOE_AHEAD))
    used = j < n_used

    @pl.when(jnp.logical_not(used))
    def _():
        ys_ref[...] = jnp.zeros_like(ys_ref)

    @pl.when(used)
    def _():
        for _ in range(MOE_TILE):
            _row_copy(f_hbm, 0, buf.at[slot], 0, sems.at[slot]).wait()
        rows = buf[slot]
        x = rows[:, 0:D_MODEL].astype(BF16)
        w_lo = rows[:, D_MODEL + META_W_LO:D_MODEL + META_W_LO + 1]
        w_hi = rows[:, D_MODEL + META_W_HI:D_MODEL + META_W_HI + 1]

        def ffn(wg_ref, wu_ref, wd_ref):
            wg, wu, wd = (r[(0,) * (len(r.shape) - 2)].astype(BF16) for r in (wg_ref, wu_ref, wd_ref))
            a = jax.nn.silu(_dot(x, wg)) * _dot(x, wu)
            return _dot(a.astype(BF16), wd)

        ys_ref[...] = (ffn(wgs_ref, wus_ref, wds_ref) + w_lo * ffn(wg1_ref, wu1_ref, wd1_ref)
                       + w_hi * ffn(wg2_ref, wu2_ref, wd2_ref))


def _moe_call(plan, fext, layer, w_s, w_e):
    _, src, tile_idx, e_lo, e_hi, n_used = plan
    n_tiles = tile_idx.shape[0]

    def routed_spec(w, which):
        return pl.BlockSpec((1, 1) + w.shape[2:], lambda j, s, t, lo, hi, nu: (layer, which(lo, hi)[j], 0, 0))

    def shared_spec(w):
        return pl.BlockSpec((1,) + w.shape[1:], lambda j, s, t, lo, hi, nu: (layer, 0, 0))

    first = lambda lo, hi: lo
    second = lambda lo, hi: hi
    return pl.pallas_call(
        _moe_kernel,
        grid_spec=pltpu.PrefetchScalarGridSpec(
            num_scalar_prefetch=5,
            grid=(n_tiles,),
            in_specs=([pl.BlockSpec(memory_space=pl.ANY)] + [shared_spec(w) for w in w_s]
                      + [routed_spec(w, first) for w in w_e] + [routed_spec(w, second) for w in w_e]),
            out_specs=pl.BlockSpec((MOE_TILE, D_MODEL), lambda j, s, t, lo, hi, nu: (j, 0)),
            scratch_shapes=[pltpu.VMEM((MOE_AHEAD + 1, MOE_TILE, F_EXT), F32),
                            pltpu.SemaphoreType.DMA((MOE_AHEAD + 1,))],
        ),
        out_shape=jax.ShapeDtypeStruct((n_tiles * MOE_TILE, D_MODEL), F32),
        compiler_params=pltpu.CompilerParams(
            dimension_semantics=("arbitrary",), vmem_limit_bytes=VMEM_LIMIT),
        name="moe",
    )(src, tile_idx, e_lo, e_hi, n_used, fext, *w_s, *w_e, *w_e)


def _gather_kernel(pos_ref, xn_ref, gt_ref, ys_hbm, o_ref, buf, sems):
    nt = pl.num_programs(1)
    step = pl.program_id(0) * nt + pl.program_id(1)
    total = pl.num_programs(0) * nt
    slot = step % 2

    def fetch(s, into):
        for r in range(MOE_TILE):
            _row_copy(ys_hbm, pos_ref[s * MOE_TILE + r], buf.at[into], r, sems.at[into]).start()

    pl.when(step == 0)(lambda: fetch(0, 0))
    pl.when(step + 1 < total)(lambda: fetch(step + 1, 1 - slot))
    for _ in range(MOE_TILE):
        _row_copy(ys_hbm, 0, buf.at[slot], 0, sems.at[slot]).wait()
    o_ref[0] = xn_ref[0] + gt_ref[0, 0] * buf[slot]


def _gather_call(pos, xn, gt2, ys, n_lat):
    b, n, _ = xn.shape
    n_lat_tiles = n_lat // MOE_TILE
    return pl.pallas_call(
        _gather_kernel,
        grid_spec=pltpu.PrefetchScalarGridSpec(
            num_scalar_prefetch=1,
            grid=(b, n // MOE_TILE),
            in_specs=[
                pl.BlockSpec((1, MOE_TILE, D_MODEL), lambda i, j, p: (i, j, 0)),
                pl.BlockSpec((1, 1, 1, D_MODEL),
                             lambda i, j, p: (i, jnp.where(j >= n_lat_tiles, 1, 0), 0, 0)),
                pl.BlockSpec(memory_space=pl.ANY),
            ],
            out_specs=pl.BlockSpec((1, MOE_TILE, D_MODEL), lambda i, j, p: (i, j, 0)),
            scratch_shapes=[pltpu.VMEM((2, MOE_TILE, D_MODEL), F32),
                            pltpu.SemaphoreType.DMA((2,))],
        ),
        out_shape=jax.ShapeDtypeStruct((b, n, D_MODEL), F32),
        compiler_params=pltpu.CompilerParams(
            dimension_semantics=("arbitrary", "arbitrary"), vmem_limit_bytes=VMEM_LIMIT),
        name="moe_gather_residual",
    )(pos, xn, gt2[:, :, None, :], ys)


def _pad_heads(w, n_heads, dim, width=LANES):
    lead = w.shape[:-1]
    w = w.reshape(lead + (n_heads, dim))
    w = jnp.pad(w, [(0, 0)] * len(lead) + [(0, 0), (0, width - dim)])
    return w.reshape(lead + (n_heads * width,))


def _partner_lanes(lane0, half_dim):
    perm = np.arange(LANES)
    quarter = half_dim // 2
    for part in range(2):
        lo = lane0 + part * half_dim
        perm[lo:lo + quarter] = np.arange(lo + quarter, lo + half_dim)
        perm[lo + quarter:lo + half_dim] = np.arange(lo, lo + quarter)
    return perm


PARTNER_MLA = _partner_lanes(MLA_NOPE, MLA_ROPE // 2)
PARTNER_GQA = _partner_lanes(0, GQA_HD // 2)


def _swap_partners(w, perm):
    lead = w.shape[:-1]
    return w.reshape(lead + (-1, LANES))[..., perm].reshape(w.shape)


def _prep_w_in(w_in):
    offs = np.concatenate([[0], np.cumsum(IN_SIZES)])
    cq, ckv, kr, qb, kb, vb, gts = [w_in[..., int(offs[i]):int(offs[i + 1])] for i in range(7)]
    kr_blk = jnp.pad(kr, [(0, 0), (0, 0), (MLA_NOPE, LANES - MLA_QK)])
    qb_p, kb_p = _pad_heads(qb, GQA_HEADS, GQA_HD), _pad_heads(kb, GQA_KV_HEADS, GQA_HD)
    w_main = jnp.concatenate(
        [cq, ckv, kr_blk, _swap_partners(kr_blk, PARTNER_MLA), qb_p, _swap_partners(qb_p, PARTNER_GQA),
         kb_p, _swap_partners(kb_p, PARTNER_GQA), gts], axis=-1).astype(BF16)
    w_vbt = jnp.swapaxes(_pad_heads(vb, GQA_KV_HEADS, GQA_HD, V_ROWS), 1, 2).astype(BF16)
    return w_main, w_vbt


def _prep_w_ukv(w_ukv):
    l, r, _ = w_ukv.shape
    w = w_ukv.reshape(l, r, MLA_HEADS, MLA_NOPE + MLA_V)
    uk = _pad_heads(w[..., :MLA_NOPE].reshape(l, r, MLA_HEADS * MLA_NOPE), MLA_HEADS, MLA_NOPE)
    uv = _pad_heads(w[..., MLA_NOPE:].reshape(l, r, MLA_HEADS * MLA_V), MLA_HEADS, MLA_V, V_ROWS)
    return uk.astype(BF16), jnp.swapaxes(uv, 1, 2).astype(BF16)


def _pad_gain(g, dim):
    return jnp.pad(g, [(0, 0), (0, LANES - dim)])


def _gain_rows(g_qa, g_ka, g_qb, g_kb):
    rows = []
    for g, dim, perm in ((g_qa, MLA_QK, PARTNER_MLA), (g_ka, MLA_QK, PARTNER_MLA),
                         (g_qb, GQA_HD, PARTNER_GQA), (g_kb, GQA_HD, PARTNER_GQA)):
        g_p = _pad_gain(g, dim)
        rows += [g_p, g_p[:, perm]]
    return jnp.stack(rows, axis=1)


def _rope_tables(n_ctx, n_lat, lane0, half_dim):
    pos = jnp.arange(n_lat)
    inv = ROPE_THETA ** (-jnp.arange(0, half_dim, 2, dtype=F32) / half_dim)
    ang_r = (pos // GRID_W).astype(F32)[:, None] * inv[None, :]
    ang_c = (pos % GRID_W).astype(F32)[:, None] * inv[None, :]
    cos = jnp.concatenate([jnp.cos(ang_r)] * 2 + [jnp.cos(ang_c)] * 2, axis=-1)
    sin = jnp.concatenate([-jnp.sin(ang_r), jnp.sin(ang_r), -jnp.sin(ang_c), jnp.sin(ang_c)], axis=-1)
    pad = [(0, n_ctx), (lane0, LANES - lane0 - 2 * half_dim)]
    return (jnp.pad(cos, pad, constant_values=1.0), jnp.pad(sin, pad))


def kernel(x, c, ctx, c_ctx, w_mod, b_mod, g_attn, g_ffn, w_in, g_cq, w_uq, g_ckv, w_ukv,
           g_qa, g_ka, g_qb, g_kb, w_oa, w_ob, w_out, w_router, b_router,
           w_e_gate, w_e_up, w_e_down, w_s_gate, w_s_up, w_s_down):
    b, n_lat, d = x.shape
    n_ctx = ctx.shape[1]
    n = n_lat + n_ctx
    depth = w_mod.shape[0]
    assert d == D_MODEL and n_ctx == ROW_TILE and n_lat % KV_CHUNK == 0 and n_lat % Q_TILE == 0

    x_lat, x_ctx, ctx_tile = x, ctx, 0
    cs =jnp.concatenate([c, c_ctx[None, :], jnp.zeros((8 - b - 1, d), F32)], axis=0)

    w_in_p, w_vbt = _prep_w_in(w_in)
    w_uq_p = _pad_heads(w_uq, MLA_HEADS, MLA_QK)
    w_uq_p = jnp.concatenate([w_uq_p, _swap_partners(w_uq_p, PARTNER_MLA)], axis=-1).astype(BF16)
    w_uk_p, w_uvt = _prep_w_ukv(w_ukv)
    g3 = lambda g: g[:, None, :]
    gains = _gain_rows(g_qa, g_ka, g_qb, g_kb)
    tables = (_rope_tables(n_ctx, n_lat, MLA_NOPE, MLA_ROPE // 2)
              + _rope_tables(n_ctx, n_lat, 0, GQA_HD // 2))
    w_oa_b, w_ob_b, w_out_b = w_oa.astype(BF16), w_ob.astype(BF16), w_out.astype(BF16)
    wr_p = jnp.pad(w_router, [(0, 0), (0, LANES - N_EXPERTS)])
    wr_hi = wr_p.astype(BF16)
    wr_lo = (wr_p - wr_hi.astype(F32)).astype(BF16)
    br_col = jnp.pad(b_router, (0, LANES - N_EXPERTS))[:, None]
    w_shared = (w_s_gate, w_s_up, w_s_down)
    w_routed = (w_e_gate, w_e_up, w_e_down)
    b_mod3 = b_mod[:, None, :]

    for layer in range(depth):
        last = layer == depth - 1
        n_rows = n_lat if last else n
        mods = _mod_call(cs, w_mod, b_mod3, layer).reshape(8, 6, d)
        lat, cx = mods[:b], jnp.broadcast_to(mods[b:b + 1], (b, 6, d))
        both = jnp.stack([lat, cx], axis=1)
        mod1, modm, gt2 = both[:, :, 0:2], both[:, :, 2:5], both[:, :, 5]
        q_all, k_all, v_all, gts = _in_call(
            x_lat, x_ctx, ctx_tile, mod1, layer, n, n_lat, g3(g_attn), w_in_p, g3(g_cq), w_uq_p,
            g3(g_ckv), w_uk_p, w_uvt, w_vbt, gains, tables)
        o_lat = _attn_lat_call(q_all, k_all, v_all, n_lat)
        o_ctx = o_lat if last else _attn_ctx_call(q_all, k_all, v_all, n_lat)
        xn, fext, counts = _mid_call(o_lat, o_ctx, gts, x_lat, x_ctx, ctx_tile, modm, layer, n_rows,
                                     n_lat, w_oa_b, w_ob_b, w_out_b, g3(g_ffn), wr_hi, wr_lo, br_col)
        n_tok = b * n_rows
        n_tiles_max = n_tok // MOE_TILE + N_PAIR_CLASSES
        fext = fext.reshape(n_tok, F_EXT)
        plan = _route_plan(fext[:, D_MODEL:D_MODEL + 4], counts, n_tiles_max)
        ys = _moe_call(plan, fext, layer, w_shared, w_routed)
        x_lat = _gather_call(plan[0], xn, gt2, ys, n_lat)
        x_ctx, ctx_tile = x_lat, n_lat // ROW_TILE
    return x_lat
```

```python
import functools

import numpy as np
import jax
import jax.numpy as jnp
from jax import lax
from jax.experimental import pallas as pl
from jax.experimental.pallas import tpu as pltpu

D_MODEL = 1024
GRID_W = 64
MLA_HEADS = 8
MLA_NOPE = 64
MLA_ROPE = 32
MLA_QK = MLA_NOPE + MLA_ROPE
MLA_V = 64
Q_LORA = 256
KV_LORA = 128
GQA_HEADS = 8
GQA_KV_HEADS = 2
GQA_GROUP = GQA_HEADS // GQA_KV_HEADS
GQA_HD = 64
N_EXPERTS = 16
N_GROUPS = 4
EXPERTS_PER_GROUP = N_EXPERTS // N_GROUPS
FF_EXPERT = 512
ROPE_THETA = 10000.0
EPS = 1e-6
IN_SIZES = (Q_LORA, KV_LORA, MLA_ROPE, GQA_HEADS * GQA_HD, GQA_KV_HEADS * GQA_HD,
            GQA_KV_HEADS * GQA_HD, 2 * D_MODEL)

LANES = 128
ROW_TILE = 256
Q_TILE = 1024
KV_CHUNK = 512
N_HEADS_ALL = MLA_HEADS + GQA_HEADS
N_KHEADS_ALL = MLA_HEADS + GQA_KV_HEADS
LOG2_E = 1.4426950408889634
ONES_ROW = MLA_V
V_ROWS = MLA_V + 16
VMEM_LIMIT = 56 * 1024 * 1024
MOE_TILE = 256
MOE_AHEAD = 3
N_CLASSES = N_GROUPS << EXPERTS_PER_GROUP
N_PAIR_CLASSES = N_GROUPS * 6
F_EXT = D_MODEL + LANES
META_CLS, META_RANK, META_W_LO, META_W_HI = 0, 1, 2, 3

C_CQ = 0
C_CKV = C_CQ + Q_LORA
C_KR = C_CKV + KV_LORA
C_QB = C_KR + 2 * LANES
C_KB = C_QB + 2 * GQA_HEADS * LANES
C_G = C_KB + 2 * GQA_KV_HEADS * LANES
C_END = C_G + 2 * D_MODEL

BF16 = jnp.bfloat16
F32 = jnp.float32


def _dot(a, b):
    return jnp.dot(a, b, preferred_element_type=F32)


def _dot_nt(a, b):
    return lax.dot_general(a, b, (((1,), (1,)), ((), ())), preferred_element_type=F32)


def _split_bf16(x):
    hi = x.astype(BF16)
    lo = (x - hi.astype(F32)).astype(BF16)
    return hi, lo


def _rms_rows(x, g):
    ms = jnp.mean(x * x, axis=-1, keepdims=True)
    return x * lax.rsqrt(ms + EPS) * g


def _norm_rope(raw, swapped, gain_cos, gain_sin, dim):
    ms = jnp.sum(raw * raw, axis=-1, keepdims=True) * (1.0 / dim)
    return (raw * gain_cos + swapped * gain_sin) * lax.rsqrt(ms + EPS)


def _mod_kernel(c_ref, w_ref, b_ref, o_ref):
    c = c_ref[...]
    s_hi, s_lo = _split_bf16(c * jax.nn.sigmoid(c))
    w_hi, w_lo = _split_bf16(w_ref[0])
    o_ref[...] = _dot(s_hi, w_hi) + _dot(s_lo, w_hi) + _dot(s_hi, w_lo) + b_ref[0]


def _mod_call(cs, w_mod, b_mod, layer):
    n_cols = w_mod.shape[2]
    tn = 1536
    return pl.pallas_call(
        _mod_kernel,
        grid=(n_cols // tn,),
        in_specs=[
            pl.BlockSpec(cs.shape, lambda j: (0, 0)),
            pl.BlockSpec((1, D_MODEL, tn), lambda j: (layer, 0, j)),
            pl.BlockSpec((1, 1, tn), lambda j: (layer, 0, j)),
        ],
        out_specs=pl.BlockSpec((cs.shape[0], tn), lambda j: (0, j)),
        out_shape=jax.ShapeDtypeStruct((cs.shape[0], n_cols), F32),
        compiler_params=pltpu.CompilerParams(
            dimension_semantics=("arbitrary",), vmem_limit_bytes=VMEM_LIMIT),
        name="mod",
    )(cs, w_mod, b_mod)


def _in_kernel(x_ref, xc_ref, mod_ref, gattn_ref, win_ref, gcq_ref, wuq_ref, gckv_ref, wuk_ref,
               wuvt_ref, wvbt_ref, gains_ref, cosa_ref, sina_ref, cosb_ref, sinb_ref,
               q_ref, k_ref, v_ref, g_ref, *, n_lat_tiles):
    x = jnp.where(pl.program_id(1) >= n_lat_tiles, xc_ref[0], x_ref[0])
    shift = mod_ref[0, 0, 0:1, :]
    scale = mod_ref[0, 0, 1:2, :]
    h = (_rms_rows(x, gattn_ref[0]) * (1 + scale) + shift).astype(BF16)

    g_ref[0] = _dot(h, win_ref[0, :, C_G:C_END])

    gains = gains_ref[0]
    cos_a, sin_a, cos_b, sin_b = cosa_ref[...], sina_ref[...], cosb_ref[...], sinb_ref[...]
    scale_a = MLA_QK ** -0.5 * LOG2_E
    scale_b = GQA_HD ** -0.5 * LOG2_E
    qa_cos, qa_sin = cos_a * (gains[0:1] * scale_a), sin_a * (gains[1:2] * scale_a)
    ka_cos, ka_sin = cos_a * gains[2:3], sin_a * gains[3:4]
    qb_cos, qb_sin = cos_b * (gains[4:5] * scale_b), sin_b * (gains[5:6] * scale_b)
    kb_cos, kb_sin = cos_b * gains[6:7], sin_b * gains[7:8]
    t = x.shape[0]
    ones_row = lax.broadcasted_iota(jnp.int32, (V_ROWS, t), 0) == ONES_ROW
    blk = lambda a, i: a[:, i * LANES:(i + 1) * LANES]

    p0 = _dot(h, win_ref[0, :, C_CQ:C_QB])
    cq = p0[:, C_CQ:C_CKV]
    ckv = p0[:, C_CKV:C_KR]
    kr_blk = p0[:, C_KR:C_KR + LANES]
    kr_swapped = p0[:, C_KR + LANES:C_QB]
    qa = _dot(_rms_rows(cq, gcq_ref[0]).astype(BF16), wuq_ref[0])
    ckv_n = _rms_rows(ckv, gckv_ref[0]).astype(BF16)
    ka = _dot(ckv_n, wuk_ref[0])
    va_t = _dot_nt(wuvt_ref[0], ckv_n)
    for hd in range(MLA_HEADS):
        q_ref[0, hd] = _norm_rope(blk(qa, hd), blk(qa, MLA_HEADS + hd), qa_cos, qa_sin,
                                  MLA_QK).astype(BF16)
        k_ref[0, hd] = _norm_rope(blk(ka, hd) + kr_blk, kr_swapped, ka_cos, ka_sin,
                                  MLA_QK).astype(BF16)
        v_ref[0, hd, 0] = jnp.where(ones_row, 1.0, va_t[hd * V_ROWS:(hd + 1) * V_ROWS, :]).astype(BF16)

    pq = _dot(h, win_ref[0, :, C_QB:C_KB])
    for hd in range(GQA_HEADS):
        q_ref[0, MLA_HEADS + hd] = _norm_rope(blk(pq, hd), blk(pq, GQA_HEADS + hd), qb_cos, qb_sin,
                                              GQA_HD).astype(BF16)
    pk = _dot(h, win_ref[0, :, C_KB:C_G])
    vb_t = _dot_nt(wvbt_ref[0], h)
    for hd in range(GQA_KV_HEADS):
        k_ref[0, MLA_HEADS + hd] = _norm_rope(blk(pk, hd), blk(pk, GQA_KV_HEADS + hd), kb_cos,
                                              kb_sin, GQA_HD).astype(BF16)
        v_ref[0, MLA_HEADS + hd, 0] = jnp.where(
            ones_row, 1.0, vb_t[hd * V_ROWS:(hd + 1) * V_ROWS, :]).astype(BF16)


def _stream_specs(n_lat_tiles, ctx_tile):
    t = ROW_TILE
    return [pl.BlockSpec((1, t, D_MODEL), lambda i, j: (i, jnp.minimum(j, n_lat_tiles - 1), 0)),
            pl.BlockSpec((1, t, D_MODEL), lambda i, j: (i, ctx_tile, 0))]


def _in_call(x_lat, x_ctx, ctx_tile, mod1, layer, n, n_lat, g_attn, w_in_p, g_cq, w_uq_p, g_ckv,
             w_uk_p, w_uvt, w_vbt, gains, tables):
    b = x_lat.shape[0]
    nt = n // ROW_TILE
    n_lat_tiles = n_lat // ROW_TILE
    t = ROW_TILE

    def lspec(arr):
        shp = arr.shape
        return pl.BlockSpec((1,) + shp[1:], lambda i, j: (layer,) + (0,) * (len(shp) - 1))

    tab_spec = pl.BlockSpec((t, LANES), lambda i, j: (j, 0))
    return pl.pallas_call(
        functools.partial(_in_kernel, n_lat_tiles=n_lat_tiles),
        grid=(b, nt),
        in_specs=_stream_specs(n_lat_tiles, ctx_tile) + [
            pl.BlockSpec((1, 1, 2, D_MODEL), lambda i, j: (i, jnp.where(j >= n_lat_tiles, 1, 0), 0, 0)),
            lspec(g_attn), lspec(w_in_p), lspec(g_cq), lspec(w_uq_p), lspec(g_ckv),
            lspec(w_uk_p), lspec(w_uvt), lspec(w_vbt), lspec(gains),
        ] + [tab_spec] * 4,
        out_specs=[
            pl.BlockSpec((1, N_HEADS_ALL, t, LANES), lambda i, j: (i, 0, j, 0)),
            pl.BlockSpec((1, N_KHEADS_ALL, t, LANES), lambda i, j: (i, 0, j, 0)),
            pl.BlockSpec((1, N_KHEADS_ALL, 1, V_ROWS, t), lambda i, j: (i, 0, j, 0, 0)),
            pl.BlockSpec((1, t, 2 * D_MODEL), lambda i, j: (i, j, 0)),
        ],
        out_shape=[
            jax.ShapeDtypeStruct((b, N_HEADS_ALL, n, LANES), BF16),
            jax.ShapeDtypeStruct((b, N_KHEADS_ALL, n, LANES), BF16),
            jax.ShapeDtypeStruct((b, N_KHEADS_ALL, nt, V_ROWS, t), BF16),
            jax.ShapeDtypeStruct((b, n, 2 * D_MODEL), F32),
        ],
        compiler_params=pltpu.CompilerParams(
            dimension_semantics=("arbitrary", "arbitrary"), vmem_limit_bytes=VMEM_LIMIT),
        name="mixer_in",
    )(x_lat, x_ctx, mod1, g_attn, w_in_p, g_cq, w_uq_p, g_ckv, w_uk_p, w_uvt, w_vbt, gains, *tables)


def _attn_kernel(q_ref, k0_ref, k1_ref, v0_ref, v1_ref, *rest, n_main, tail_chunk, pipelined):
    if pipelined:
        o_ref, s_scr, acc_scr = rest[-3:]
    else:
        o_ref, acc_scr = rest[-2:]
    qs = (q_ref[0, 0], q_ref[0, 1])
    k_refs = (k0_ref, k1_ref)
    v_refs = (v0_ref, v1_ref)
    tq = qs[0].shape[0]
    per = KV_CHUNK // ROW_TILE

    def fold(hh, m, cmax, st, chunk0, n_sub, first=False):
        m_new = cmax if first else jnp.maximum(m, cmax)
        pt = jnp.exp2(st - m_new).astype(BF16)
        pv = _dot(v_refs[hh][0, 0, chunk0], pt[0:ROW_TILE])
        for u in range(1, n_sub):
            pv = pv + _dot(v_refs[hh][0, 0, chunk0 + u], pt[u * ROW_TILE:(u + 1) * ROW_TILE])
        if first:
            acc_scr[hh] = pv
        else:
            acc_scr[hh] = jnp.exp2(m - m_new) * acc_scr[hh] + pv
        return m_new

    def produce(c, slot):
        cmax = []
        for hh in range(2):
            off = c * KV_CHUNK
            if not isinstance(off, int):
                off = pl.multiple_of(off, KV_CHUNK)
            st = _dot_nt(k_refs[hh][0, 0, pl.ds(off, KV_CHUNK), :], qs[hh])
            s_scr[slot, hh] = st
            cmax.append(jnp.max(st, axis=0, keepdims=True))
        return tuple(cmax)

    def consume(c, slot, ms, cmax):
        return tuple(fold(hh, ms[hh], cmax[hh], s_scr[slot, hh], c * per, per) for hh in range(2))

    ms = []
    for hh in range(2):
        st = _dot_nt(k_refs[hh][0, 0, pl.ds(tail_chunk * ROW_TILE, ROW_TILE), :], qs[hh])
        ms.append(fold(hh, None, jnp.max(st, axis=0, keepdims=True), st, tail_chunk, 1, first=True))
    ms = tuple(ms)

    if pipelined:
        def body(i, carry):
            ms, cmax = carry
            cmax1 = produce(2 * i + 1, 1)
            ms = consume(2 * i, 0, ms, cmax)
            cmax2 = produce(2 * i + 2, 0)
            ms = consume(2 * i + 1, 1, ms, cmax1)
            return ms, cmax2

        ms, cmax = lax.fori_loop(0, (n_main - 2) // 2, body, (ms, produce(0, 0)))
        cmax1 = produce(n_main - 1, 1)
        ms = consume(n_main - 2, 0, ms, cmax)
        ms = consume(n_main - 1, 1, ms, cmax1)

    outs = []
    for hh in range(2):
        acc = acc_scr[hh]
        o_t = jnp.concatenate([acc / acc[ONES_ROW:ONES_ROW + 1, :],
                               jnp.zeros((LANES - V_ROWS, tq), F32)], axis=0)
        outs.append(o_t.T)
    low = lax.broadcasted_iota(jnp.int32, (tq, LANES), 1) < MLA_V
    o_ref[0] = jnp.where(low, outs[0], pltpu.roll(outs[1], MLA_V, 1)).astype(o_ref.dtype)


def _kv_head(p):
    mla_pairs = MLA_HEADS // 2
    pairs_per_kv = GQA_GROUP // 2
    gqa = MLA_HEADS + (p - mla_pairs) // pairs_per_kv
    return jnp.where(p < mla_pairs, 2 * p, gqa), jnp.where(p < mla_pairs, 2 * p + 1, gqa)


def _attn_lat_call(q_all, k_all, v_all, n_lat):
    b, _, n, _ = q_all.shape
    n_pairs = N_HEADS_ALL // 2
    n_chunks = n // ROW_TILE
    n_main = n_lat // KV_CHUNK
    assert n_main >= 2 and n_main % 2 == 0
    kern = functools.partial(_attn_kernel, n_main=n_main, tail_chunk=n_lat // ROW_TILE,
                             pipelined=True)
    k_spec = lambda which: pl.BlockSpec(
        (1, 1, n, LANES), lambda i, p, j: (i, _kv_head(p)[which], 0, 0))
    v_spec = lambda which: pl.BlockSpec(
        (1, 1, n_chunks, V_ROWS, ROW_TILE), lambda i, p, j: (i, _kv_head(p)[which], 0, 0, 0))
    return pl.pallas_call(
        kern,
        grid=(b, n_pairs, n_lat // Q_TILE),
        in_specs=[pl.BlockSpec((1, 2, Q_TILE, LANES), lambda i, p, j: (i, p, j, 0)),
                  k_spec(0), k_spec(1), v_spec(0), v_spec(1)],
        out_specs=pl.BlockSpec((1, Q_TILE, LANES), lambda i, p, j: (i, j, p)),
        out_shape=jax.ShapeDtypeStruct((b, n_lat, n_pairs * LANES), BF16),
        scratch_shapes=[pltpu.VMEM((2, 2, KV_CHUNK, Q_TILE), F32),
                        pltpu.VMEM((2, V_ROWS, Q_TILE), F32)],
        compiler_params=pltpu.CompilerParams(
            dimension_semantics=("arbitrary", "arbitrary", "arbitrary"),
            vmem_limit_bytes=VMEM_LIMIT),
        name="attention",
    )(q_all, k_all, k_all, v_all, v_all)


def _attn_ctx_call(q_all, k_all, v_all, n_lat):
    b, _, n, _ = q_all.shape
    n_pairs = N_HEADS_ALL // 2
    c = n_lat // ROW_TILE
    assert n - n_lat == ROW_TILE
    kern = functools.partial(_attn_kernel, n_main=0, tail_chunk=0, pipelined=False)
    k_spec = lambda which: pl.BlockSpec(
        (1, 1, ROW_TILE, LANES), lambda i, p: (i, _kv_head(p)[which], c, 0))
    v_spec = lambda which: pl.BlockSpec(
        (1, 1, 1, V_ROWS, ROW_TILE), lambda i, p: (i, _kv_head(p)[which], c, 0, 0))
    return pl.pallas_call(
        kern,
        grid=(b, n_pairs),
        in_specs=[pl.BlockSpec((1, 2, ROW_TILE, LANES), lambda i, p: (i, p, c, 0)),
                  k_spec(0), k_spec(1), v_spec(0), v_spec(1)],
        out_specs=pl.BlockSpec((1, ROW_TILE, LANES), lambda i, p: (i, 0, p)),
        out_shape=jax.ShapeDtypeStruct((b, ROW_TILE, n_pairs * LANES), BF16),
        scratch_shapes=[pltpu.VMEM((2, V_ROWS, ROW_TILE), F32)],
        compiler_params=pltpu.CompilerParams(
            dimension_semantics=("arbitrary", "arbitrary"), vmem_limit_bytes=VMEM_LIMIT),
        name="attention_ctx",
    )(q_all, k_all, k_all, v_all, v_all)


def _route_rows(sel, scores):
    rows = [sel[e:e + 1, :] for e in range(N_EXPERTS)]
    srow = [scores[e:e + 1, :] for e in range(N_EXPERTS)]
    grp = []
    for g in range(N_GROUPS):
        a, b, c, d = rows[4 * g:4 * g + 4]
        hi1, lo1 = jnp.maximum(a, b), jnp.minimum(a, b)
        hi2, lo2 = jnp.maximum(c, d), jnp.minimum(c, d)
        top1 = jnp.maximum(hi1, hi2)
        top2 = jnp.maximum(jnp.minimum(hi1, hi2), jnp.maximum(lo1, lo2))
        grp.append(top1 + top2)
    best = jnp.zeros_like(grp[0], dtype=jnp.int32)
    best_v = grp[0]
    for g in range(1, N_GROUPS):
        upd = grp[g] > best_v
        best = jnp.where(upd, g, best)
        best_v = jnp.where(upd, grp[g], best_v)
    picked = []
    keeps = []
    for e in range(N_EXPERTS):
        g = e // EXPERTS_PER_GROUP
        rank = jnp.zeros_like(best)
        for o in range(4 * g, 4 * g + 4):
            if o == e:
                continue
            ahead = rows[o] > rows[e]
            if o < e:
                ahead = ahead | (rows[o] == rows[e])
            rank = rank + ahead.astype(jnp.int32)
        keep = (rank < 2) & (best == g)
        picked.append(jnp.where(keep, srow[e], 0.0))
        keeps.append(keep)
    total = picked[0]
    for e in range(1, N_EXPERTS):
        total = total + picked[e]
    return [p / total for p in picked], best, keeps


def _route_meta(gates, best, keeps):
    mask = jnp.zeros_like(best)
    w_lo = jnp.zeros_like(gates[0])
    w_hi = jnp.zeros_like(gates[0])
    for e in range(N_EXPERTS):
        g, i = divmod(e, EXPERTS_PER_GROUP)
        mask = mask + jnp.where(keeps[e], 1 << i, 0)
        if i == 0:
            w_lo = w_lo + jnp.where(keeps[e], gates[e], 0.0)
            continue
        below = keeps[4 * g]
        for o in range(4 * g + 1, e):
            below = below | keeps[o]
        w_lo = w_lo + jnp.where(keeps[e] & ~below, gates[e], 0.0)
        w_hi = w_hi + jnp.where(keeps[e] & below, gates[e], 0.0)
    cls = (best * (1 << EXPERTS_PER_GROUP) + mask).astype(F32)
    return cls, w_lo, w_hi


def _mid_kernel(o_ref, octx_ref, g_ref, x_ref, xc_ref, mod_ref, woa_ref, wob_ref, wout_ref, gffn_ref,
                wrh_ref, wrl_ref, brc_ref, xn_ref, fext_ref, cnt_ref, run_ref, *, n_lat_tiles):
    is_ctx = pl.program_id(1) >= n_lat_tiles
    o = jnp.where(is_ctx, octx_ref[0], o_ref[0])
    n_a = MLA_HEADS * MLA_V
    ya = _dot(o[:, 0:n_a], woa_ref[0])
    yb = _dot(o[:, n_a:], wob_ref[0])
    g = g_ref[0]
    y = jax.nn.sigmoid(g[:, 0:D_MODEL]) * ya + jax.nn.sigmoid(g[:, D_MODEL:]) * yb
    z = _dot(y.astype(BF16), wout_ref[0])
    gt1 = mod_ref[0, 0, 0:1, :]
    sh2 = mod_ref[0, 0, 1:2, :]
    sc2 = mod_ref[0, 0, 2:3, :]
    xn = jnp.where(is_ctx, xc_ref[0], x_ref[0]) + gt1 * z
    xn_ref[0] = xn
    f = _rms_rows(xn, gffn_ref[0]) * (1 + sc2) + sh2
    fext_ref[0, :, 0:D_MODEL] = f
    f_hi, f_lo = _split_bf16(f)
    w_hi = wrh_ref[...]
    logits = _dot(f_hi, w_hi) + _dot(f_lo, w_hi) + _dot(f_hi, wrl_ref[...])
    scores_t = jax.nn.sigmoid(logits).T
    scores = scores_t[0:N_EXPERTS, :]
    sel = scores + brc_ref[0:N_EXPERTS, :]
    cls, w_lo, w_hi_gate = _route_meta(*_route_rows(sel, scores))
    t = scores.shape[1]
    row_id = lax.broadcasted_iota(jnp.int32, (LANES, t), 0)
    meta_t = jnp.where(row_id == META_CLS, cls, 0.0)
    meta_t = jnp.where(row_id == META_W_LO, w_lo, meta_t)
    meta_t = jnp.where(row_id == META_W_HI, w_hi_gate, meta_t)
    meta = meta_t.T

    @pl.when((pl.program_id(0) == 0) & (pl.program_id(1) == 0))
    def _():
        run_ref[...] = jnp.zeros_like(run_ref)

    lane = lax.broadcasted_iota(jnp.int32, (t, LANES), 1)
    onehot = jnp.where(lane.astype(F32) == meta[:, META_CLS:META_CLS + 1], 1.0, 0.0)
    earlier = jnp.where(lax.broadcasted_iota(jnp.int32, (t, t), 0)
                        > lax.broadcasted_iota(jnp.int32, (t, t), 1), 1.0, 0.0).astype(BF16)
    before = _dot(earlier, onehot.astype(BF16)) + run_ref[...]
    rank = jnp.sum(before * onehot, axis=-1, keepdims=True)
    fext_ref[0, :, D_MODEL:] = jnp.where(lane == META_RANK, rank, meta)
    run_ref[...] += jnp.sum(onehot, axis=0, keepdims=True)
    cnt_ref[...] = jnp.broadcast_to(run_ref[...], cnt_ref.shape)


def _mid_call(o_lat, o_ctx, gts, x_lat, x_ctx, ctx_tile, modm, layer, n_rows, n_lat, w_oa, w_ob, w_out,
              g_ffn, wr_hi, wr_lo, br_col):
    b = x_lat.shape[0]
    n = n_rows
    n_lat_tiles = n_lat // ROW_TILE
    t = ROW_TILE

    def lspec(arr):
        shp = arr.shape
        return pl.BlockSpec((1,) + shp[1:], lambda i, j: (layer,) + (0,) * (len(shp) - 1))

    def full(arr):
        return pl.BlockSpec(arr.shape, lambda i, j: (0,) * arr.ndim)

    return pl.pallas_call(
        functools.partial(_mid_kernel, n_lat_tiles=n_lat_tiles),
        grid=(b, n // t),
        in_specs=[
            pl.BlockSpec((1, t, o_lat.shape[2]), lambda i, j: (i, jnp.minimum(j, n_lat_tiles - 1), 0)),
            pl.BlockSpec((1, t, o_ctx.shape[2]), lambda i, j: (i, 0, 0)),
            pl.BlockSpec((1, t, 2 * D_MODEL), lambda i, j: (i, j, 0)),
        ] + _stream_specs(n_lat_tiles, ctx_tile) + [
            pl.BlockSpec((1, 1, 3, D_MODEL), lambda i, j: (i, jnp.where(j >= n_lat_tiles, 1, 0), 0, 0)),
            lspec(w_oa), lspec(w_ob), lspec(w_out), lspec(g_ffn),
            full(wr_hi), full(wr_lo), full(br_col),
        ],
        out_specs=[
            pl.BlockSpec((1, t, D_MODEL), lambda i, j: (i, j, 0)),
            pl.BlockSpec((1, t, F_EXT), lambda i, j: (i, j, 0)),
            pl.BlockSpec((8, LANES), lambda i, j: (0, 0)),
        ],
        out_shape=[
            jax.ShapeDtypeStruct((b, n, D_MODEL), F32),
            jax.ShapeDtypeStruct((b, n, F_EXT), F32),
            jax.ShapeDtypeStruct((8, LANES), F32),
        ],
        scratch_shapes=[pltpu.VMEM((1, LANES), F32)],
        compiler_params=pltpu.CompilerParams(
            dimension_semantics=("arbitrary", "arbitrary"), vmem_limit_bytes=VMEM_LIMIT),
        name="mixer_out_router",
    )(o_lat, o_ctx, gts, x_lat, x_ctx, modm, w_oa, w_ob, w_out, g_ffn, wr_hi, wr_lo, br_col)


def _class_expert_tables():
    lo = np.zeros((N_CLASSES,), np.int32)
    hi = np.zeros((N_CLASSES,), np.int32)
    for c in range(N_CLASSES):
        g, mask = divmod(c, 1 << EXPERTS_PER_GROUP)
        bits = [i for i in range(EXPERTS_PER_GROUP) if mask >> i & 1]
        if len(bits) == 2:
            lo[c], hi[c] = EXPERTS_PER_GROUP * g + bits[0], EXPERTS_PER_GROUP * g + bits[1]
    return lo, hi


def _route_plan(meta, counts, n_tiles_max):
    cls = meta[:, META_CLS].astype(jnp.int32)
    rank = meta[:, META_RANK].astype(jnp.int32)
    cnt = counts[0, :N_CLASSES].astype(jnp.int32)
    tiles = (cnt + MOE_TILE - 1) // MOE_TILE
    tile_end = jnp.cumsum(tiles)
    start = (tile_end - tiles) * MOE_TILE
    hit = cls[None, :] == jnp.arange(N_CLASSES, dtype=jnp.int32)[:, None]
    pos = jnp.sum(jnp.where(hit, start[:, None], 0), axis=0) + rank
    n_used = tile_end[-1]
    tile_idx = jnp.minimum(jnp.arange(n_tiles_max, dtype=jnp.int32), n_used - 1)
    tile_cls = jnp.sum(tile_end[None, :] <= tile_idx[:, None], axis=1).astype(jnp.int32)
    lo, hi = _class_expert_tables()
    src = jnp.zeros((n_tiles_max * MOE_TILE,), jnp.int32).at[pos].set(
        jnp.arange(pos.shape[0], dtype=jnp.int32))
    return (pos.astype(jnp.int32), src, tile_idx, jnp.asarray(lo)[tile_cls],
            jnp.asarray(hi)[tile_cls], n_used.reshape(1).astype(jnp.int32))


def _row_copy(src_ref, src_row, dst_ref, dst_row, sem):
    return pltpu.make_async_copy(src_ref.at[pl.ds(src_row, 1)], dst_ref.at[pl.ds(dst_row, 1)], sem)


def _moe_kernel(src_ref, tidx_ref, elo_ref, ehi_ref, nused_ref, f_hbm, wgs_ref, wus_ref, wds_ref,
                wg1_ref, wu1_ref, wd1_ref, wg2_ref, wu2_ref, wd2_ref, ys_ref, buf, sems):
    del elo_ref, ehi_ref
    j = pl.program_id(0)
    n_used = nused_ref[0]
    slot = j % (MOE_AHEAD + 1)

    def fetch(step):
        into = step % (MOE_AHEAD + 1)
        base = tidx_ref[step] * MOE_TILE
        for r in range(MOE_TILE):
            _row_copy(f_hbm, src_ref[base + r], buf.at[into], r, sems.at[into]).start()

    for first in range(MOE_AHEAD):
        pl.when((j == 0) & (first < n_used))(functools.partial(fetch, first))
    pl.when(j + MOE_AHEAD < n_used)(lambda: fetch(j + MOE_AHEAD))
    used = j < n_used

    @pl.when(jnp.logical_not(used))
    def _():
        ys_ref[...] = jnp.zeros_like(ys_ref)

    @pl.when(used)
    def _():
        for _ in range(MOE_TILE):
            _row_copy(f_hbm, 0, buf.at[slot], 0, sems.at[slot]).wait()
        rows = buf[slot]
        x = rows[:, 0:D_MODEL].astype(BF16)
        w_lo = rows[:, D_MODEL + META_W_LO:D_MODEL + META_W_LO + 1]
        w_hi = rows[:, D_MODEL + META_W_HI:D_MODEL + META_W_HI + 1]

        def ffn(wg_ref, wu_ref, wd_ref):
            wg, wu, wd = (r[(0,) * (len(r.shape) - 2)].astype(BF16) for r in (wg_ref, wu_ref, wd_ref))
            a = jax.nn.silu(_dot(x, wg)) * _dot(x, wu)
            return _dot(a.astype(BF16), wd)

        ys_ref[...] = (ffn(wgs_ref, wus_ref, wds_ref) + w_lo * ffn(wg1_ref, wu1_ref, wd1_ref)
                       + w_hi * ffn(wg2_ref, wu2_ref, wd2_ref))


def _moe_call(plan, fext, layer, w_s, w_e):
    _, src, tile_idx, e_lo, e_hi, n_used = plan
    n_tiles = tile_idx.shape[0]

    def routed_spec(w, which):
        return pl.BlockSpec((1, 1) + w.shape[2:], lambda j, s, t, lo, hi, nu: (layer, which(lo, hi)[j], 0, 0))

    def shared_spec(w):
        return pl.BlockSpec((1,) + w.shape[1:], lambda j, s, t, lo, hi, nu: (layer, 0, 0))

    first = lambda lo, hi: lo
    second = lambda lo, hi: hi
    return pl.pallas_call(
        _moe_kernel,
        grid_spec=pltpu.PrefetchScalarGridSpec(
            num_scalar_prefetch=5,
            grid=(n_tiles,),
            in_specs=([pl.BlockSpec(memory_space=pl.ANY)] + [shared_spec(w) for w in w_s]
                      + [routed_spec(w, first) for w in w_e] + [routed_spec(w, second) for w in w_e]),
            out_specs=pl.BlockSpec((MOE_TILE, D_MODEL), lambda j, s, t, lo, hi, nu: (j, 0)),
            scratch_shapes=[pltpu.VMEM((MOE_AHEAD + 1, MOE_TILE, F_EXT), F32),
                            pltpu.SemaphoreType.DMA((MOE_AHEAD + 1,))],
        ),
        out_shape=jax.ShapeDtypeStruct((n_tiles * MOE_TILE, D_MODEL), F32),
        compiler_params=pltpu.CompilerParams(
            dimension_semantics=("arbitrary",), vmem_limit_bytes=VMEM_LIMIT),
        name="moe",
    )(src, tile_idx, e_lo, e_hi, n_used, fext, *w_s, *w_e, *w_e)


def _gather_kernel(pos_ref, xn_ref, gt_ref, ys_hbm, o_ref, buf, sems):
    nt = pl.num_programs(1)
    step = pl.program_id(0) * nt + pl.program_id(1)
    total = pl.num_programs(0) * nt
    slot = step % (MOE_AHEAD + 1)

    def fetch(s):
        into = s % (MOE_AHEAD + 1)
        for r in range(MOE_TILE):
            _row_copy(ys_hbm, pos_ref[s * MOE_TILE + r], buf.at[into], r, sems.at[into]).start()

    for first in range(MOE_AHEAD):
        pl.when((step == 0) & (first < total))(functools.partial(fetch, first))
    pl.when(step + MOE_AHEAD < total)(lambda: fetch(step + MOE_AHEAD))
    for _ in range(MOE_TILE):
        _row_copy(ys_hbm, 0, buf.at[slot], 0, sems.at[slot]).wait()
    o_ref[0] = xn_ref[0] + gt_ref[0, 0] * buf[slot]


def _gather_call(pos, xn, gt2, ys, n_lat):
    b, n, _ = xn.shape
    n_lat_tiles = n_lat // MOE_TILE
    return pl.pallas_call(
        _gather_kernel,
        grid_spec=pltpu.PrefetchScalarGridSpec(
            num_scalar_prefetch=1,
            grid=(b, n // MOE_TILE),
            in_specs=[
                pl.BlockSpec((1, MOE_TILE, D_MODEL), lambda i, j, p: (i, j, 0)),
                pl.BlockSpec((1, 1, 1, D_MODEL),
                             lambda i, j, p: (i, jnp.where(j >= n_lat_tiles, 1, 0), 0, 0)),
                pl.BlockSpec(memory_space=pl.ANY),
            ],
            out_specs=pl.BlockSpec((1, MOE_TILE, D_MODEL), lambda i, j, p: (i, j, 0)),
            scratch_shapes=[pltpu.VMEM((MOE_AHEAD + 1, MOE_TILE, D_MODEL), F32),
                            pltpu.SemaphoreType.DMA((MOE_AHEAD + 1,))],
        ),
        out_shape=jax.ShapeDtypeStruct((b, n, D_MODEL), F32),
        compiler_params=pltpu.CompilerParams(
            dimension_semantics=("arbitrary", "arbitrary"), vmem_limit_bytes=VMEM_LIMIT),
        name="moe_gather_residual",
    )(pos, xn, gt2[:, :, None, :], ys)


def _pad_heads(w, n_heads, dim, width=LANES):
    lead = w.shape[:-1]
    w = w.reshape(lead + (n_heads, dim))
    w = jnp.pad(w, [(0, 0)] * len(lead) + [(0, 0), (0, width - dim)])
    return w.reshape(lead + (n_heads * width,))


def _partner_lanes(lane0, half_dim):
    perm = np.arange(LANES)
    quarter = half_dim // 2
    for part in range(2):
        lo = lane0 + part * half_dim
        perm[lo:lo + quarter] = np.arange(lo + quarter, lo + half_dim)
        perm[lo + quarter:lo + half_dim] = np.arange(lo, lo + quarter)
    return perm


PARTNER_MLA = _partner_lanes(MLA_NOPE, MLA_ROPE // 2)
PARTNER_GQA = _partner_lanes(0, GQA_HD // 2)


def _swap_partners(w, perm):
    lead = w.shape[:-1]
    return w.reshape(lead + (-1, LANES))[..., perm].reshape(w.shape)


def _prep_w_in(w_in):
    offs = np.concatenate([[0], np.cumsum(IN_SIZES)])
    cq, ckv, kr, qb, kb, vb, gts = [w_in[..., int(offs[i]):int(offs[i + 1])] for i in range(7)]
    kr_blk = jnp.pad(kr, [(0, 0), (0, 0), (MLA_NOPE, LANES - MLA_QK)])
    qb_p, kb_p = _pad_heads(qb, GQA_HEADS, GQA_HD), _pad_heads(kb, GQA_KV_HEADS, GQA_HD)
    w_main = jnp.concatenate(
        [cq, ckv, kr_blk, _swap_partners(kr_blk, PARTNER_MLA), qb_p, _swap_partners(qb_p, PARTNER_GQA),
         kb_p, _swap_partners(kb_p, PARTNER_GQA), gts], axis=-1).astype(BF16)
    w_vbt = jnp.swapaxes(_pad_heads(vb, GQA_KV_HEADS, GQA_HD, V_ROWS), 1, 2).astype(BF16)
    return w_main, w_vbt


def _prep_w_ukv(w_ukv):
    l, r, _ = w_ukv.shape
    w = w_ukv.reshape(l, r, MLA_HEADS, MLA_NOPE + MLA_V)
    uk = _pad_heads(w[..., :MLA_NOPE].reshape(l, r, MLA_HEADS * MLA_NOPE), MLA_HEADS, MLA_NOPE)
    uv = _pad_heads(w[..., MLA_NOPE:].reshape(l, r, MLA_HEADS * MLA_V), MLA_HEADS, MLA_V, V_ROWS)
    return uk.astype(BF16), jnp.swapaxes(uv, 1, 2).astype(BF16)


def _pad_gain(g, dim):
    return jnp.pad(g, [(0, 0), (0, LANES - dim)])


def _gain_rows(g_qa, g_ka, g_qb, g_kb):
    rows = []
    for g, dim, perm in ((g_qa, MLA_QK, PARTNER_MLA), (g_ka, MLA_QK, PARTNER_MLA),
                         (g_qb, GQA_HD, PARTNER_GQA), (g_kb, GQA_HD, PARTNER_GQA)):
        g_p = _pad_gain(g, dim)
        rows += [g_p, g_p[:, perm]]
    return jnp.stack(rows, axis=1)


def _rope_tables(n_ctx, n_lat, lane0, half_dim):
    pos = jnp.arange(n_lat)
    inv = ROPE_THETA ** (-jnp.arange(0, half_dim, 2, dtype=F32) / half_dim)
    ang_r = (pos // GRID_W).astype(F32)[:, None] * inv[None, :]
    ang_c = (pos % GRID_W).astype(F32)[:, None] * inv[None, :]
    cos = jnp.concatenate([jnp.cos(ang_r)] * 2 + [jnp.cos(ang_c)] * 2, axis=-1)
    sin = jnp.concatenate([-jnp.sin(ang_r), jnp.sin(ang_r), -jnp.sin(ang_c), jnp.sin(ang_c)], axis=-1)
    pad = [(0, n_ctx), (lane0, LANES - lane0 - 2 * half_dim)]
    return (jnp.pad(cos, pad, constant_values=1.0), jnp.pad(sin, pad))


def kernel(x, c, ctx, c_ctx, w_mod, b_mod, g_attn, g_ffn, w_in, g_cq, w_uq, g_ckv, w_ukv,
           g_qa, g_ka, g_qb, g_kb, w_oa, w_ob, w_out, w_router, b_router,
           w_e_gate, w_e_up, w_e_down, w_s_gate, w_s_up, w_s_down):
    b, n_lat, d = x.shape
    n_ctx = ctx.shape[1]
    n = n_lat + n_ctx
    depth = w_mod.shape[0]
    assert d == D_MODEL and n_ctx == ROW_TILE and n_lat % KV_CHUNK == 0 and n_lat % Q_TILE == 0

    x_lat, x_ctx, ctx_tile = x, ctx, 0
    cs =jnp.concatenate([c, c_ctx[None, :], jnp.zeros((8 - b - 1, d), F32)], axis=0)

    w_in_p, w_vbt = _prep_w_in(w_in)
    w_uq_p = _pad_heads(w_uq, MLA_HEADS, MLA_QK)
    w_uq_p = jnp.concatenate([w_uq_p, _swap_partners(w_uq_p, PARTNER_MLA)], axis=-1).astype(BF16)
    w_uk_p, w_uvt = _prep_w_ukv(w_ukv)
    g3 = lambda g: g[:, None, :]
    gains = _gain_rows(g_qa, g_ka, g_qb, g_kb)
    tables = (_rope_tables(n_ctx, n_lat, MLA_NOPE, MLA_ROPE // 2)
              + _rope_tables(n_ctx, n_lat, 0, GQA_HD // 2))
    w_oa_b, w_ob_b, w_out_b = w_oa.astype(BF16), w_ob.astype(BF16), w_out.astype(BF16)
    wr_p = jnp.pad(w_router, [(0, 0), (0, LANES - N_EXPERTS)])
    wr_hi = wr_p.astype(BF16)
    wr_lo = (wr_p - wr_hi.astype(F32)).astype(BF16)
    br_col = jnp.pad(b_router, (0, LANES - N_EXPERTS))[:, None]
    w_shared = (w_s_gate, w_s_up, w_s_down)
    w_routed = (w_e_gate, w_e_up, w_e_down)
    b_mod3 = b_mod[:, None, :]

    for layer in range(depth):
        last = layer == depth - 1
        n_rows = n_lat if last else n
        mods = _mod_call(cs, w_mod, b_mod3, layer).reshape(8, 6, d)
        lat, cx = mods[:b], jnp.broadcast_to(mods[b:b + 1], (b, 6, d))
        both = jnp.stack([lat, cx], axis=1)
        mod1, modm, gt2 = both[:, :, 0:2], both[:, :, 2:5], both[:, :, 5]
        q_all, k_all, v_all, gts = _in_call(
            x_lat, x_ctx, ctx_tile, mod1, layer, n, n_lat, g3(g_attn), w_in_p, g3(g_cq), w_uq_p,
            g3(g_ckv), w_uk_p, w_uvt, w_vbt, gains, tables)
        o_lat = _attn_lat_call(q_all, k_all, v_all, n_lat)
        o_ctx = o_lat if last else _attn_ctx_call(q_all, k_all, v_all, n_lat)
        xn, fext, counts = _mid_call(o_lat, o_ctx, gts, x_lat, x_ctx, ctx_tile, modm, layer, n_rows,
                                     n_lat, w_oa_b, w_ob_b, w_out_b, g3(g_ffn), wr_hi, wr_lo, br_col)
        n_tok = b * n_rows
        n_tiles_max = n_tok // MOE_TILE + N_PAIR_CLASSES
        fext = fext.reshape(n_tok, F_EXT)
        plan = _route_plan(fext[:, D_MODEL:D_MODEL + 4], counts, n_tiles_max)
        ys = _moe_call(plan, fext, layer, w_shared, w_routed)
        x_lat = _gather_call(plan[0], xn, gt2, ys, n_lat)
        x_ctx, ctx_tile = x_lat, n_lat // ROW_TILE
    return x_lat
```

```python
import functools

import numpy as np
import jax
import jax.numpy as jnp
from jax import lax
from jax.experimental import pallas as pl
from jax.experimental.pallas import tpu as pltpu

D_MODEL = 1024
GRID_W = 64
MLA_HEADS = 8
MLA_NOPE = 64
MLA_ROPE = 32
MLA_QK = MLA_NOPE + MLA_ROPE
MLA_V = 64
Q_LORA = 256
KV_LORA = 128
GQA_HEADS = 8
GQA_KV_HEADS = 2
GQA_GROUP = GQA_HEADS // GQA_KV_HEADS
GQA_HD = 64
N_EXPERTS = 16
N_GROUPS = 4
EXPERTS_PER_GROUP = N_EXPERTS // N_GROUPS
FF_EXPERT = 512
ROPE_THETA = 10000.0
EPS = 1e-6
IN_SIZES = (Q_LORA, KV_LORA, MLA_ROPE, GQA_HEADS * GQA_HD, GQA_KV_HEADS * GQA_HD,
            GQA_KV_HEADS * GQA_HD, 2 * D_MODEL)

LANES = 128
ROW_TILE = 256
Q_TILE = 2048
KV_CHUNK = 512
N_HEADS_ALL = MLA_HEADS + GQA_HEADS
N_KHEADS_ALL = MLA_HEADS + GQA_KV_HEADS
LOG2_E = 1.4426950408889634
ONES_ROW = MLA_V
V_ROWS = MLA_V + 16
VMEM_LIMIT = 56 * 1024 * 1024
MOE_TILE = 256
MOE_AHEAD = 3
N_CLASSES = N_GROUPS << EXPERTS_PER_GROUP
N_PAIR_CLASSES = N_GROUPS * 6
F_EXT = D_MODEL + LANES
META_CLS, META_RANK, META_W_LO, META_W_HI = 0, 1, 2, 3

C_CQ = 0
C_CKV = C_CQ + Q_LORA
C_KR = C_CKV + KV_LORA
C_QB = C_KR + 2 * LANES
C_KB = C_QB + 2 * GQA_HEADS * LANES
C_G = C_KB + 2 * GQA_KV_HEADS * LANES
C_END = C_G + 2 * D_MODEL

BF16 = jnp.bfloat16
F32 = jnp.float32


def _dot(a, b):
    return jnp.dot(a, b, preferred_element_type=F32)


def _dot_nt(a, b):
    return lax.dot_general(a, b, (((1,), (1,)), ((), ())), preferred_element_type=F32)


def _split_bf16(x):
    hi = x.astype(BF16)
    lo = (x - hi.astype(F32)).astype(BF16)
    return hi, lo


def _rms_rows(x, g):
    ms = jnp.mean(x * x, axis=-1, keepdims=True)
    return x * lax.rsqrt(ms + EPS) * g


def _norm_rope(raw, swapped, gain_cos, gain_sin, dim):
    ms = jnp.sum(raw * raw, axis=-1, keepdims=True) * (1.0 / dim)
    return (raw * gain_cos + swapped * gain_sin) * lax.rsqrt(ms + EPS)


def _mod_kernel(c_ref, w_ref, b_ref, o_ref):
    c = c_ref[...]
    s_hi, s_lo = _split_bf16(c * jax.nn.sigmoid(c))
    w_hi, w_lo = _split_bf16(w_ref[0])
    o_ref[...] = _dot(s_hi, w_hi) + _dot(s_lo, w_hi) + _dot(s_hi, w_lo) + b_ref[0]


def _mod_call(cs, w_mod, b_mod, layer):
    n_cols = w_mod.shape[2]
    tn = 1536
    return pl.pallas_call(
        _mod_kernel,
        grid=(n_cols // tn,),
        in_specs=[
            pl.BlockSpec(cs.shape, lambda j: (0, 0)),
            pl.BlockSpec((1, D_MODEL, tn), lambda j: (layer, 0, j)),
            pl.BlockSpec((1, 1, tn), lambda j: (layer, 0, j)),
        ],
        out_specs=pl.BlockSpec((cs.shape[0], tn), lambda j: (0, j)),
        out_shape=jax.ShapeDtypeStruct((cs.shape[0], n_cols), F32),
        compiler_params=pltpu.CompilerParams(
            dimension_semantics=("arbitrary",), vmem_limit_bytes=VMEM_LIMIT),
        name="mod",
    )(cs, w_mod, b_mod)


def _in_kernel(x_ref, xc_ref, mod_ref, gattn_ref, win_ref, gcq_ref, wuq_ref, gckv_ref, wuk_ref,
               wuvt_ref, wvbt_ref, gains_ref, cosa_ref, sina_ref, cosb_ref, sinb_ref,
               q_ref, k_ref, v_ref, g_ref, *, n_lat_tiles):
    x = jnp.where(pl.program_id(1) >= n_lat_tiles, xc_ref[0], x_ref[0])
    shift = mod_ref[0, 0, 0:1, :]
    scale = mod_ref[0, 0, 1:2, :]
    h = (_rms_rows(x, gattn_ref[0]) * (1 + scale) + shift).astype(BF16)

    g_ref[0] = _dot(h, win_ref[0, :, C_G:C_END])

    gains = gains_ref[0]
    cos_a, sin_a, cos_b, sin_b = cosa_ref[...], sina_ref[...], cosb_ref[...], sinb_ref[...]
    scale_a = MLA_QK ** -0.5 * LOG2_E
    scale_b = GQA_HD ** -0.5 * LOG2_E
    qa_cos, qa_sin = cos_a * (gains[0:1] * scale_a), sin_a * (gains[1:2] * scale_a)
    ka_cos, ka_sin = cos_a * gains[2:3], sin_a * gains[3:4]
    qb_cos, qb_sin = cos_b * (gains[4:5] * scale_b), sin_b * (gains[5:6] * scale_b)
    kb_cos, kb_sin = cos_b * gains[6:7], sin_b * gains[7:8]
    t = x.shape[0]
    ones_row = lax.broadcasted_iota(jnp.int32, (V_ROWS, t), 0) == ONES_ROW
    blk = lambda a, i: a[:, i * LANES:(i + 1) * LANES]

    p0 = _dot(h, win_ref[0, :, C_CQ:C_QB])
    cq = p0[:, C_CQ:C_CKV]
    ckv = p0[:, C_CKV:C_KR]
    kr_blk = p0[:, C_KR:C_KR + LANES]
    kr_swapped = p0[:, C_KR + LANES:C_QB]
    qa = _dot(_rms_rows(cq, gcq_ref[0]).astype(BF16), wuq_ref[0])
    ckv_n = _rms_rows(ckv, gckv_ref[0]).astype(BF16)
    ka = _dot(ckv_n, wuk_ref[0])
    va_t = _dot_nt(wuvt_ref[0], ckv_n)
    for hd in range(MLA_HEADS):
        q_ref[0, hd] = _norm_rope(blk(qa, hd), blk(qa, MLA_HEADS + hd), qa_cos, qa_sin,
                                  MLA_QK).astype(BF16)
        k_ref[0, hd] = _norm_rope(blk(ka, hd) + kr_blk, kr_swapped, ka_cos, ka_sin,
                                  MLA_QK).astype(BF16)
        v_ref[0, hd, 0] = jnp.where(ones_row, 1.0, va_t[hd * V_ROWS:(hd + 1) * V_ROWS, :]).astype(BF16)

    pq = _dot(h, win_ref[0, :, C_QB:C_KB])
    for hd in range(GQA_HEADS):
        q_ref[0, MLA_HEADS + hd] = _norm_rope(blk(pq, hd), blk(pq, GQA_HEADS + hd), qb_cos, qb_sin,
                                              GQA_HD).astype(BF16)
    pk = _dot(h, win_ref[0, :, C_KB:C_G])
    vb_t = _dot_nt(wvbt_ref[0], h)
    for hd in range(GQA_KV_HEADS):
        k_ref[0, MLA_HEADS + hd] = _norm_rope(blk(pk, hd), blk(pk, GQA_KV_HEADS + hd), kb_cos,
                                              kb_sin, GQA_HD).astype(BF16)
        v_ref[0, MLA_HEADS + hd, 0] = jnp.where(
            ones_row, 1.0, vb_t[hd * V_ROWS:(hd + 1) * V_ROWS, :]).astype(BF16)


def _stream_specs(n_lat_tiles, ctx_tile):
    t = ROW_TILE
    return [pl.BlockSpec((1, t, D_MODEL), lambda i, j: (i, jnp.minimum(j, n_lat_tiles - 1), 0)),
            pl.BlockSpec((1, t, D_MODEL), lambda i, j: (i, ctx_tile, 0))]


def _in_call(x_lat, x_ctx, ctx_tile, mod1, layer, n, n_lat, g_attn, w_in_p, g_cq, w_uq_p, g_ckv,
             w_uk_p, w_uvt, w_vbt, gains, tables):
    b = x_lat.shape[0]
    nt = n // ROW_TILE
    n_lat_tiles = n_lat // ROW_TILE
    t = ROW_TILE

    def lspec(arr):
        shp = arr.shape
        return pl.BlockSpec((1,) + shp[1:], lambda i, j: (layer,) + (0,) * (len(shp) - 1))

    tab_spec = pl.BlockSpec((t, LANES), lambda i, j: (j, 0))
    return pl.pallas_call(
        functools.partial(_in_kernel, n_lat_tiles=n_lat_tiles),
        grid=(b, nt),
        in_specs=_stream_specs(n_lat_tiles, ctx_tile) + [
            pl.BlockSpec((1, 1, 2, D_MODEL), lambda i, j: (i, jnp.where(j >= n_lat_tiles, 1, 0), 0, 0)),
            lspec(g_attn), lspec(w_in_p), lspec(g_cq), lspec(w_uq_p), lspec(g_ckv),
            lspec(w_uk_p), lspec(w_uvt), lspec(w_vbt), lspec(gains),
        ] + [tab_spec] * 4,
        out_specs=[
            pl.BlockSpec((1, N_HEADS_ALL, t, LANES), lambda i, j: (i, 0, j, 0)),
            pl.BlockSpec((1, N_KHEADS_ALL, t, LANES), lambda i, j: (i, 0, j, 0)),
            pl.BlockSpec((1, N_KHEADS_ALL, 1, V_ROWS, t), lambda i, j: (i, 0, j, 0, 0)),
            pl.BlockSpec((1, t, 2 * D_MODEL), lambda i, j: (i, j, 0)),
        ],
        out_shape=[
            jax.ShapeDtypeStruct((b, N_HEADS_ALL, n, LANES), BF16),
            jax.ShapeDtypeStruct((b, N_KHEADS_ALL, n, LANES), BF16),
            jax.ShapeDtypeStruct((b, N_KHEADS_ALL, nt, V_ROWS, t), BF16),
            jax.ShapeDtypeStruct((b, n, 2 * D_MODEL), F32),
        ],
        compiler_params=pltpu.CompilerParams(
            dimension_semantics=("arbitrary", "arbitrary"), vmem_limit_bytes=VMEM_LIMIT),
        name="mixer_in",
    )(x_lat, x_ctx, mod1, g_attn, w_in_p, g_cq, w_uq_p, g_ckv, w_uk_p, w_uvt, w_vbt, gains, *tables)


def _attn_kernel(q_ref, k0_ref, k1_ref, v0_ref, v1_ref, *rest, n_main, tail_chunk, pipelined):
    if pipelined:
        o_ref, s_scr, acc_scr = rest[-3:]
    else:
        o_ref, acc_scr = rest[-2:]
    qs = (q_ref[0, 0], q_ref[0, 1])
    k_refs = (k0_ref, k1_ref)
    v_refs = (v0_ref, v1_ref)
    tq = qs[0].shape[0]
    per = KV_CHUNK // ROW_TILE

    def fold(hh, m, cmax, st, chunk0, n_sub, first=False):
        m_new = cmax if first else jnp.maximum(m, cmax)
        pt = jnp.exp2(st - m_new).astype(BF16)
        pv = _dot(v_refs[hh][0, 0, chunk0], pt[0:ROW_TILE])
        for u in range(1, n_sub):
            pv = pv + _dot(v_refs[hh][0, 0, chunk0 + u], pt[u * ROW_TILE:(u + 1) * ROW_TILE])
        if first:
            acc_scr[hh] = pv
        else:
            acc_scr[hh] = jnp.exp2(m - m_new) * acc_scr[hh] + pv
        return m_new

    def produce(c, slot):
        cmax = []
        for hh in range(2):
            off = c * KV_CHUNK
            if not isinstance(off, int):
                off = pl.multiple_of(off, KV_CHUNK)
            st = _dot_nt(k_refs[hh][0, 0, pl.ds(off, KV_CHUNK), :], qs[hh])
            s_scr[slot, hh] = st
            cmax.append(jnp.max(st, axis=0, keepdims=True))
        return tuple(cmax)

    def consume(c, slot, ms, cmax):
        return tuple(fold(hh, ms[hh], cmax[hh], s_scr[slot, hh], c * per, per) for hh in range(2))

    ms = []
    for hh in range(2):
        st = _dot_nt(k_refs[hh][0, 0, pl.ds(tail_chunk * ROW_TILE, ROW_TILE), :], qs[hh])
        ms.append(fold(hh, None, jnp.max(st, axis=0, keepdims=True), st, tail_chunk, 1, first=True))
    ms = tuple(ms)

    if pipelined:
        def body(i, carry):
            ms, cmax = carry
            cmax1 = produce(2 * i + 1, 1)
            ms = consume(2 * i, 0, ms, cmax)
            cmax2 = produce(2 * i + 2, 0)
            ms = consume(2 * i + 1, 1, ms, cmax1)
            return ms, cmax2

        ms, cmax = lax.fori_loop(0, (n_main - 2) // 2, body, (ms, produce(0, 0)))
        cmax1 = produce(n_main - 1, 1)
        ms = consume(n_main - 2, 0, ms, cmax)
        ms = consume(n_main - 1, 1, ms, cmax1)

    outs = []
    for hh in range(2):
        acc = acc_scr[hh]
        o_t = jnp.concatenate([acc / acc[ONES_ROW:ONES_ROW + 1, :],
                               jnp.zeros((LANES - V_ROWS, tq), F32)], axis=0)
        outs.append(o_t.T)
    low = lax.broadcasted_iota(jnp.int32, (tq, LANES), 1) < MLA_V
    o_ref[0] = jnp.where(low, outs[0], pltpu.roll(outs[1], MLA_V, 1)).astype(o_ref.dtype)


def _kv_head(p):
    mla_pairs = MLA_HEADS // 2
    pairs_per_kv = GQA_GROUP // 2
    gqa = MLA_HEADS + (p - mla_pairs) // pairs_per_kv
    return jnp.where(p < mla_pairs, 2 * p, gqa), jnp.where(p < mla_pairs, 2 * p + 1, gqa)


def _attn_lat_call(q_all, k_all, v_all, n_lat):
    b, _, n, _ = q_all.shape
    n_pairs = N_HEADS_ALL // 2
    n_chunks = n // ROW_TILE
    n_main = n_lat // KV_CHUNK
    assert n_main >= 2 and n_main % 2 == 0
    kern = functools.partial(_attn_kernel, n_main=n_main, tail_chunk=n_lat // ROW_TILE,
                             pipelined=True)
    k_spec = lambda which: pl.BlockSpec(
        (1, 1, n, LANES), lambda i, p, j: (i, _kv_head(p)[which], 0, 0))
    v_spec = lambda which: pl.BlockSpec(
        (1, 1, n_chunks, V_ROWS, ROW_TILE), lambda i, p, j: (i, _kv_head(p)[which], 0, 0, 0))
    return pl.pallas_call(
        kern,
        grid=(b, n_pairs, n_lat // Q_TILE),
        in_specs=[pl.BlockSpec((1, 2, Q_TILE, LANES), lambda i, p, j: (i, p, j, 0)),
                  k_spec(0), k_spec(1), v_spec(0), v_spec(1)],
        out_specs=pl.BlockSpec((1, Q_TILE, LANES), lambda i, p, j: (i, j, p)),
        out_shape=jax.ShapeDtypeStruct((b, n_lat, n_pairs * LANES), BF16),
        scratch_shapes=[pltpu.VMEM((2, 2, KV_CHUNK, Q_TILE), F32),
                        pltpu.VMEM((2, V_ROWS, Q_TILE), F32)],
        compiler_params=pltpu.CompilerParams(
            dimension_semantics=("arbitrary", "arbitrary", "arbitrary"),
            vmem_limit_bytes=VMEM_LIMIT),
        name="attention",
    )(q_all, k_all, k_all, v_all, v_all)


def _attn_ctx_call(q_all, k_all, v_all, n_lat):
    b, _, n, _ = q_all.shape
    n_pairs = N_HEADS_ALL // 2
    c = n_lat // ROW_TILE
    assert n - n_lat == ROW_TILE
    kern = functools.partial(_attn_kernel, n_main=0, tail_chunk=0, pipelined=False)
    k_spec = lambda which: pl.BlockSpec(
        (1, 1, ROW_TILE, LANES), lambda i, p: (i, _kv_head(p)[which], c, 0))
    v_spec = lambda which: pl.BlockSpec(
        (1, 1, 1, V_ROWS, ROW_TILE), lambda i, p: (i, _kv_head(p)[which], c, 0, 0))
    return pl.pallas_call(
        kern,
        grid=(b, n_pairs),
        in_specs=[pl.BlockSpec((1, 2, ROW_TILE, LANES), lambda i, p: (i, p, c, 0)),
                  k_spec(0), k_spec(1), v_spec(0), v_spec(1)],
        out_specs=pl.BlockSpec((1, ROW_TILE, LANES), lambda i, p: (i, 0, p)),
        out_shape=jax.ShapeDtypeStruct((b, ROW_TILE, n_pairs * LANES), BF16),
        scratch_shapes=[pltpu.VMEM((2, V_ROWS, ROW_TILE), F32)],
        compiler_params=pltpu.CompilerParams(
            dimension_semantics=("arbitrary", "arbitrary"), vmem_limit_bytes=VMEM_LIMIT),
        name="attention_ctx",
    )(q_all, k_all, k_all, v_all, v_all)


def _route_rows(sel, scores):
    rows = [sel[e:e + 1, :] for e in range(N_EXPERTS)]
    srow = [scores[e:e + 1, :] for e in range(N_EXPERTS)]
    grp = []
    for g in range(N_GROUPS):
        a, b, c, d = rows[4 * g:4 * g + 4]
        hi1, lo1 = jnp.maximum(a, b), jnp.minimum(a, b)
        hi2, lo2 = jnp.maximum(c, d), jnp.minimum(c, d)
        top1 = jnp.maximum(hi1, hi2)
        top2 = jnp.maximum(jnp.minimum(hi1, hi2), jnp.maximum(lo1, lo2))
        grp.append(top1 + top2)
    best = jnp.zeros_like(grp[0], dtype=jnp.int32)
    best_v = grp[0]
    for g in range(1, N_GROUPS):
        upd = grp[g] > best_v
        best = jnp.where(upd, g, best)
        best_v = jnp.where(upd, grp[g], best_v)
    picked = []
    keeps = []
    for e in range(N_EXPERTS):
        g = e // EXPERTS_PER_GROUP
        rank = jnp.zeros_like(best)
        for o in range(4 * g, 4 * g + 4):
            if o == e:
                continue
            ahead = rows[o] > rows[e]
            if o < e:
                ahead = ahead | (rows[o] == rows[e])
            rank = rank + ahead.astype(jnp.int32)
        keep = (rank < 2) & (best == g)
        picked.append(jnp.where(keep, srow[e], 0.0))
        keeps.append(keep)
    total = picked[0]
    for e in range(1, N_EXPERTS):
        total = total + picked[e]
    return [p / total for p in picked], best, keeps


def _route_meta(gates, best, keeps):
    mask = jnp.zeros_like(best)
    w_lo = jnp.zeros_like(gates[0])
    w_hi = jnp.zeros_like(gates[0])
    for e in range(N_EXPERTS):
        g, i = divmod(e, EXPERTS_PER_GROUP)
        mask = mask + jnp.where(keeps[e], 1 << i, 0)
        if i == 0:
            w_lo = w_lo + jnp.where(keeps[e], gates[e], 0.0)
            continue
        below = keeps[4 * g]
        for o in range(4 * g + 1, e):
            below = below | keeps[o]
        w_lo = w_lo + jnp.where(keeps[e] & ~below, gates[e], 0.0)
        w_hi = w_hi + jnp.where(keeps[e] & below, gates[e], 0.0)
    cls = (best * (1 << EXPERTS_PER_GROUP) + mask).astype(F32)
    return cls, w_lo, w_hi


def _mid_kernel(o_ref, octx_ref, g_ref, x_ref, xc_ref, mod_ref, woa_ref, wob_ref, wout_ref, gffn_ref,
                wrh_ref, wrl_ref, brc_ref, xn_ref, fext_ref, cnt_ref, run_ref, *, n_lat_tiles):
    is_ctx = pl.program_id(1) >= n_lat_tiles
    o = jnp.where(is_ctx, octx_ref[0], o_ref[0])
    n_a = MLA_HEADS * MLA_V
    ya = _dot(o[:, 0:n_a], woa_ref[0])
    yb = _dot(o[:, n_a:], wob_ref[0])
    g = g_ref[0]
    y = jax.nn.sigmoid(g[:, 0:D_MODEL]) * ya + jax.nn.sigmoid(g[:, D_MODEL:]) * yb
    z = _dot(y.astype(BF16), wout_ref[0])
    gt1 = mod_ref[0, 0, 0:1, :]
    sh2 = mod_ref[0, 0, 1:2, :]
    sc2 = mod_ref[0, 0, 2:3, :]
    xn = jnp.where(is_ctx, xc_ref[0], x_ref[0]) + gt1 * z
    xn_ref[0] = xn
    f = _rms_rows(xn, gffn_ref[0]) * (1 + sc2) + sh2
    fext_ref[0, :, 0:D_MODEL] = f
    f_hi, f_lo = _split_bf16(f)
    w_hi = wrh_ref[...]
    logits = _dot(f_hi, w_hi) + _dot(f_lo, w_hi) + _dot(f_hi, wrl_ref[...])
    scores_t = jax.nn.sigmoid(logits).T
    scores = scores_t[0:N_EXPERTS, :]
    sel = scores + brc_ref[0:N_EXPERTS, :]
    cls, w_lo, w_hi_gate = _route_meta(*_route_rows(sel, scores))
    t = scores.shape[1]
    row_id = lax.broadcasted_iota(jnp.int32, (LANES, t), 0)
    meta_t = jnp.where(row_id == META_CLS, cls, 0.0)
    meta_t = jnp.where(row_id == META_W_LO, w_lo, meta_t)
    meta_t = jnp.where(row_id == META_W_HI, w_hi_gate, meta_t)
    meta = meta_t.T

    @pl.when((pl.program_id(0) == 0) & (pl.program_id(1) == 0))
    def _():
        run_ref[...] = jnp.zeros_like(run_ref)

    lane = lax.broadcasted_iota(jnp.int32, (t, LANES), 1)
    onehot = jnp.where(lane.astype(F32) == meta[:, META_CLS:META_CLS + 1], 1.0, 0.0)
    earlier = jnp.where(lax.broadcasted_iota(jnp.int32, (t, t), 0)
                        > lax.broadcasted_iota(jnp.int32, (t, t), 1), 1.0, 0.0).astype(BF16)
    before = _dot(earlier, onehot.astype(BF16)) + run_ref[...]
    rank = jnp.sum(before * onehot, axis=-1, keepdims=True)
    fext_ref[0, :, D_MODEL:] = jnp.where(lane == META_RANK, rank, meta)
    run_ref[...] += jnp.sum(onehot, axis=0, keepdims=True)
    cnt_ref[...] = jnp.broadcast_to(run_ref[...], cnt_ref.shape)


def _mid_call(o_lat, o_ctx, gts, x_lat, x_ctx, ctx_tile, modm, layer, n_rows, n_lat, w_oa, w_ob, w_out,
              g_ffn, wr_hi, wr_lo, br_col):
    b = x_lat.shape[0]
    n = n_rows
    n_lat_tiles = n_lat // ROW_TILE
    t = ROW_TILE

    def lspec(arr):
        shp = arr.shape
        return pl.BlockSpec((1,) + shp[1:], lambda i, j: (layer,) + (0,) * (len(shp) - 1))

    def full(arr):
        return pl.BlockSpec(arr.shape, lambda i, j: (0,) * arr.ndim)

    return pl.pallas_call(
        functools.partial(_mid_kernel, n_lat_tiles=n_lat_tiles),
        grid=(b, n // t),
        in_specs=[
            pl.BlockSpec((1, t, o_lat.shape[2]), lambda i, j: (i, jnp.minimum(j, n_lat_tiles - 1), 0)),
            pl.BlockSpec((1, t, o_ctx.shape[2]), lambda i, j: (i, 0, 0)),
            pl.BlockSpec((1, t, 2 * D_MODEL), lambda i, j: (i, j, 0)),
        ] + _stream_specs(n_lat_tiles, ctx_tile) + [
            pl.BlockSpec((1, 1, 3, D_MODEL), lambda i, j: (i, jnp.where(j >= n_lat_tiles, 1, 0), 0, 0)),
            lspec(w_oa), lspec(w_ob), lspec(w_out), lspec(g_ffn),
            full(wr_hi), full(wr_lo), full(br_col),
        ],
        out_specs=[
            pl.BlockSpec((1, t, D_MODEL), lambda i, j: (i, j, 0)),
            pl.BlockSpec((1, t, F_EXT), lambda i, j: (i, j, 0)),
            pl.BlockSpec((8, LANES), lambda i, j: (0, 0)),
        ],
        out_shape=[
            jax.ShapeDtypeStruct((b, n, D_MODEL), F32),
            jax.ShapeDtypeStruct((b, n, F_EXT), F32),
            jax.ShapeDtypeStruct((8, LANES), F32),
        ],
        scratch_shapes=[pltpu.VMEM((1, LANES), F32)],
        compiler_params=pltpu.CompilerParams(
            dimension_semantics=("arbitrary", "arbitrary"), vmem_limit_bytes=VMEM_LIMIT),
        name="mixer_out_router",
    )(o_lat, o_ctx, gts, x_lat, x_ctx, modm, w_oa, w_ob, w_out, g_ffn, wr_hi, wr_lo, br_col)


def _class_expert_tables():
    lo = np.zeros((N_CLASSES,), np.int32)
    hi = np.zeros((N_CLASSES,), np.int32)
    for c in range(N_CLASSES):
        g, mask = divmod(c, 1 << EXPERTS_PER_GROUP)
        bits = [i for i in range(EXPERTS_PER_GROUP) if mask >> i & 1]
        if len(bits) == 2:
            lo[c], hi[c] = EXPERTS_PER_GROUP * g + bits[0], EXPERTS_PER_GROUP * g + bits[1]
    return lo, hi


def _route_plan(meta, counts, n_tiles_max):
    cls = meta[:, META_CLS].astype(jnp.int32)
    rank = meta[:, META_RANK].astype(jnp.int32)
    cnt = counts[0, :N_CLASSES].astype(jnp.int32)
    tiles = (cnt + MOE_TILE - 1) // MOE_TILE
    tile_end = jnp.cumsum(tiles)
    start = (tile_end - tiles) * MOE_TILE
    hit = cls[None, :] == jnp.arange(N_CLASSES, dtype=jnp.int32)[:, None]
    pos = jnp.sum(jnp.where(hit, start[:, None], 0), axis=0) + rank
    n_used = tile_end[-1]
    tile_idx = jnp.minimum(jnp.arange(n_tiles_max, dtype=jnp.int32), n_used - 1)
    tile_cls = jnp.sum(tile_end[None, :] <= tile_idx[:, None], axis=1).astype(jnp.int32)
    lo, hi = _class_expert_tables()
    src = jnp.zeros((n_tiles_max * MOE_TILE,), jnp.int32).at[pos].set(
        jnp.arange(pos.shape[0], dtype=jnp.int32))
    return (pos.astype(jnp.int32), src, tile_idx, jnp.asarray(lo)[tile_cls],
            jnp.asarray(hi)[tile_cls], n_used.reshape(1).astype(jnp.int32))


def _row_copy(src_ref, src_row, dst_ref, dst_row, sem):
    return pltpu.make_async_copy(src_ref.at[pl.ds(src_row, 1)], dst_ref.at[pl.ds(dst_row, 1)], sem)


def _moe_kernel(src_ref, tidx_ref, elo_ref, ehi_ref, nused_ref, f_hbm, wgs_ref, wus_ref, wds_ref,
                wg1_ref, wu1_ref, wd1_ref, wg2_ref, wu2_ref, wd2_ref, ys_ref, buf, sems):
    del elo_ref, ehi_ref
    j = pl.program_id(0)
    n_used = nused_ref[0]
    slot = j % (MOE_AHEAD + 1)

    def fetch(step):
        into = step % (MOE_AHEAD + 1)
        base = tidx_ref[step] * MOE_TILE
        for r in range(MOE_TILE):
            _row_copy(f_hbm, src_ref[base + r], buf.at[into], r, sems.at[into]).start()

    for first in range(MOE_AHEAD):
        pl.when((j == 0) & (first < n_used))(functools.partial(fetch, first))
    pl.when(j + MOE_AHEAD < n_used)(lambda: fetch(j + MOE_AHEAD))
    used = j < n_used

    @pl.when(jnp.logical_not(used))
    def _():
        ys_ref[...] = jnp.zeros_like(ys_ref)

    @pl.when(used)
    def _():
        for _ in range(MOE_TILE):
            _row_copy(f_hbm, 0, buf.at[slot], 0, sems.at[slot]).wait()
        rows = buf[slot]
        x = rows[:, 0:D_MODEL].astype(BF16)
        w_lo = rows[:, D_MODEL + META_W_LO:D_MODEL + META_W_LO + 1]
        w_hi = rows[:, D_MODEL + META_W_HI:D_MODEL + META_W_HI + 1]

        def ffn(wg_ref, wu_ref, wd_ref):
            wg, wu, wd = (r[(0,) * (len(r.shape) - 2)].astype(BF16) for r in (wg_ref, wu_ref, wd_ref))
            a = jax.nn.silu(_dot(x, wg)) * _dot(x, wu)
            return _dot(a.astype(BF16), wd)

        ys_ref[...] = (ffn(wgs_ref, wus_ref, wds_ref) + w_lo * ffn(wg1_ref, wu1_ref, wd1_ref)
                       + w_hi * ffn(wg2_ref, wu2_ref, wd2_ref))


def _moe_call(plan, fext, layer, w_s, w_e):
    _, src, tile_idx, e_lo, e_hi, n_used = plan
    n_tiles = tile_idx.shape[0]

    def routed_spec(w, which):
        return pl.BlockSpec((1, 1) + w.shape[2:], lambda j, s, t, lo, hi, nu: (layer, which(lo, hi)[j], 0, 0))

    def shared_spec(w):
        return pl.BlockSpec((1,) + w.shape[1:], lambda j, s, t, lo, hi, nu: (layer, 0, 0))

    first = lambda lo, hi: lo
    second = lambda lo, hi: hi
    return pl.pallas_call(
        _moe_kernel,
        grid_spec=pltpu.PrefetchScalarGridSpec(
            num_scalar_prefetch=5,
            grid=(n_tiles,),
            in_specs=([pl.BlockSpec(memory_space=pl.ANY)] + [shared_spec(w) for w in w_s]
                      + [routed_spec(w, first) for w in w_e] + [routed_spec(w, second) for w in w_e]),
            out_specs=pl.BlockSpec((MOE_TILE, D_MODEL), lambda j, s, t, lo, hi, nu: (j, 0)),
            scratch_shapes=[pltpu.VMEM((MOE_AHEAD + 1, MOE_TILE, F_EXT), F32),
                            pltpu.SemaphoreType.DMA((MOE_AHEAD + 1,))],
        ),
        out_shape=jax.ShapeDtypeStruct((n_tiles * MOE_TILE, D_MODEL), F32),
        compiler_params=pltpu.CompilerParams(
            dimension_semantics=("arbitrary",), vmem_limit_bytes=VMEM_LIMIT),
        name="moe",
    )(src, tile_idx, e_lo, e_hi, n_used, fext, *w_s, *w_e, *w_e)


def _gather_kernel(pos_ref, xn_ref, gt_ref, ys_hbm, o_ref, buf, sems):
    nt = pl.num_programs(1)
    step = pl.program_id(0) * nt + pl.program_id(1)
    total = pl.num_programs(0) * nt
    slot = step % (MOE_AHEAD + 1)

    def fetch(s):
        into = s % (MOE_AHEAD + 1)
        for r in range(MOE_TILE):
            _row_copy(ys_hbm, pos_ref[s * MOE_TILE + r], buf.at[into], r, sems.at[into]).start()

    for first in range(MOE_AHEAD):
        pl.when((step == 0) & (first < total))(functools.partial(fetch, first))
    pl.when(step + MOE_AHEAD < total)(lambda: fetch(step + MOE_AHEAD))
    for _ in range(MOE_TILE):
        _row_copy(ys_hbm, 0, buf.at[slot], 0, sems.at[slot]).wait()
    o_ref[0] = xn_ref[0] + gt_ref[0, 0] * buf[slot]


def _gather_call(pos, xn, gt2, ys, n_lat):
    b, n, _ = xn.shape
    n_lat_tiles = n_lat // MOE_TILE
    return pl.pallas_call(
        _gather_kernel,
        grid_spec=pltpu.PrefetchScalarGridSpec(
            num_scalar_prefetch=1,
            grid=(b, n // MOE_TILE),
            in_specs=[
                pl.BlockSpec((1, MOE_TILE, D_MODEL), lambda i, j, p: (i, j, 0)),
                pl.BlockSpec((1, 1, 1, D_MODEL),
                             lambda i, j, p: (i, jnp.where(j >= n_lat_tiles, 1, 0), 0, 0)),
                pl.BlockSpec(memory_space=pl.ANY),
            ],
            out_specs=pl.BlockSpec((1, MOE_TILE, D_MODEL), lambda i, j, p: (i, j, 0)),
            scratch_shapes=[pltpu.VMEM((MOE_AHEAD + 1, MOE_TILE, D_MODEL), F32),
                            pltpu.SemaphoreType.DMA((MOE_AHEAD + 1,))],
        ),
        out_shape=jax.ShapeDtypeStruct((b, n, D_MODEL), F32),
        compiler_params=pltpu.CompilerParams(
            dimension_semantics=("arbitrary", "arbitrary"), vmem_limit_bytes=VMEM_LIMIT),
        name="moe_gather_residual",
    )(pos, xn, gt2[:, :, None, :], ys)


def _pad_heads(w, n_heads, dim, width=LANES):
    lead = w.shape[:-1]
    w = w.reshape(lead + (n_heads, dim))
    w = jnp.pad(w, [(0, 0)] * len(lead) + [(0, 0), (0, width - dim)])
    return w.reshape(lead + (n_heads * width,))


def _partner_lanes(lane0, half_dim):
    perm = np.arange(LANES)
    quarter = half_dim // 2
    for part in range(2):
        lo = lane0 + part * half_dim
        perm[lo:lo + quarter] = np.arange(lo + quarter, lo + half_dim)
        perm[lo + quarter:lo + half_dim] = np.arange(lo, lo + quarter)
    return perm


PARTNER_MLA = _partner_lanes(MLA_NOPE, MLA_ROPE // 2)
PARTNER_GQA = _partner_lanes(0, GQA_HD // 2)


def _swap_partners(w, perm):
    lead = w.shape[:-1]
    return w.reshape(lead + (-1, LANES))[..., perm].reshape(w.shape)


def _prep_w_in(w_in):
    offs = np.concatenate([[0], np.cumsum(IN_SIZES)])
    cq, ckv, kr, qb, kb, vb, gts = [w_in[..., int(offs[i]):int(offs[i + 1])] for i in range(7)]
    kr_blk = jnp.pad(kr, [(0, 0), (0, 0), (MLA_NOPE, LANES - MLA_QK)])
    qb_p, kb_p = _pad_heads(qb, GQA_HEADS, GQA_HD), _pad_heads(kb, GQA_KV_HEADS, GQA_HD)
    w_main = jnp.concatenate(
        [cq, ckv, kr_blk, _swap_partners(kr_blk, PARTNER_MLA), qb_p, _swap_partners(qb_p, PARTNER_GQA),
         kb_p, _swap_partners(kb_p, PARTNER_GQA), gts], axis=-1).astype(BF16)
    w_vbt = jnp.swapaxes(_pad_heads(vb, GQA_KV_HEADS, GQA_HD, V_ROWS), 1, 2).astype(BF16)
    return w_main, w_vbt


def _prep_w_ukv(w_ukv):
    l, r, _ = w_ukv.shape
    w = w_ukv.reshape(l, r, MLA_HEADS, MLA_NOPE + MLA_V)
    uk = _pad_heads(w[..., :MLA_NOPE].reshape(l, r, MLA_HEADS * MLA_NOPE), MLA_HEADS, MLA_NOPE)
    uv = _pad_heads(w[..., MLA_NOPE:].reshape(l, r, MLA_HEADS * MLA_V), MLA_HEADS, MLA_V, V_ROWS)
    return uk.astype(BF16), jnp.swapaxes(uv, 1, 2).astype(BF16)


def _pad_gain(g, dim):
    return jnp.pad(g, [(0, 0), (0, LANES - dim)])


def _gain_rows(g_qa, g_ka, g_qb, g_kb):
    rows = []
    for g, dim, perm in ((g_qa, MLA_QK, PARTNER_MLA), (g_ka, MLA_QK, PARTNER_MLA),
                         (g_qb, GQA_HD, PARTNER_GQA), (g_kb, GQA_HD, PARTNER_GQA)):
        g_p = _pad_gain(g, dim)
        rows += [g_p, g_p[:, perm]]
    return jnp.stack(rows, axis=1)


def _rope_tables(n_ctx, n_lat, lane0, half_dim):
    pos = jnp.arange(n_lat)
    inv = ROPE_THETA ** (-jnp.arange(0, half_dim, 2, dtype=F32) / half_dim)
    ang_r = (pos // GRID_W).astype(F32)[:, None] * inv[None, :]
    ang_c = (pos % GRID_W).astype(F32)[:, None] * inv[None, :]
    cos = jnp.concatenate([jnp.cos(ang_r)] * 2 + [jnp.cos(ang_c)] * 2, axis=-1)
    sin = jnp.concatenate([-jnp.sin(ang_r), jnp.sin(ang_r), -jnp.sin(ang_c), jnp.sin(ang_c)], axis=-1)
    pad = [(0, n_ctx), (lane0, LANES - lane0 - 2 * half_dim)]
    return (jnp.pad(cos, pad, constant_values=1.0), jnp.pad(sin, pad))


def kernel(x, c, ctx, c_ctx, w_mod, b_mod, g_attn, g_ffn, w_in, g_cq, w_uq, g_ckv, w_ukv,
           g_qa, g_ka, g_qb, g_kb, w_oa, w_ob, w_out, w_router, b_router,
           w_e_gate, w_e_up, w_e_down, w_s_gate, w_s_up, w_s_down):
    b, n_lat, d = x.shape
    n_ctx = ctx.shape[1]
    n = n_lat + n_ctx
    depth = w_mod.shape[0]
    assert d == D_MODEL and n_ctx == ROW_TILE and n_lat % KV_CHUNK == 0 and n_lat % Q_TILE == 0

    x_lat, x_ctx, ctx_tile = x, ctx, 0
    cs =jnp.concatenate([c, c_ctx[None, :], jnp.zeros((8 - b - 1, d), F32)], axis=0)

    w_in_p, w_vbt = _prep_w_in(w_in)
    w_uq_p = _pad_heads(w_uq, MLA_HEADS, MLA_QK)
    w_uq_p = jnp.concatenate([w_uq_p, _swap_partners(w_uq_p, PARTNER_MLA)], axis=-1).astype(BF16)
    w_uk_p, w_uvt = _prep_w_ukv(w_ukv)
    g3 = lambda g: g[:, None, :]
    gains = _gain_rows(g_qa, g_ka, g_qb, g_kb)
    tables = (_rope_tables(n_ctx, n_lat, MLA_NOPE, MLA_ROPE // 2)
              + _rope_tables(n_ctx, n_lat, 0, GQA_HD // 2))
    w_oa_b, w_ob_b, w_out_b = w_oa.astype(BF16), w_ob.astype(BF16), w_out.astype(BF16)
    wr_p = jnp.pad(w_router, [(0, 0), (0, LANES - N_EXPERTS)])
    wr_hi = wr_p.astype(BF16)
    wr_lo = (wr_p - wr_hi.astype(F32)).astype(BF16)
    br_col = jnp.pad(b_router, (0, LANES - N_EXPERTS))[:, None]
    w_shared = (w_s_gate, w_s_up, w_s_down)
    w_routed = (w_e_gate, w_e_up, w_e_down)
    b_mod3 = b_mod[:, None, :]

    for layer in range(depth):
        last = layer == depth - 1
        n_rows = n_lat if last else n
        mods = _mod_call(cs, w_mod, b_mod3, layer).reshape(8, 6, d)
        lat, cx = mods[:b], jnp.broadcast_to(mods[b:b + 1], (b, 6, d))
        both = jnp.stack([lat, cx], axis=1)
        mod1, modm, gt2 = both[:, :, 0:2], both[:, :, 2:5], both[:, :, 5]
        q_all, k_all, v_all, gts = _in_call(
            x_lat, x_ctx, ctx_tile, mod1, layer, n, n_lat, g3(g_attn), w_in_p, g3(g_cq), w_uq_p,
            g3(g_ckv), w_uk_p, w_uvt, w_vbt, gains, tables)
        o_lat = _attn_lat_call(q_all, k_all, v_all, n_lat)
        o_ctx = o_lat if last else _attn_ctx_call(q_all, k_all, v_all, n_lat)
        xn, fext, counts = _mid_call(o_lat, o_ctx, gts, x_lat, x_ctx, ctx_tile, modm, layer, n_rows,
                                     n_lat, w_oa_b, w_ob_b, w_out_b, g3(g_ffn), wr_hi, wr_lo, br_col)
        n_tok = b * n_rows
        n_tiles_max = n_tok // MOE_TILE + N_PAIR_CLASSES
        fext = fext.reshape(n_tok, F_EXT)
        plan = _route_plan(fext[:, D_MODEL:D_MODEL + 4], counts, n_tiles_max)
        ys = _moe_call(plan, fext, layer, w_shared, w_routed)
        x_lat = _gather_call(plan[0], xn, gt2, ys, n_lat)
        x_ctx, ctx_tile = x_lat, n_lat // ROW_TILE
    return x_lat
```

```python
import functools

import numpy as np
import jax
import jax.numpy as jnp
from jax import lax
from jax.experimental import pallas as pl
from jax.experimental.pallas import tpu as pltpu

D_MODEL = 1024
GRID_W = 64
MLA_HEADS = 8
MLA_NOPE = 64
MLA_ROPE = 32
MLA_QK = MLA_NOPE + MLA_ROPE
MLA_V = 64
Q_LORA = 256
KV_LORA = 128
GQA_HEADS = 8
GQA_KV_HEADS = 2
GQA_GROUP = GQA_HEADS // GQA_KV_HEADS
GQA_HD = 64
N_EXPERTS = 16
N_GROUPS = 4
EXPERTS_PER_GROUP = N_EXPERTS // N_GROUPS
FF_EXPERT = 512
ROPE_THETA = 10000.0
EPS = 1e-6
IN_SIZES = (Q_LORA, KV_LORA, MLA_ROPE, GQA_HEADS * GQA_HD, GQA_KV_HEADS * GQA_HD,
            GQA_KV_HEADS * GQA_HD, 2 * D_MODEL)

LANES = 128
ROW_TILE = 256
Q_TILE = 2048
KV_CHUNK = 512
N_HEADS_ALL = MLA_HEADS + GQA_HEADS
N_KHEADS_ALL = MLA_HEADS + GQA_KV_HEADS
LOG2_E = 1.4426950408889634
ONES_ROW = MLA_V
V_ROWS = MLA_V + 16
VMEM_LIMIT = 56 * 1024 * 1024
MOE_TILE = 256
MOE_AHEAD = 5
N_CLASSES = N_GROUPS << EXPERTS_PER_GROUP
N_PAIR_CLASSES = N_GROUPS * 6
F_EXT = D_MODEL + LANES
META_CLS, META_RANK, META_W_LO, META_W_HI = 0, 1, 2, 3

C_CQ = 0
C_CKV = C_CQ + Q_LORA
C_KR = C_CKV + KV_LORA
C_QB = C_KR + 2 * LANES
C_KB = C_QB + 2 * GQA_HEADS * LANES
C_G = C_KB + 2 * GQA_KV_HEADS * LANES
C_END = C_G + 2 * D_MODEL

BF16 = jnp.bfloat16
F32 = jnp.float32


def _dot(a, b):
    return jnp.dot(a, b, preferred_element_type=F32)


def _dot_nt(a, b):
    return lax.dot_general(a, b, (((1,), (1,)), ((), ())), preferred_element_type=F32)


def _split_bf16(x):
    hi = x.astype(BF16)
    lo = (x - hi.astype(F32)).astype(BF16)
    return hi, lo


def _rms_rows(x, g):
    ms = jnp.mean(x * x, axis=-1, keepdims=True)
    return x * lax.rsqrt(ms + EPS) * g


def _norm_rope(raw, swapped, gain_cos, gain_sin, dim):
    ms = jnp.sum(raw * raw, axis=-1, keepdims=True) * (1.0 / dim)
    return (raw * gain_cos + swapped * gain_sin) * lax.rsqrt(ms + EPS)


def _mod_kernel(c_ref, w_ref, b_ref, o_ref):
    c = c_ref[...]
    s_hi, s_lo = _split_bf16(c * jax.nn.sigmoid(c))
    w_hi, w_lo = _split_bf16(w_ref[0])
    o_ref[...] = _dot(s_hi, w_hi) + _dot(s_lo, w_hi) + _dot(s_hi, w_lo) + b_ref[0]


def _mod_call(cs, w_mod, b_mod, layer):
    n_cols = w_mod.shape[2]
    tn = 1536
    return pl.pallas_call(
        _mod_kernel,
        grid=(n_cols // tn,),
        in_specs=[
            pl.BlockSpec(cs.shape, lambda j: (0, 0)),
            pl.BlockSpec((1, D_MODEL, tn), lambda j: (layer, 0, j)),
            pl.BlockSpec((1, 1, tn), lambda j: (layer, 0, j)),
        ],
        out_specs=pl.BlockSpec((cs.shape[0], tn), lambda j: (0, j)),
        out_shape=jax.ShapeDtypeStruct((cs.shape[0], n_cols), F32),
        compiler_params=pltpu.CompilerParams(
            dimension_semantics=("arbitrary",), vmem_limit_bytes=VMEM_LIMIT),
        name="mod",
    )(cs, w_mod, b_mod)


def _in_kernel(x_ref, xc_ref, mod_ref, gattn_ref, win_ref, gcq_ref, wuq_ref, gckv_ref, wuk_ref,
               wuvt_ref, wvbt_ref, gains_ref, cosa_ref, sina_ref, cosb_ref, sinb_ref,
               q_ref, k_ref, v_ref, g_ref, *, n_lat_tiles):
    x = jnp.where(pl.program_id(1) >= n_lat_tiles, xc_ref[0], x_ref[0])
    shift = mod_ref[0, 0, 0:1, :]
    scale = mod_ref[0, 0, 1:2, :]
    h = (_rms_rows(x, gattn_ref[0]) * (1 + scale) + shift).astype(BF16)

    g_ref[0] = _dot(h, win_ref[0, :, C_G:C_END])

    gains = gains_ref[0]
    cos_a, sin_a, cos_b, sin_b = cosa_ref[...], sina_ref[...], cosb_ref[...], sinb_ref[...]
    scale_a = MLA_QK ** -0.5 * LOG2_E
    scale_b = GQA_HD ** -0.5 * LOG2_E
    qa_cos, qa_sin = cos_a * (gains[0:1] * scale_a), sin_a * (gains[1:2] * scale_a)
    ka_cos, ka_sin = cos_a * gains[2:3], sin_a * gains[3:4]
    qb_cos, qb_sin = cos_b * (gains[4:5] * scale_b), sin_b * (gains[5:6] * scale_b)
    kb_cos, kb_sin = cos_b * gains[6:7], sin_b * gains[7:8]
    t = x.shape[0]
    ones_row = lax.broadcasted_iota(jnp.int32, (V_ROWS, t), 0) == ONES_ROW
    blk = lambda a, i: a[:, i * LANES:(i + 1) * LANES]

    p0 = _dot(h, win_ref[0, :, C_CQ:C_QB])
    cq = p0[:, C_CQ:C_CKV]
    ckv = p0[:, C_CKV:C_KR]
    kr_blk = p0[:, C_KR:C_KR + LANES]
    kr_swapped = p0[:, C_KR + LANES:C_QB]
    qa = _dot(_rms_rows(cq, gcq_ref[0]).astype(BF16), wuq_ref[0])
    ckv_n = _rms_rows(ckv, gckv_ref[0]).astype(BF16)
    ka = _dot(ckv_n, wuk_ref[0])
    va_t = _dot_nt(wuvt_ref[0], ckv_n)
    for hd in range(MLA_HEADS):
        q_ref[0, hd] = _norm_rope(blk(qa, hd), blk(qa, MLA_HEADS + hd), qa_cos, qa_sin,
                                  MLA_QK).astype(BF16)
        k_ref[0, hd] = _norm_rope(blk(ka, hd) + kr_blk, kr_swapped, ka_cos, ka_sin,
                                  MLA_QK).astype(BF16)
        v_ref[0, hd, 0] = jnp.where(ones_row, 1.0, va_t[hd * V_ROWS:(hd + 1) * V_ROWS, :]).astype(BF16)

    pq = _dot(h, win_ref[0, :, C_QB:C_KB])
    for hd in range(GQA_HEADS):
        q_ref[0, MLA_HEADS + hd] = _norm_rope(blk(pq, hd), blk(pq, GQA_HEADS + hd), qb_cos, qb_sin,
                                              GQA_HD).astype(BF16)
    pk = _dot(h, win_ref[0, :, C_KB:C_G])
    vb_t = _dot_nt(wvbt_ref[0], h)
    for hd in range(GQA_KV_HEADS):
        k_ref[0, MLA_HEADS + hd] = _norm_rope(blk(pk, hd), blk(pk, GQA_KV_HEADS + hd), kb_cos,
                                              kb_sin, GQA_HD).astype(BF16)
        v_ref[0, MLA_HEADS + hd, 0] = jnp.where(
            ones_row, 1.0, vb_t[hd * V_ROWS:(hd + 1) * V_ROWS, :]).astype(BF16)


def _stream_specs(n_lat_tiles, ctx_tile):
    t = ROW_TILE
    return [pl.BlockSpec((1, t, D_MODEL), lambda i, j: (i, jnp.minimum(j, n_lat_tiles - 1), 0)),
            pl.BlockSpec((1, t, D_MODEL), lambda i, j: (i, ctx_tile, 0))]


def _in_call(x_lat, x_ctx, ctx_tile, mod1, layer, n, n_lat, g_attn, w_in_p, g_cq, w_uq_p, g_ckv,
             w_uk_p, w_uvt, w_vbt, gains, tables):
    b = x_lat.shape[0]
    nt = n // ROW_TILE
    n_lat_tiles = n_lat // ROW_TILE
    t = ROW_TILE

    def lspec(arr):
        shp = arr.shape
        return pl.BlockSpec((1,) + shp[1:], lambda i, j: (layer,) + (0,) * (len(shp) - 1))

    tab_spec = pl.BlockSpec((t, LANES), lambda i, j: (j, 0))
    return pl.pallas_call(
        functools.partial(_in_kernel, n_lat_tiles=n_lat_tiles),
        grid=(b, nt),
        in_specs=_stream_specs(n_lat_tiles, ctx_tile) + [
            pl.BlockSpec((1, 1, 2, D_MODEL), lambda i, j: (i, jnp.where(j >= n_lat_tiles, 1, 0), 0, 0)),
            lspec(g_attn), lspec(w_in_p), lspec(g_cq), lspec(w_uq_p), lspec(g_ckv),
            lspec(w_uk_p), lspec(w_uvt), lspec(w_vbt), lspec(gains),
        ] + [tab_spec] * 4,
        out_specs=[
            pl.BlockSpec((1, N_HEADS_ALL, t, LANES), lambda i, j: (i, 0, j, 0)),
            pl.BlockSpec((1, N_KHEADS_ALL, t, LANES), lambda i, j: (i, 0, j, 0)),
            pl.BlockSpec((1, N_KHEADS_ALL, 1, V_ROWS, t), lambda i, j: (i, 0, j, 0, 0)),
            pl.BlockSpec((1, t, 2 * D_MODEL), lambda i, j: (i, j, 0)),
        ],
        out_shape=[
            jax.ShapeDtypeStruct((b, N_HEADS_ALL, n, LANES), BF16),
            jax.ShapeDtypeStruct((b, N_KHEADS_ALL, n, LANES), BF16),
            jax.ShapeDtypeStruct((b, N_KHEADS_ALL, nt, V_ROWS, t), BF16),
            jax.ShapeDtypeStruct((b, n, 2 * D_MODEL), F32),
        ],
        compiler_params=pltpu.CompilerParams(
            dimension_semantics=("arbitrary", "arbitrary"), vmem_limit_bytes=VMEM_LIMIT),
        name="mixer_in",
    )(x_lat, x_ctx, mod1, g_attn, w_in_p, g_cq, w_uq_p, g_ckv, w_uk_p, w_uvt, w_vbt, gains, *tables)


def _attn_kernel(q_ref, k0_ref, k1_ref, v0_ref, v1_ref, *rest, n_main, tail_chunk, pipelined):
    if pipelined:
        o_ref, s_scr, acc_scr = rest[-3:]
    else:
        o_ref, acc_scr = rest[-2:]
    qs = (q_ref[0, 0], q_ref[0, 1])
    k_refs = (k0_ref, k1_ref)
    v_refs = (v0_ref, v1_ref)
    tq = qs[0].shape[0]
    per = KV_CHUNK // ROW_TILE

    def fold(hh, m, cmax, st, chunk0, n_sub, first=False):
        m_new = cmax if first else jnp.maximum(m, cmax)
        pt = jnp.exp2(st - m_new).astype(BF16)
        pv = _dot(v_refs[hh][0, 0, chunk0], pt[0:ROW_TILE])
        for u in range(1, n_sub):
            pv = pv + _dot(v_refs[hh][0, 0, chunk0 + u], pt[u * ROW_TILE:(u + 1) * ROW_TILE])
        if first:
            acc_scr[hh] = pv
        else:
            acc_scr[hh] = jnp.exp2(m - m_new) * acc_scr[hh] + pv
        return m_new

    def produce(c, slot):
        cmax = []
        for hh in range(2):
            off = c * KV_CHUNK
            if not isinstance(off, int):
                off = pl.multiple_of(off, KV_CHUNK)
            st = _dot_nt(k_refs[hh][0, 0, pl.ds(off, KV_CHUNK), :], qs[hh])
            s_scr[slot, hh] = st
            cmax.append(jnp.max(st, axis=0, keepdims=True))
        return tuple(cmax)

    def consume(c, slot, ms, cmax):
        return tuple(fold(hh, ms[hh], cmax[hh], s_scr[slot, hh], c * per, per) for hh in range(2))

    ms = []
    for hh in range(2):
        st = _dot_nt(k_refs[hh][0, 0, pl.ds(tail_chunk * ROW_TILE, ROW_TILE), :], qs[hh])
        ms.append(fold(hh, None, jnp.max(st, axis=0, keepdims=True), st, tail_chunk, 1, first=True))
    ms = tuple(ms)

    if pipelined:
        def body(i, carry):
            ms, cmax = carry
            cmax1 = produce(2 * i + 1, 1)
            ms = consume(2 * i, 0, ms, cmax)
            cmax2 = produce(2 * i + 2, 0)
            ms = consume(2 * i + 1, 1, ms, cmax1)
            return ms, cmax2

        ms, cmax = lax.fori_loop(0, (n_main - 2) // 2, body, (ms, produce(0, 0)))
        cmax1 = produce(n_main - 1, 1)
        ms = consume(n_main - 2, 0, ms, cmax)
        ms = consume(n_main - 1, 1, ms, cmax1)

    outs = []
    for hh in range(2):
        acc = acc_scr[hh]
        o_t = jnp.concatenate([acc / acc[ONES_ROW:ONES_ROW + 1, :],
                               jnp.zeros((LANES - V_ROWS, tq), F32)], axis=0)
        outs.append(o_t.T)
    low = lax.broadcasted_iota(jnp.int32, (tq, LANES), 1) < MLA_V
    o_ref[0] = jnp.where(low, outs[0], pltpu.roll(outs[1], MLA_V, 1)).astype(o_ref.dtype)


def _kv_head(p):
    mla_pairs = MLA_HEADS // 2
    pairs_per_kv = GQA_GROUP // 2
    gqa = MLA_HEADS + (p - mla_pairs) // pairs_per_kv
    return jnp.where(p < mla_pairs, 2 * p, gqa), jnp.where(p < mla_pairs, 2 * p + 1, gqa)


def _attn_lat_call(q_all, k_all, v_all, n_lat):
    b, _, n, _ = q_all.shape
    n_pairs = N_HEADS_ALL // 2
    n_chunks = n // ROW_TILE
    n_main = n_lat // KV_CHUNK
    assert n_main >= 2 and n_main % 2 == 0
    kern = functools.partial(_attn_kernel, n_main=n_main, tail_chunk=n_lat // ROW_TILE,
                             pipelined=True)
    k_spec = lambda which: pl.BlockSpec(
        (1, 1, n, LANES), lambda i, p, j: (i, _kv_head(p)[which], 0, 0))
    v_spec = lambda which: pl.BlockSpec(
        (1, 1, n_chunks, V_ROWS, ROW_TILE), lambda i, p, j: (i, _kv_head(p)[which], 0, 0, 0))
    return pl.pallas_call(
        kern,
        grid=(b, n_pairs, n_lat // Q_TILE),
        in_specs=[pl.BlockSpec((1, 2, Q_TILE, LANES), lambda i, p, j: (i, p, j, 0)),
                  k_spec(0), k_spec(1), v_spec(0), v_spec(1)],
        out_specs=pl.BlockSpec((1, Q_TILE, LANES), lambda i, p, j: (i, j, p)),
        out_shape=jax.ShapeDtypeStruct((b, n_lat, n_pairs * LANES), BF16),
        scratch_shapes=[pltpu.VMEM((2, 2, KV_CHUNK, Q_TILE), F32),
                        pltpu.VMEM((2, V_ROWS, Q_TILE), F32)],
        compiler_params=pltpu.CompilerParams(
            dimension_semantics=("arbitrary", "arbitrary", "arbitrary"),
            vmem_limit_bytes=VMEM_LIMIT),
        name="attention",
    )(q_all, k_all, k_all, v_all, v_all)


def _attn_ctx_call(q_all, k_all, v_all, n_lat):
    b, _, n, _ = q_all.shape
    n_pairs = N_HEADS_ALL // 2
    c = n_lat // ROW_TILE
    assert n - n_lat == ROW_TILE
    kern = functools.partial(_attn_kernel, n_main=0, tail_chunk=0, pipelined=False)
    k_spec = lambda which: pl.BlockSpec(
        (1, 1, ROW_TILE, LANES), lambda i, p: (i, _kv_head(p)[which], c, 0))
    v_spec = lambda which: pl.BlockSpec(
        (1, 1, 1, V_ROWS, ROW_TILE), lambda i, p: (i, _kv_head(p)[which], c, 0, 0))
    return pl.pallas_call(
        kern,
        grid=(b, n_pairs),
        in_specs=[pl.BlockSpec((1, 2, ROW_TILE, LANES), lambda i, p: (i, p, c, 0)),
                  k_spec(0), k_spec(1), v_spec(0), v_spec(1)],
        out_specs=pl.BlockSpec((1, ROW_TILE, LANES), lambda i, p: (i, 0, p)),
        out_shape=jax.ShapeDtypeStruct((b, ROW_TILE, n_pairs * LANES), BF16),
        scratch_shapes=[pltpu.VMEM((2, V_ROWS, ROW_TILE), F32)],
        compiler_params=pltpu.CompilerParams(
            dimension_semantics=("arbitrary", "arbitrary"), vmem_limit_bytes=VMEM_LIMIT),
        name="attention_ctx",
    )(q_all, k_all, k_all, v_all, v_all)


def _route_rows(sel, scores):
    rows = [sel[e:e + 1, :] for e in range(N_EXPERTS)]
    srow = [scores[e:e + 1, :] for e in range(N_EXPERTS)]
    grp = []
    for g in range(N_GROUPS):
        a, b, c, d = rows[4 * g:4 * g + 4]
        hi1, lo1 = jnp.maximum(a, b), jnp.minimum(a, b)
        hi2, lo2 = jnp.maximum(c, d), jnp.minimum(c, d)
        top1 = jnp.maximum(hi1, hi2)
        top2 = jnp.maximum(jnp.minimum(hi1, hi2), jnp.maximum(lo1, lo2))
        grp.append(top1 + top2)
    best = jnp.zeros_like(grp[0], dtype=jnp.int32)
    best_v = grp[0]
    for g in range(1, N_GROUPS):
        upd = grp[g] > best_v
        best = jnp.where(upd, g, best)
        best_v = jnp.where(upd, grp[g], best_v)
    picked = []
    keeps = []
    for e in range(N_EXPERTS):
        g = e // EXPERTS_PER_GROUP
        rank = jnp.zeros_like(best)
        for o in range(4 * g, 4 * g + 4):
            if o == e:
                continue
            ahead = rows[o] > rows[e]
            if o < e:
                ahead = ahead | (rows[o] == rows[e])
            rank = rank + ahead.astype(jnp.int32)
        keep = (rank < 2) & (best == g)
        picked.append(jnp.where(keep, srow[e], 0.0))
        keeps.append(keep)
    total = picked[0]
    for e in range(1, N_EXPERTS):
        total = total + picked[e]
    return [p / total for p in picked], best, keeps


def _route_meta(gates, best, keeps):
    mask = jnp.zeros_like(best)
    w_lo = jnp.zeros_like(gates[0])
    w_hi = jnp.zeros_like(gates[0])
    for e in range(N_EXPERTS):
        g, i = divmod(e, EXPERTS_PER_GROUP)
        mask = mask + jnp.where(keeps[e], 1 << i, 0)
        if i == 0:
            w_lo = w_lo + jnp.where(keeps[e], gates[e], 0.0)
            continue
        below = keeps[4 * g]
        for o in range(4 * g + 1, e):
            below = below | keeps[o]
        w_lo = w_lo + jnp.where(keeps[e] & ~below, gates[e], 0.0)
        w_hi = w_hi + jnp.where(keeps[e] & below, gates[e], 0.0)
    cls = (best * (1 << EXPERTS_PER_GROUP) + mask).astype(F32)
    return cls, w_lo, w_hi


def _mid_kernel(o_ref, octx_ref, g_ref, x_ref, xc_ref, mod_ref, woa_ref, wob_ref, wout_ref, gffn_ref,
                wrh_ref, wrl_ref, brc_ref, xn_ref, fext_ref, cnt_ref, run_ref, *, n_lat_tiles):
    is_ctx = pl.program_id(1) >= n_lat_tiles
    o = jnp.where(is_ctx, octx_ref[0], o_ref[0])
    n_a = MLA_HEADS * MLA_V
    ya = _dot(o[:, 0:n_a], woa_ref[0])
    yb = _dot(o[:, n_a:], wob_ref[0])
    g = g_ref[0]
    y = jax.nn.sigmoid(g[:, 0:D_MODEL]) * ya + jax.nn.sigmoid(g[:, D_MODEL:]) * yb
    z = _dot(y.astype(BF16), wout_ref[0])
    gt1 = mod_ref[0, 0, 0:1, :]
    sh2 = mod_ref[0, 0, 1:2, :]
    sc2 = mod_ref[0, 0, 2:3, :]
    xn = jnp.where(is_ctx, xc_ref[0], x_ref[0]) + gt1 * z
    xn_ref[0] = xn
    f = _rms_rows(xn, gffn_ref[0]) * (1 + sc2) + sh2
    fext_ref[0, :, 0:D_MODEL] = f
    f_hi, f_lo = _split_bf16(f)
    w_hi = wrh_ref[...]
    logits = _dot(f_hi, w_hi) + _dot(f_lo, w_hi) + _dot(f_hi, wrl_ref[...])
    scores_t = jax.nn.sigmoid(logits).T
    scores = scores_t[0:N_EXPERTS, :]
    sel = scores + brc_ref[0:N_EXPERTS, :]
    cls, w_lo, w_hi_gate = _route_meta(*_route_rows(sel, scores))
    t = scores.shape[1]
    row_id = lax.broadcasted_iota(jnp.int32, (LANES, t), 0)
    meta_t = jnp.where(row_id == META_CLS, cls, 0.0)
    meta_t = jnp.where(row_id == META_W_LO, w_lo, meta_t)
    meta_t = jnp.where(row_id == META_W_HI, w_hi_gate, meta_t)
    meta = meta_t.T

    @pl.when((pl.program_id(0) == 0) & (pl.program_id(1) == 0))
    def _():
        run_ref[...] = jnp.zeros_like(run_ref)

    lane = lax.broadcasted_iota(jnp.int32, (t, LANES), 1)
    onehot = jnp.where(lane.astype(F32) == meta[:, META_CLS:META_CLS + 1], 1.0, 0.0)
    earlier = jnp.where(lax.broadcasted_iota(jnp.int32, (t, t), 0)
                        > lax.broadcasted_iota(jnp.int32, (t, t), 1), 1.0, 0.0).astype(BF16)
    before = _dot(earlier, onehot.astype(BF16)) + run_ref[...]
    rank = jnp.sum(before * onehot, axis=-1, keepdims=True)
    fext_ref[0, :, D_MODEL:] = jnp.where(lane == META_RANK, rank, meta)
    run_ref[...] += jnp.sum(onehot, axis=0, keepdims=True)
    cnt_ref[...] = jnp.broadcast_to(run_ref[...], cnt_ref.shape)


def _mid_call(o_lat, o_ctx, gts, x_lat, x_ctx, ctx_tile, modm, layer, n_rows, n_lat, w_oa, w_ob, w_out,
              g_ffn, wr_hi, wr_lo, br_col):
    b = x_lat.shape[0]
    n = n_rows
    n_lat_tiles = n_lat // ROW_TILE
    t = ROW_TILE

    def lspec(arr):
        shp = arr.shape
        return pl.BlockSpec((1,) + shp[1:], lambda i, j: (layer,) + (0,) * (len(shp) - 1))

    def full(arr):
        return pl.BlockSpec(arr.shape, lambda i, j: (0,) * arr.ndim)

    return pl.pallas_call(
        functools.partial(_mid_kernel, n_lat_tiles=n_lat_tiles),
        grid=(b, n // t),
        in_specs=[
            pl.BlockSpec((1, t, o_lat.shape[2]), lambda i, j: (i, jnp.minimum(j, n_lat_tiles - 1), 0)),
            pl.BlockSpec((1, t, o_ctx.shape[2]), lambda i, j: (i, 0, 0)),
            pl.BlockSpec((1, t, 2 * D_MODEL), lambda i, j: (i, j, 0)),
        ] + _stream_specs(n_lat_tiles, ctx_tile) + [
            pl.BlockSpec((1, 1, 3, D_MODEL), lambda i, j: (i, jnp.where(j >= n_lat_tiles, 1, 0), 0, 0)),
            lspec(w_oa), lspec(w_ob), lspec(w_out), lspec(g_ffn),
            full(wr_hi), full(wr_lo), full(br_col),
        ],
        out_specs=[
            pl.BlockSpec((1, t, D_MODEL), lambda i, j: (i, j, 0)),
            pl.BlockSpec((1, t, F_EXT), lambda i, j: (i, j, 0)),
            pl.BlockSpec((8, LANES), lambda i, j: (0, 0)),
        ],
        out_shape=[
            jax.ShapeDtypeStruct((b, n, D_MODEL), F32),
            jax.ShapeDtypeStruct((b, n, F_EXT), F32),
            jax.ShapeDtypeStruct((8, LANES), F32),
        ],
        scratch_shapes=[pltpu.VMEM((1, LANES), F32)],
        compiler_params=pltpu.CompilerParams(
            dimension_semantics=("arbitrary", "arbitrary"), vmem_limit_bytes=VMEM_LIMIT),
        name="mixer_out_router",
    )(o_lat, o_ctx, gts, x_lat, x_ctx, modm, w_oa, w_ob, w_out, g_ffn, wr_hi, wr_lo, br_col)


def _class_expert_tables():
    lo = np.zeros((N_CLASSES,), np.int32)
    hi = np.zeros((N_CLASSES,), np.int32)
    for c in range(N_CLASSES):
        g, mask = divmod(c, 1 << EXPERTS_PER_GROUP)
        bits = [i for i in range(EXPERTS_PER_GROUP) if mask >> i & 1]
        if len(bits) == 2:
            lo[c], hi[c] = EXPERTS_PER_GROUP * g + bits[0], EXPERTS_PER_GROUP * g + bits[1]
    return lo, hi


def _route_plan(meta, counts, n_tiles_max):
    cls = meta[:, META_CLS].astype(jnp.int32)
    rank = meta[:, META_RANK].astype(jnp.int32)
    cnt = counts[0, :N_CLASSES].astype(jnp.int32)
    tiles = (cnt + MOE_TILE - 1) // MOE_TILE
    tile_end = jnp.cumsum(tiles)
    start = (tile_end - tiles) * MOE_TILE
    hit = cls[None, :] == jnp.arange(N_CLASSES, dtype=jnp.int32)[:, None]
    pos = jnp.sum(jnp.where(hit, start[:, None], 0), axis=0) + rank
    n_used = tile_end[-1]
    tile_idx = jnp.minimum(jnp.arange(n_tiles_max, dtype=jnp.int32), n_used - 1)
    tile_cls = jnp.sum(tile_end[None, :] <= tile_idx[:, None], axis=1).astype(jnp.int32)
    lo, hi = _class_expert_tables()
    src = jnp.zeros((n_tiles_max * MOE_TILE,), jnp.int32).at[pos].set(
        jnp.arange(pos.shape[0], dtype=jnp.int32))
    return (pos.astype(jnp.int32), src, tile_idx, jnp.asarray(lo)[tile_cls],
            jnp.asarray(hi)[tile_cls], n_used.reshape(1).astype(jnp.int32))


def _row_copy(src_ref, src_row, dst_ref, dst_row, sem):
    return pltpu.make_async_copy(src_ref.at[pl.ds(src_row, 1)], dst_ref.at[pl.ds(dst_row, 1)], sem)


def _moe_kernel(src_ref, tidx_ref, elo_ref, ehi_ref, nused_ref, f_hbm, wgs_ref, wus_ref, wds_ref,
                wg1_ref, wu1_ref, wd1_ref, wg2_ref, wu2_ref, wd2_ref, ys_ref, buf, sems):
    del elo_ref, ehi_ref
    j = pl.program_id(0)
    n_used = nused_ref[0]
    slot = j % (MOE_AHEAD + 1)

    def fetch(step):
        into = step % (MOE_AHEAD + 1)
        base = tidx_ref[step] * MOE_TILE
        for r in range(MOE_TILE):
            _row_copy(f_hbm, src_ref[base + r], buf.at[into], r, sems.at[into]).start()

    for first in range(MOE_AHEAD):
        pl.when((j == 0) & (first < n_used))(functools.partial(fetch, first))
    pl.when(j + MOE_AHEAD < n_used)(lambda: fetch(j + MOE_AHEAD))
    used = j < n_used

    @pl.when(jnp.logical_not(used))
    def _():
        ys_ref[...] = jnp.zeros_like(ys_ref)

    @pl.when(used)
    def _():
        for _ in range(MOE_TILE):
            _row_copy(f_hbm, 0, buf.at[slot], 0, sems.at[slot]).wait()
        rows = buf[slot]
        x = rows[:, 0:D_MODEL].astype(BF16)
        w_lo = rows[:, D_MODEL + META_W_LO:D_MODEL + META_W_LO + 1]
        w_hi = rows[:, D_MODEL + META_W_HI:D_MODEL + META_W_HI + 1]

        def ffn(wg_ref, wu_ref, wd_ref):
            wg, wu, wd = (r[(0,) * (len(r.shape) - 2)].astype(BF16) for r in (wg_ref, wu_ref, wd_ref))
            a = jax.nn.silu(_dot(x, wg)) * _dot(x, wu)
            return _dot(a.astype(BF16), wd)

        ys_ref[...] = (ffn(wgs_ref, wus_ref, wds_ref) + w_lo * ffn(wg1_ref, wu1_ref, wd1_ref)
                       + w_hi * ffn(wg2_ref, wu2_ref, wd2_ref))


def _moe_call(plan, fext, layer, w_s, w_e):
    _, src, tile_idx, e_lo, e_hi, n_used = plan
    n_tiles = tile_idx.shape[0]

    def routed_spec(w, which):
        return pl.BlockSpec((1, 1) + w.shape[2:], lambda j, s, t, lo, hi, nu: (layer, which(lo, hi)[j], 0, 0))

    def shared_spec(w):
        return pl.BlockSpec((1,) + w.shape[1:], lambda j, s, t, lo, hi, nu: (layer, 0, 0))

    first = lambda lo, hi: lo
    second = lambda lo, hi: hi
    return pl.pallas_call(
        _moe_kernel,
        grid_spec=pltpu.PrefetchScalarGridSpec(
            num_scalar_prefetch=5,
            grid=(n_tiles,),
            in_specs=([pl.BlockSpec(memory_space=pl.ANY)] + [shared_spec(w) for w in w_s]
                      + [routed_spec(w, first) for w in w_e] + [routed_spec(w, second) for w in w_e]),
            out_specs=pl.BlockSpec((MOE_TILE, D_MODEL), lambda j, s, t, lo, hi, nu: (j, 0)),
            scratch_shapes=[pltpu.VMEM((MOE_AHEAD + 1, MOE_TILE, F_EXT), F32),
                            pltpu.SemaphoreType.DMA((MOE_AHEAD + 1,))],
        ),
        out_shape=jax.ShapeDtypeStruct((n_tiles * MOE_TILE, D_MODEL), F32),
        compiler_params=pltpu.CompilerParams(
            dimension_semantics=("arbitrary",), vmem_limit_bytes=VMEM_LIMIT),
        name="moe",
    )(src, tile_idx, e_lo, e_hi, n_used, fext, *w_s, *w_e, *w_e)


def _gather_kernel(pos_ref, xn_ref, gt_ref, ys_hbm, o_ref, buf, sems):
    nt = pl.num_programs(1)
    step = pl.program_id(0) * nt + pl.program_id(1)
    total = pl.num_programs(0) * nt
    slot = step % (MOE_AHEAD + 1)

    def fetch(s):
        into = s % (MOE_AHEAD + 1)
        for r in range(MOE_TILE):
            _row_copy(ys_hbm, pos_ref[s * MOE_TILE + r], buf.at[into], r, sems.at[into]).start()

    for first in range(MOE_AHEAD):
        pl.when((step == 0) & (first < total))(functools.partial(fetch, first))
    pl.when(step + MOE_AHEAD < total)(lambda: fetch(step + MOE_AHEAD))
    for _ in range(MOE_TILE):
        _row_copy(ys_hbm, 0, buf.at[slot], 0, sems.at[slot]).wait()
    o_ref[0] = xn_ref[0] + gt_ref[0, 0] * buf[slot]


def _gather_call(pos, xn, gt2, ys, n_lat):
    b, n, _ = xn.shape
    n_lat_tiles = n_lat // MOE_TILE
    return pl.pallas_call(
        _gather_kernel,
        grid_spec=pltpu.PrefetchScalarGridSpec(
            num_scalar_prefetch=1,
            grid=(b, n // MOE_TILE),
            in_specs=[
                pl.BlockSpec((1, MOE_TILE, D_MODEL), lambda i, j, p: (i, j, 0)),
                pl.BlockSpec((1, 1, 1, D_MODEL),
                             lambda i, j, p: (i, jnp.where(j >= n_lat_tiles, 1, 0), 0, 0)),
                pl.BlockSpec(memory_space=pl.ANY),
            ],
            out_specs=pl.BlockSpec((1, MOE_TILE, D_MODEL), lambda i, j, p: (i, j, 0)),
            scratch_shapes=[pltpu.VMEM((MOE_AHEAD + 1, MOE_TILE, D_MODEL), F32),
                            pltpu.SemaphoreType.DMA((MOE_AHEAD + 1,))],
        ),
        out_shape=jax.ShapeDtypeStruct((b, n, D_MODEL), F32),
        compiler_params=pltpu.CompilerParams(
            dimension_semantics=("arbitrary", "arbitrary"), vmem_limit_bytes=VMEM_LIMIT),
        name="moe_gather_residual",
    )(pos, xn, gt2[:, :, None, :], ys)


def _pad_heads(w, n_heads, dim, width=LANES):
    lead = w.shape[:-1]
    w = w.reshape(lead + (n_heads, dim))
    w = jnp.pad(w, [(0, 0)] * len(lead) + [(0, 0), (0, width - dim)])
    return w.reshape(lead + (n_heads * width,))


def _partner_lanes(lane0, half_dim):
    perm = np.arange(LANES)
    quarter = half_dim // 2
    for part in range(2):
        lo = lane0 + part * half_dim
        perm[lo:lo + quarter] = np.arange(lo + quarter, lo + half_dim)
        perm[lo + quarter:lo + half_dim] = np.arange(lo, lo + quarter)
    return perm


PARTNER_MLA = _partner_lanes(MLA_NOPE, MLA_ROPE // 2)
PARTNER_GQA = _partner_lanes(0, GQA_HD // 2)


def _swap_partners(w, perm):
    lead = w.shape[:-1]
    return w.reshape(lead + (-1, LANES))[..., perm].reshape(w.shape)


def _prep_w_in(w_in):
    offs = np.concatenate([[0], np.cumsum(IN_SIZES)])
    cq, ckv, kr, qb, kb, vb, gts = [w_in[..., int(offs[i]):int(offs[i + 1])] for i in range(7)]
    kr_blk = jnp.pad(kr, [(0, 0), (0, 0), (MLA_NOPE, LANES - MLA_QK)])
    qb_p, kb_p = _pad_heads(qb, GQA_HEADS, GQA_HD), _pad_heads(kb, GQA_KV_HEADS, GQA_HD)
    w_main = jnp.concatenate(
        [cq, ckv, kr_blk, _swap_partners(kr_blk, PARTNER_MLA), qb_p, _swap_partners(qb_p, PARTNER_GQA),
         kb_p, _swap_partners(kb_p, PARTNER_GQA), gts], axis=-1).astype(BF16)
    w_vbt = jnp.swapaxes(_pad_heads(vb, GQA_KV_HEADS, GQA_HD, V_ROWS), 1, 2).astype(BF16)
    return w_main, w_vbt


def _prep_w_ukv(w_ukv):
    l, r, _ = w_ukv.shape
    w = w_ukv.reshape(l, r, MLA_HEADS, MLA_NOPE + MLA_V)
    uk = _pad_heads(w[..., :MLA_NOPE].reshape(l, r, MLA_HEADS * MLA_NOPE), MLA_HEADS, MLA_NOPE)
    uv = _pad_heads(w[..., MLA_NOPE:].reshape(l, r, MLA_HEADS * MLA_V), MLA_HEADS, MLA_V, V_ROWS)
    return uk.astype(BF16), jnp.swapaxes(uv, 1, 2).astype(BF16)


def _pad_gain(g, dim):
    return jnp.pad(g, [(0, 0), (0, LANES - dim)])


def _gain_rows(g_qa, g_ka, g_qb, g_kb):
    rows = []
    for g, dim, perm in ((g_qa, MLA_QK, PARTNER_MLA), (g_ka, MLA_QK, PARTNER_MLA),
                         (g_qb, GQA_HD, PARTNER_GQA), (g_kb, GQA_HD, PARTNER_GQA)):
        g_p = _pad_gain(g, dim)
        rows += [g_p, g_p[:, perm]]
    return jnp.stack(rows, axis=1)


def _rope_tables(n_ctx, n_lat, lane0, half_dim):
    pos = jnp.arange(n_lat)
    inv = ROPE_THETA ** (-jnp.arange(0, half_dim, 2, dtype=F32) / half_dim)
    ang_r = (pos // GRID_W).astype(F32)[:, None] * inv[None, :]
    ang_c = (pos % GRID_W).astype(F32)[:, None] * inv[None, :]
    cos = jnp.concatenate([jnp.cos(ang_r)] * 2 + [jnp.cos(ang_c)] * 2, axis=-1)
    sin = jnp.concatenate([-jnp.sin(ang_r), jnp.sin(ang_r), -jnp.sin(ang_c), jnp.sin(ang_c)], axis=-1)
    pad = [(0, n_ctx), (lane0, LANES - lane0 - 2 * half_dim)]
    return (jnp.pad(cos, pad, constant_values=1.0), jnp.pad(sin, pad))


def kernel(x, c, ctx, c_ctx, w_mod, b_mod, g_attn, g_ffn, w_in, g_cq, w_uq, g_ckv, w_ukv,
           g_qa, g_ka, g_qb, g_kb, w_oa, w_ob, w_out, w_router, b_router,
           w_e_gate, w_e_up, w_e_down, w_s_gate, w_s_up, w_s_down):
    b, n_lat, d = x.shape
    n_ctx = ctx.shape[1]
    n = n_lat + n_ctx
    depth = w_mod.shape[0]
    assert d == D_MODEL and n_ctx == ROW_TILE and n_lat % KV_CHUNK == 0 and n_lat % Q_TILE == 0

    x_lat, x_ctx, ctx_tile = x, ctx, 0
    cs =jnp.concatenate([c, c_ctx[None, :], jnp.zeros((8 - b - 1, d), F32)], axis=0)

    w_in_p, w_vbt = _prep_w_in(w_in)
    w_uq_p = _pad_heads(w_uq, MLA_HEADS, MLA_QK)
    w_uq_p = jnp.concatenate([w_uq_p, _swap_partners(w_uq_p, PARTNER_MLA)], axis=-1).astype(BF16)
    w_uk_p, w_uvt = _prep_w_ukv(w_ukv)
    g3 = lambda g: g[:, None, :]
    gains = _gain_rows(g_qa, g_ka, g_qb, g_kb)
    tables = (_rope_tables(n_ctx, n_lat, MLA_NOPE, MLA_ROPE // 2)
              + _rope_tables(n_ctx, n_lat, 0, GQA_HD // 2))
    w_oa_b, w_ob_b, w_out_b = w_oa.astype(BF16), w_ob.astype(BF16), w_out.astype(BF16)
    wr_p = jnp.pad(w_router, [(0, 0), (0, LANES - N_EXPERTS)])
    wr_hi = wr_p.astype(BF16)
    wr_lo = (wr_p - wr_hi.astype(F32)).astype(BF16)
    br_col = jnp.pad(b_router, (0, LANES - N_EXPERTS))[:, None]
    w_shared = (w_s_gate, w_s_up, w_s_down)
    w_routed = (w_e_gate, w_e_up, w_e_down)
    b_mod3 = b_mod[:, None, :]

    for layer in range(depth):
        last = layer == depth - 1
        n_rows = n_lat if last else n
        mods = _mod_call(cs, w_mod, b_mod3, layer).reshape(8, 6, d)
        lat, cx = mods[:b], jnp.broadcast_to(mods[b:b + 1], (b, 6, d))
        both = jnp.stack([lat, cx], axis=1)
        mod1, modm, gt2 = both[:, :, 0:2], both[:, :, 2:5], both[:, :, 5]
        q_all, k_all, v_all, gts = _in_call(
            x_lat, x_ctx, ctx_tile, mod1, layer, n, n_lat, g3(g_attn), w_in_p, g3(g_cq), w_uq_p,
            g3(g_ckv), w_uk_p, w_uvt, w_vbt, gains, tables)
        o_lat = _attn_lat_call(q_all, k_all, v_all, n_lat)
        o_ctx = o_lat if last else _attn_ctx_call(q_all, k_all, v_all, n_lat)
        xn, fext, counts = _mid_call(o_lat, o_ctx, gts, x_lat, x_ctx, ctx_tile, modm, layer, n_rows,
                                     n_lat, w_oa_b, w_ob_b, w_out_b, g3(g_ffn), wr_hi, wr_lo, br_col)
        n_tok = b * n_rows
        n_tiles_max = n_tok // MOE_TILE + N_PAIR_CLASSES
        fext = fext.reshape(n_tok, F_EXT)
        plan = _route_plan(fext[:, D_MODEL:D_MODEL + 4], counts, n_tiles_max)
        ys = _moe_call(plan, fext, layer, w_shared, w_routed)
        x_lat = _gather_call(plan[0], xn, gt2, ys, n_lat)
        x_ctx, ctx_tile = x_lat, n_lat // ROW_TILE
    return x_lat
```
